```python
import math
import jax, jax.numpy as jnp
from jax import lax
import numpy as np


D_MODEL = 1024
BATCH = 8
SEQ = 4096
DEPTH = 4

N_HEADS_DIFF = 8
HEAD_DIM_DIFF = 64
V_DIM_DIFF = 2 * HEAD_DIM_DIFF
Q_BLOCK = 128
N_HEADS_GLA = 4
KEY_DIM_GLA = D_MODEL // 2 // N_HEADS_GLA
VAL_DIM_GLA = D_MODEL // N_HEADS_GLA
GLA_RANK = 16
GLA_TAU = 16.0
GLA_CHUNK = 64
N_GROUPS = 4
EXPERTS_PER_GROUP = 8
N_EXPERTS = N_GROUPS * EXPERTS_PER_GROUP
TOP_K_IN_GROUP = 2
D_EXPERT = D_MODEL // 2
MOE_BLOCK = 128
LN_EPS = 1e-5
DEEPNORM_ALPHA = (2.0 * DEPTH) ** 0.25
DEEPNORM_BETA = (8.0 * DEPTH) ** -0.25
W_QA = N_HEADS_DIFF * 2 * HEAD_DIM_DIFF
W_KA = N_HEADS_DIFF * 2 * HEAD_DIM_DIFF
W_VA = N_HEADS_DIFF * V_DIM_DIFF
W_QB = N_HEADS_GLA * KEY_DIM_GLA
W_KB = N_HEADS_GLA * KEY_DIM_GLA
W_VB = N_HEADS_GLA * VAL_DIM_GLA
W_GB = N_HEADS_GLA * VAL_DIM_GLA
W_AB = GLA_RANK
W_GATES = 2 * D_MODEL
IN_WIDTHS = (W_QA, W_KA, W_VA, W_QB, W_KB, W_VB, W_GB, W_AB, W_GATES)
D_IN = W_QA + W_KA + W_VA + W_QB + W_KB + W_VB + W_GB + W_AB + W_GATES

kernel_name = 'hybrid_diffattn_gla_hmoe_deepnorm'


def _split_points():
    pts, acc = [], 0
    for w in IN_WIDTHS[:-1]:
        acc += w
        pts.append(acc)
    return pts


def _standardize_f32(x):
    xf = x.astype(jnp.float32)
    mu = jnp.mean(xf, axis=-1, keepdims=True)
    var = jnp.mean(jnp.square(xf - mu), axis=-1, keepdims=True)
    return (xf - mu) * lax.rsqrt(var + LN_EPS)


def _layer_norm(x, g, b):
    return (_standardize_f32(x) * g + b).astype(x.dtype)


def _rms_norm(x, g):
    xf = x.astype(jnp.float32)
    return xf * lax.rsqrt(jnp.mean(jnp.square(xf), axis=-1, keepdims=True) + LN_EPS) * g


def _modulate(x, shift, scale):
    return (_standardize_f32(x) * (1.0 + scale[:, None, :]) + shift[:, None, :]).astype(x.dtype)


def _alibi_slopes(n_heads):
    return jnp.asarray(2.0 ** (-8.0 * np.arange(1, n_heads + 1) / n_heads), dtype=jnp.float32)


def diff_attention(q, k, v, lam):
    slopes = _alibi_slopes(q.shape[1])
    scale = HEAD_DIM_DIFF ** -0.5
    n_blocks = q.shape[3] // Q_BLOCK
    outs = []
    for i in range(n_blocks):
        q0 = i * Q_BLOCK
        kv = q0 + Q_BLOCK
        s = jnp.einsum('bhcqd,bhckd->bhcqk', q[:, :, :, q0:kv], k[:, :, :, :kv]).astype(jnp.float32) * scale
        dist = (jnp.arange(q0, kv)[:, None] - jnp.arange(kv)[None, :]).astype(jnp.float32)
        bias = jnp.where(dist >= 0, -slopes[:, None, None] * dist, -jnp.inf)
        p = jax.nn.softmax(s + bias[None, :, None], axis=-1)
        a = (p[:, :, 0] - lam * p[:, :, 1]).astype(v.dtype)
        outs.append(jnp.einsum('bhqk,bhkd->bhqd', a, v[:, :, :kv]))
    return jnp.concatenate(outs, axis=2)


def gla_chunked(q, k, v, log_a):
    b_, h_, s_, dk = q.shape
    dv = v.shape[-1]
    n_chunks = s_ // GLA_CHUNK

    def to_chunks(t):
        return jnp.moveaxis(t.astype(jnp.float32).reshape(b_, h_, n_chunks, GLA_CHUNK, t.shape[-1]), 2, 0)

    causal = jnp.tril(jnp.ones((GLA_CHUNK, GLA_CHUNK), dtype=bool))

    def step(state, inp):
        qc, kc, vc, ac = inp
        cum = jnp.cumsum(ac, axis=2)
        o_inter = jnp.einsum('bhtk,bhkv->bhtv', qc * jnp.exp(cum), state)
        diff = cum[:, :, :, None, :] - cum[:, :, None, :, :]
        decay = jnp.exp(jnp.where(causal[:, :, None], diff, -jnp.inf))
        att = jnp.einsum('bhtk,bhsk,bhtsk->bhts', qc, kc, decay)
        o_intra = jnp.einsum('bhts,bhsv->bhtv', att, vc)
        last = cum[:, :, -1:, :]
        new_state = jnp.exp(last)[:, :, 0, :, None] * state + jnp.einsum('bhsk,bhsv->bhkv', kc * jnp.exp(last - cum), vc)
        return new_state, o_inter + o_intra

    state0 = jnp.zeros((b_, h_, dk, dv), jnp.float32)
    _, o = lax.scan(step, state0, (to_chunks(q), to_chunks(k), to_chunks(v), to_chunks(log_a)))
    return jnp.moveaxis(o, 0, 2).reshape(b_, h_, s_, dv)


def token_mixer(u, layer_idx, w_in, b_gates, w_alpha, b_alpha, lq1, lk1, lq2, lk2,
                diff_g, gla_g, w_ba, w_bb, w_out):
    b_, s_, _ = u.shape
    proj = u @ w_in
    qa, ka, va, qb, kb, vb, gb, ab, gates = jnp.split(proj, _split_points(), axis=-1)

    qa = qa.reshape(b_, s_, N_HEADS_DIFF, 2, HEAD_DIM_DIFF).transpose(0, 2, 3, 1, 4)
    ka = ka.reshape(b_, s_, N_HEADS_DIFF, 2, HEAD_DIM_DIFF).transpose(0, 2, 3, 1, 4)
    va = va.reshape(b_, s_, N_HEADS_DIFF, V_DIM_DIFF).transpose(0, 2, 1, 3)
    lam_init = 0.8 - 0.6 * math.exp(-0.3 * layer_idx)
    lam = (jnp.exp(jnp.sum(lq1.astype(jnp.float32) * lk1)) - jnp.exp(jnp.sum(lq2.astype(jnp.float32) * lk2)) + lam_init)
    oa = diff_attention(qa, ka, va, lam)
    oa = (_rms_norm(oa, diff_g) * (1.0 - lam_init)).astype(u.dtype)
    oa = oa.transpose(0, 2, 1, 3).reshape(b_, s_, W_VA)

    log_a = jax.nn.log_sigmoid((ab @ w_alpha + b_alpha).astype(jnp.float32)) / GLA_TAU
    heads_k = lambda t: t.reshape(b_, s_, N_HEADS_GLA, -1).transpose(0, 2, 1, 3)
    ob = gla_chunked(heads_k(qb * (KEY_DIM_GLA ** -0.5)), heads_k(kb), heads_k(vb), heads_k(log_a))
    ob = _rms_norm(ob, gla_g).transpose(0, 2, 1, 3).reshape(b_, s_, W_VB)
    ob = (ob * jax.nn.silu(gb.astype(jnp.float32))).astype(u.dtype)

    gate_a, gate_b = jnp.split(jax.nn.sigmoid(gates + b_gates), 2, axis=-1)
    mixed = gate_a * (oa @ w_ba) + gate_b * (ob @ w_bb)
    return mixed @ w_out


def hier_moe(u, w_rg, b_rg, w_re, b_re, wg, wu, wd):
    b_, s_, d_ = u.shape
    n_tok = b_ * s_
    uf = u.reshape(n_tok, d_)
    g_logits = (uf @ w_rg + b_rg).astype(jnp.float32)
    g_prob = jax.nn.softmax(g_logits, axis=-1)
    g_idx = jnp.argmax(g_logits, axis=-1).astype(jnp.int32)
    g_w = jnp.take_along_axis(g_prob, g_idx[:, None], axis=1)[:, 0]
    e_logits = (uf @ w_re + b_re).astype(jnp.float32).reshape(n_tok, N_GROUPS, EXPERTS_PER_GROUP)
    e_in = jnp.take_along_axis(e_logits, g_idx[:, None, None], axis=1)[:, 0]
    top_v, top_i = lax.top_k(e_in, TOP_K_IN_GROUP)
    weights = jax.nn.softmax(top_v, axis=-1) * g_w[:, None]

    expert_id = (g_idx[:, None] * EXPERTS_PER_GROUP + top_i).reshape(-1).astype(jnp.int32)
    token_id = jnp.repeat(jnp.arange(n_tok, dtype=jnp.int32), TOP_K_IN_GROUP)
    weight = weights.reshape(-1)
    n_assign = n_tok * TOP_K_IN_GROUP

    order = jnp.argsort(expert_id)
    se, st, sw = expert_id[order], token_id[order], weight[order]
    counts = jnp.bincount(expert_id, length=N_EXPERTS).astype(jnp.int32)
    starts = jnp.cumsum(counts) - counts
    padded = ((counts + MOE_BLOCK - 1) // MOE_BLOCK) * MOE_BLOCK
    pends = jnp.cumsum(padded)
    pstarts = pends - padded
    dest = pstarts[se] + (jnp.arange(n_assign, dtype=jnp.int32) - starts[se])
    n_rows = ((n_assign + MOE_BLOCK - 1) // MOE_BLOCK) * MOE_BLOCK + N_EXPERTS * MOE_BLOCK
    n_blocks = n_rows // MOE_BLOCK
    buf_tok = jnp.full((n_rows,), n_tok, jnp.int32).at[dest].set(st)
    buf_w = jnp.zeros((n_rows,), jnp.float32).at[dest].set(sw)
    blk_start = jnp.arange(n_blocks, dtype=jnp.int32) * MOE_BLOCK
    blk_e = jnp.minimum(jnp.searchsorted(pends, blk_start, side='right'), N_EXPERTS - 1).astype(jnp.int32)

    u_pad = jnp.concatenate([uf, jnp.zeros((1, d_), uf.dtype)], axis=0)
    xb = u_pad[buf_tok].reshape(n_blocks, MOE_BLOCK, d_)

    def expert_block(args):
        xblk, e = args
        h = jax.nn.silu(xblk @ wg[e]) * (xblk @ wu[e])
        return h @ wd[e]

    yb = lax.map(expert_block, (xb, blk_e)).reshape(n_rows, d_)
    y = jax.ops.segment_sum(yb.astype(jnp.float32) * buf_w[:, None], buf_tok, num_segments=n_tok + 1)[:n_tok]
    return y.astype(u.dtype).reshape(b_, s_, d_)


def setup_inputs(seed: int = 0) -> dict:
    key = jax.random.key(seed)
    ks = jax.random.split(key, 32)
    f32 = jnp.float32

    def nrm(k, shape, scale):
        return jax.random.normal(k, shape, f32) * scale

    d = D_MODEL
    col_scale = jnp.concatenate([
        jnp.ones((W_QA + W_KA,), f32), jnp.full((W_VA,), DEEPNORM_BETA, f32),
        jnp.ones((W_QB + W_KB,), f32), jnp.full((W_VB,), DEEPNORM_BETA, f32),
        jnp.ones((W_GB + W_AB + W_GATES,), f32)])
    return {
        'x': nrm(ks[0], (BATCH, SEQ, d), 1.0),
        'c': nrm(ks[1], (BATCH, d), 1.0),
        'w_ada': nrm(ks[2], (DEPTH, d, 6 * d), d ** -0.5),
        'b_ada': nrm(ks[3], (DEPTH, 6 * d), 0.02),
        'w_in': nrm(ks[4], (DEPTH, d, D_IN), d ** -0.5) * col_scale,
        'b_gates': nrm(ks[5], (DEPTH, W_GATES), 0.02),
        'w_alpha': nrm(ks[6], (DEPTH, GLA_RANK, W_QB), GLA_RANK ** -0.5),
        'b_alpha': nrm(ks[7], (DEPTH, W_QB), 0.02),
        'lambda_q1': nrm(ks[8], (DEPTH, HEAD_DIM_DIFF), 0.1),
        'lambda_k1': nrm(ks[9], (DEPTH, HEAD_DIM_DIFF), 0.1),
        'lambda_q2': nrm(ks[10], (DEPTH, HEAD_DIM_DIFF), 0.1),
        'lambda_k2': nrm(ks[11], (DEPTH, HEAD_DIM_DIFF), 0.1),
        'diff_norm_g': 1.0 + nrm(ks[12], (DEPTH, V_DIM_DIFF), 0.02),
        'gla_norm_g': 1.0 + nrm(ks[13], (DEPTH, VAL_DIM_GLA), 0.02),
        'w_branch_a': nrm(ks[14], (DEPTH, W_VA, d), W_VA ** -0.5),
        'w_branch_b': nrm(ks[15], (DEPTH, W_VB, d), W_VB ** -0.5),
        'w_out': nrm(ks[16], (DEPTH, d, d), d ** -0.5 * DEEPNORM_BETA),
        'ln1_g': 1.0 + nrm(ks[17], (DEPTH, d), 0.02),
        'ln1_b': nrm(ks[18], (DEPTH, d), 0.02),
        'w_router_g': nrm(ks[19], (DEPTH, d, N_GROUPS), d ** -0.5),
        'b_router_g': nrm(ks[20], (DEPTH, N_GROUPS), 0.01),
        'w_router_e': nrm(ks[21], (DEPTH, d, N_EXPERTS), d ** -0.5),
        'b_router_e': nrm(ks[22], (DEPTH, N_EXPERTS), 0.01),
        'w_gate_e': nrm(ks[23], (DEPTH, N_EXPERTS, d, D_EXPERT), d ** -0.5),
        'w_up_e': nrm(ks[24], (DEPTH, N_EXPERTS, d, D_EXPERT), d ** -0.5 * DEEPNORM_BETA),
        'w_down_e': nrm(ks[25], (DEPTH, N_EXPERTS, D_EXPERT, d), D_EXPERT ** -0.5 * DEEPNORM_BETA),
        'ln2_g': 1.0 + nrm(ks[26], (DEPTH, d), 0.02),
        'ln2_b': nrm(ks[27], (DEPTH, d), 0.02),
    }


def reference(x, c, w_ada, b_ada, w_in, b_gates, w_alpha, b_alpha, lambda_q1, lambda_k1,
              lambda_q2, lambda_k2, diff_norm_g, gla_norm_g, w_branch_a, w_branch_b, w_out,
              ln1_g, ln1_b, w_router_g, b_router_g, w_router_e, b_router_e, w_gate_e, w_up_e,
              w_down_e, ln2_g, ln2_b):
    cond = jax.nn.silu(c)
    for l in range(DEPTH):
        ada = cond @ w_ada[l] + b_ada[l]
        sh1, sc1, g1, sh2, sc2, g2 = jnp.split(ada, 6, axis=-1)
        u = _modulate(x, sh1, sc1)
        y = token_mixer(u, l, w_in[l], b_gates[l], w_alpha[l], b_alpha[l],
                        lambda_q1[l], lambda_k1[l], lambda_q2[l], lambda_k2[l],
                        diff_norm_g[l], gla_norm_g[l], w_branch_a[l], w_branch_b[l], w_out[l])
        x = _layer_norm(DEEPNORM_ALPHA * x + g1[:, None, :] * y, ln1_g[l], ln1_b[l])
        u = _modulate(x, sh2, sc2)
        y = hier_moe(u, w_router_g[l], b_router_g[l], w_router_e[l], b_router_e[l],
                     w_gate_e[l], w_up_e[l], w_down_e[l])
        x = _layer_norm(DEEPNORM_ALPHA * x + g2[:, None, :] * y, ln2_g[l], ln2_b[l])
    return x
```

```python
import functools
import math

import jax
import jax.numpy as jnp
import numpy as np
from jax import lax
from jax.experimental import pallas as pl
from jax.experimental.pallas import tpu as pltpu

F32 = jnp.float32
BF16 = jnp.bfloat16

N_HEADS_DIFF = 8
HEAD_DIM_DIFF = 64
N_HEADS_GLA = 4
KEY_DIM_GLA = 128
VAL_DIM_GLA = 256
GLA_RANK = 16
GLA_TAU = 16.0
GLA_CHUNK = 64
N_GROUPS = 4
EXPERTS_PER_GROUP = 8
N_EXPERTS = N_GROUPS * EXPERTS_PER_GROUP
LN_EPS = 1e-5
LANES = 128
NEG_BIG = -1e30

OFF_QA, OFF_KA, OFF_VA = 0, 1024, 2048
OFF_QB, OFF_KB, OFF_VB, OFF_GB, OFF_GATES = 3072, 3584, 4096, 5120, 6144
W_MAIN = 8192

VMEM_LIMIT = 56 * 1024 * 1024


def _cparams(sem, **kw):
    return pltpu.CompilerParams(dimension_semantics=sem, vmem_limit_bytes=VMEM_LIMIT, **kw)


def _standardize(x):
    mu = jnp.mean(x, axis=-1, keepdims=True)
    xc = x - mu
    var = jnp.mean(xc * xc, axis=-1, keepdims=True)
    return xc * lax.rsqrt(var + LN_EPS)


def _sigmoid(x):
    return 1.0 / (1.0 + jnp.exp(-x))


def _ada_kernel(c_ref, w_ref, b_ref, o_ref):
    c = c_ref[...]
    cond = c * _sigmoid(c)
    o_ref[0] = jnp.dot(cond, w_ref[0], preferred_element_type=F32,
                       precision=lax.Precision.HIGHEST) + b_ref[0]


def _ada(c, w_ada, b_ada):
    depth, d, d6 = w_ada.shape
    b = c.shape[0]
    return pl.pallas_call(
        _ada_kernel,
        out_shape=jax.ShapeDtypeStruct((depth, b, d6), F32),
        grid=(depth, d6 // d),
        in_specs=[pl.BlockSpec((b, d), lambda l, j: (0, 0)),
                  pl.BlockSpec((1, d, d), lambda l, j: (l, 0, j)),
                  pl.BlockSpec((1, 1, d), lambda l, j: (l, 0, j))],
        out_specs=pl.BlockSpec((1, b, d), lambda l, j: (l, 0, j)),
        compiler_params=_cparams(("arbitrary", "arbitrary")),
    )(c, w_ada, b_ada.reshape(depth, 1, d6))


def _inproj_kernel(x_ref, sh_ref, sc_ref, w_ref, wab_ref, wal_ref, bal_ref,
                   p_ref, la_ref, u_scr):
    @pl.when(pl.program_id(1) == 0)
    def _():
        u = _standardize(x_ref[...]) * (1.0 + sc_ref[0]) + sh_ref[0]
        ub = u.astype(BF16)
        u_scr[...] = ub
        ab = jnp.dot(ub, wab_ref[...], preferred_element_type=F32)
        pre = jnp.dot(ab.astype(BF16), wal_ref[...], preferred_element_type=F32) + bal_ref[...]
        la_ref[...] = (jnp.minimum(pre, 0.0) - jnp.log(1.0 + jnp.exp(-jnp.abs(pre)))) * (1.0 / GLA_TAU)

    p_ref[...] = jnp.dot(u_scr[...], w_ref[...], preferred_element_type=F32).astype(BF16)


def _inproj(x2, sh, sc, w_main, w_ab, w_alpha, b_alpha, seq, tm=1024, tn=1024):
    n, d = x2.shape
    tpb = seq // tm
    wq = w_alpha.shape[1]
    return pl.pallas_call(
        _inproj_kernel,
        out_shape=(jax.ShapeDtypeStruct((n, W_MAIN), BF16),
                   jax.ShapeDtypeStruct((n, wq), F32)),
        grid=(n // tm, W_MAIN // tn),
        in_specs=[pl.BlockSpec((tm, d), lambda i, j: (i, 0)),
                  pl.BlockSpec((1, 1, d), lambda i, j: (i // tpb, 0, 0)),
                  pl.BlockSpec((1, 1, d), lambda i, j: (i // tpb, 0, 0)),
                  pl.BlockSpec((d, tn), lambda i, j: (0, j)),
                  pl.BlockSpec((d, LANES), lambda i, j: (0, 0)),
                  pl.BlockSpec((LANES, wq), lambda i, j: (0, 0)),
                  pl.BlockSpec((1, wq), lambda i, j: (0, 0))],
        out_specs=(pl.BlockSpec((tm, tn), lambda i, j: (i, j)),
                   pl.BlockSpec((tm, wq), lambda i, j: (i, 0))),
        scratch_shapes=[pltpu.VMEM((tm, d), BF16)],
        compiler_params=_cparams(("arbitrary", "arbitrary")),
    )(x2, sh, sc, w_main, w_ab, w_alpha, b_alpha)


def _attn_kernel(slopes_ref, q_ref, k_ref, v_ref, lq1_ref, lk1_ref, lq2_ref, lk2_ref, g_ref,
                 o_ref, acc_scr, m_scr, l_scr, *, tq, lam_init):
    h = pl.program_id(1)
    qi = pl.program_id(2)
    slope = slopes_ref[h]
    dh = HEAD_DIM_DIFF

    q = q_ref[...] * jnp.asarray(dh ** -0.5, BF16)
    lane = lax.broadcasted_iota(jnp.int32, q.shape, 1)
    zero = jnp.zeros_like(q)
    qq = jnp.concatenate([jnp.where(lane < dh, q, zero), jnp.where(lane >= dh, q, zero)], axis=0)

    row = lax.broadcasted_iota(jnp.int32, (2 * tq, tq), 0)
    col = lax.broadcasted_iota(jnp.int32, (2 * tq, tq), 1)
    rel = jnp.where(row >= tq, row - tq, row) - col
    nb = (-slope) * rel.astype(F32)

    m_scr[...] = jnp.full(m_scr.shape, NEG_BIG, F32)
    l_scr[...] = jnp.zeros(l_scr.shape, F32)
    acc_scr[...] = jnp.zeros(acc_scr.shape, F32)

    def step(k, v, bias, cj):
        s = lax.dot_general(qq, k, (((1,), (1,)), ((), ())), preferred_element_type=F32) + bias
        m_prev = m_scr[...]
        m_new = jnp.maximum(m_prev, jnp.max(s, axis=-1, keepdims=True) + cj)
        p = jnp.exp(s - (m_new - cj))
        alpha = jnp.exp(m_prev - m_new)
        l_scr[...] = alpha * l_scr[...] + jnp.sum(p, axis=-1, keepdims=True)
        acc_scr[...] = alpha * acc_scr[...] + jnp.dot(p.astype(BF16), v, preferred_element_type=F32)
        m_scr[...] = m_new

    def body(j, carry):
        off = pl.multiple_of(j * tq, tq)
        cj = (-slope) * ((qi - j) * tq).astype(F32)
        step(k_ref[pl.ds(off, tq), :], v_ref[pl.ds(off, tq), :], nb, cj)
        return carry

    lax.fori_loop(0, qi, body, 0)
    off = pl.multiple_of(qi * tq, tq)
    step(k_ref[pl.ds(off, tq), :], v_ref[pl.ds(off, tq), :],
         jnp.where(rel >= 0, nb, NEG_BIG), jnp.float32(0.0))

    lam = (jnp.exp(jnp.sum(lq1_ref[...] * lk1_ref[...], axis=-1, keepdims=True))
           - jnp.exp(jnp.sum(lq2_ref[...] * lk2_ref[...], axis=-1, keepdims=True)) + lam_init)
    o = acc_scr[...] / l_scr[...]
    o = o[:tq] - lam * o[tq:]
    o = o * lax.rsqrt(jnp.mean(o * o, axis=-1, keepdims=True) + LN_EPS) * g_ref[...] * (1.0 - lam_init)
    o_ref[...] = o.astype(o_ref.dtype)


def _attn(p3, lq1, lk1, lq2, lk2, g, lam_init, tq=512):
    b, s, _ = p3.shape
    h = N_HEADS_DIFF
    slopes = jnp.asarray(2.0 ** (-8.0 * np.arange(1, h + 1) / h), dtype=F32)
    vec = pl.BlockSpec((1, HEAD_DIM_DIFF), lambda bi, hi, qi, sl: (0, 0))
    return pl.pallas_call(
        functools.partial(_attn_kernel, tq=tq, lam_init=lam_init),
        out_shape=jax.ShapeDtypeStruct((b, s, h * LANES), BF16),
        grid_spec=pltpu.PrefetchScalarGridSpec(
            num_scalar_prefetch=1,
            grid=(b, h, s // tq),
            in_specs=[pl.BlockSpec((None, tq, LANES), lambda bi, hi, qi, sl: (bi, qi, OFF_QA // LANES + hi)),
                      pl.BlockSpec((None, s, LANES), lambda bi, hi, qi, sl: (bi, 0, OFF_KA // LANES + hi)),
                      pl.BlockSpec((None, s, LANES), lambda bi, hi, qi, sl: (bi, 0, OFF_VA // LANES + hi)),
                      vec, vec, vec, vec,
                      pl.BlockSpec((1, LANES), lambda bi, hi, qi, sl: (0, 0))],
            out_specs=pl.BlockSpec((None, tq, LANES), lambda bi, hi, qi, sl: (bi, qi, hi)),
            scratch_shapes=[pltpu.VMEM((2 * tq, LANES), F32),
                            pltpu.VMEM((2 * tq, 1), F32),
                            pltpu.VMEM((2 * tq, 1), F32)]),
        compiler_params=_cparams(("arbitrary", "arbitrary", "arbitrary")),
    )(slopes, p3, p3, p3, lq1, lk1, lq2, lk2, g)


def _gla_kernel(q_ref, k_ref, v_ref, gb_ref, la_ref, g_ref, o_ref, state_scr, *, tt):
    c = GLA_CHUNK

    @pl.when(pl.program_id(2) == 0)
    def _():
        state_scr[...] = jnp.zeros(state_scr.shape, F32)

    la = la_ref[...]
    la_hi = la.astype(BF16)
    la_lo = (la - la_hi.astype(F32)).astype(BF16)
    r = lax.broadcasted_iota(jnp.int32, (tt, tt), 0)
    cc = lax.broadcasted_iota(jnp.int32, (tt, tt), 1)
    tri = jnp.where((r // c == cc // c) & (cc <= r), 1.0, 0.0).astype(BF16)
    cum = (jnp.dot(tri, la_hi, preferred_element_type=F32)
           + jnp.dot(tri, la_lo, preferred_element_type=F32))

    rr = lax.broadcasted_iota(jnp.int32, (c, c), 0)
    cr = lax.broadcasted_iota(jnp.int32, (c, c), 1)
    causal = cr <= rr
    qscale = KEY_DIM_GLA ** -0.5

    for ci in range(tt // c):
        sl = slice(ci * c, (ci + 1) * c)
        cum_c = cum[sl]
        last = cum_c[c - 1:c]
        mid = cum_c[c // 2:c // 2 + 1]
        q_c = q_ref[sl, :].astype(F32) * qscale
        k_c = k_ref[sl, :].astype(F32)
        v_c = v_ref[sl, :]
        st = state_scr[...]
        qe = (q_c * jnp.exp(cum_c)).astype(BF16)
        o_inter = lax.dot_general(qe, st.astype(BF16), (((1,), (1,)), ((), ())),
                                  preferred_element_type=F32)
        q2 = (q_c * jnp.exp(cum_c - mid)).astype(BF16)
        k2 = (k_c * jnp.exp(mid - cum_c)).astype(BF16)
        att = lax.dot_general(q2, k2, (((1,), (1,)), ((), ())), preferred_element_type=F32)
        att = jnp.where(causal, att, 0.0).astype(BF16)
        o = o_inter + jnp.dot(att, v_c, preferred_element_type=F32)
        kd = (k_c * jnp.exp(last - cum_c)).astype(BF16)
        v_t = v_c.astype(F32).T.astype(BF16)
        state_scr[...] = st * jnp.exp(last) + jnp.dot(v_t, kd, preferred_element_type=F32)

        o = o * lax.rsqrt(jnp.mean(o * o, axis=-1, keepdims=True) + LN_EPS) * g_ref[...]
        gate = gb_ref[sl, :].astype(F32)
        o_ref[sl, :] = (o * (gate * _sigmoid(gate))).astype(o_ref.dtype)


def _gla(p3, la3, g, tt=512):
    b, s, _ = p3.shape
    h = N_HEADS_GLA
    dk, dv = KEY_DIM_GLA, VAL_DIM_GLA
    return pl.pallas_call(
        functools.partial(_gla_kernel, tt=tt),
        out_shape=jax.ShapeDtypeStruct((b, s, h * dv), BF16),
        grid=(b, h, s // tt),
        in_specs=[pl.BlockSpec((None, tt, dk), lambda bi, hi, ti: (bi, ti, OFF_QB // dk + hi)),
                  pl.BlockSpec((None, tt, dk), lambda bi, hi, ti: (bi, ti, OFF_KB // dk + hi)),
                  pl.BlockSpec((None, tt, dv), lambda bi, hi, ti: (bi, ti, OFF_VB // dv + hi)),
                  pl.BlockSpec((None, tt, dv), lambda bi, hi, ti: (bi, ti, OFF_GB // dv + hi)),
                  pl.BlockSpec((None, tt, dk), lambda bi, hi, ti: (bi, ti, hi)),
                  pl.BlockSpec((1, dv), lambda bi, hi, ti: (0, 0))],
        out_specs=pl.BlockSpec((None, tt, dv), lambda bi, hi, ti: (bi, ti, hi)),
        scratch_shapes=[pltpu.VMEM((dv, dk), F32)],
        compiler_params=_cparams(("arbitrary", "arbitrary", "arbitrary")),
    )(p3, p3, p3, p3, la3, g)


def _mix_kernel(oa_ref, ob_ref, gt_ref, x_ref, bg_ref, g1_ref, sh2_ref, sc2_ref, lng_ref, lnb_ref,
                wba_ref, wbb_ref, wo_ref, wrh_ref, wrl_ref, br_ref,
                x1_ref, u2_ref, lg_ref, *, alpha):
    d = x_ref.shape[-1]
    a = jnp.dot(oa_ref[...], wba_ref[...], preferred_element_type=F32)
    bm = jnp.dot(ob_ref[...], wbb_ref[...], preferred_element_type=F32)
    gates = _sigmoid(gt_ref[...].astype(F32) + bg_ref[...])
    mixed = gates[:, :d] * a + gates[:, d:] * bm
    y = jnp.dot(mixed.astype(BF16), wo_ref[...], preferred_element_type=F32)
    x1 = _standardize(alpha * x_ref[...] + g1_ref[0] * y) * lng_ref[...] + lnb_ref[...]
    x1_ref[...] = x1
    u2 = _standardize(x1) * (1.0 + sc2_ref[0]) + sh2_ref[0]
    u2_ref[...] = u2
    uh = u2.astype(BF16)
    ul = (u2 - uh.astype(F32)).astype(BF16)
    lg_ref[...] = (jnp.dot(uh, wrh_ref[...], preferred_element_type=F32)
                   + jnp.dot(uh, wrl_ref[...], preferred_element_type=F32)
                   + jnp.dot(ul, wrh_ref[...], preferred_element_type=F32) + br_ref[...])


def _mix(oa, ob, p, x2, bg, g1, sh2, sc2, lng, lnb, wba, wbb, wo, wrh, wrl, br, seq, alpha, tm=512):
    n, d = x2.shape
    tpb = seq // tm
    row = lambda i: (i, 0)
    const = lambda i: (0, 0)
    per_b = pl.BlockSpec((1, 1, d), lambda i: (i // tpb, 0, 0))
    return pl.pallas_call(
        functools.partial(_mix_kernel, alpha=alpha),
        out_shape=(jax.ShapeDtypeStruct((n, d), F32),
                   jax.ShapeDtypeStruct((n, d), F32),
                   jax.ShapeDtypeStruct((n, LANES), F32)),
        grid=(n // tm,),
        in_specs=[pl.BlockSpec((tm, d), row), pl.BlockSpec((tm, d), row),
                  pl.BlockSpec((tm, 2 * d), lambda i: (i, OFF_GATES // (2 * d))),
                  pl.BlockSpec((tm, d), row),
                  pl.BlockSpec((1, 2 * d), const), per_b, per_b, per_b,
                  pl.BlockSpec((1, d), const), pl.BlockSpec((1, d), const),
                  pl.BlockSpec((d, d), const), pl.BlockSpec((d, d), const), pl.BlockSpec((d, d), const),
                  pl.BlockSpec((d, LANES), const), pl.BlockSpec((d, LANES), const),
                  pl.BlockSpec((1, LANES), const)],
        out_specs=(pl.BlockSpec((tm, d), row), pl.BlockSpec((tm, d), row),
                   pl.BlockSpec((tm, LANES), row)),
        compiler_params=_cparams(("arbitrary",)),
    )(oa, ob, p, x2, bg, g1, sh2, sc2, lng, lnb, wba, wbb, wo, wrh, wrl, br)


def _route_kernel(lg_ref, ri_ref, rw_ref, cnt_ref, *, tm):
    @pl.when(pl.program_id(0) == 0)
    def _():
        cnt_ref[...] = jnp.zeros(cnt_ref.shape, F32)

    lg = lg_ref[...]
    lane = lax.broadcasted_iota(jnp.int32, lg.shape, 1)
    lanef = lane.astype(F32)
    far = float(LANES)
    is_g = lane < N_GROUPS
    gl = jnp.where(is_g, lg, NEG_BIG)
    gmax = jnp.max(gl, axis=-1, keepdims=True)
    gidx = jnp.min(jnp.where(gl == gmax, lanef, far), axis=-1, keepdims=True)
    gw = 1.0 / jnp.sum(jnp.where(is_g, jnp.exp(gl - gmax), 0.0), axis=-1, keepdims=True)
    lo = N_GROUPS + gidx * EXPERTS_PER_GROUP
    in_g = (lanef >= lo) & (lanef < lo + EXPERTS_PER_GROUP)
    el = jnp.where(in_g, lg, NEG_BIG)
    v1 = jnp.max(el, axis=-1, keepdims=True)
    i1 = jnp.min(jnp.where(in_g & (el == v1), lanef, far), axis=-1, keepdims=True)
    in_g2 = in_g & (lanef != i1)
    el2 = jnp.where(in_g2, lg, NEG_BIG)
    v2 = jnp.max(el2, axis=-1, keepdims=True)
    i2 = jnp.min(jnp.where(in_g2 & (el2 == v2), lanef, far), axis=-1, keepdims=True)
    t = jnp.exp(v2 - v1)
    w1 = gw / (1.0 + t)
    w2 = gw * t / (1.0 + t)

    oh1 = lanef == i1
    oh2 = lanef == i2
    oh = jnp.where(oh1 | oh2, 1.0, 0.0)
    r = lax.broadcasted_iota(jnp.int32, (tm, tm), 0)
    c = lax.broadcasted_iota(jnp.int32, (tm, tm), 1)
    lower = jnp.where(c < r, 1.0, 0.0).astype(BF16)
    base = jnp.dot(lower, oh.astype(BF16), preferred_element_type=F32) + cnt_ref[0:1, :]
    r1 = jnp.sum(jnp.where(oh1, base, 0.0), axis=-1, keepdims=True)
    r2 = jnp.sum(jnp.where(oh2, base, 0.0), axis=-1, keepdims=True)
    cnt_ref[...] = cnt_ref[...] + jnp.sum(oh, axis=0, keepdims=True)

    e1 = i1 - float(N_GROUPS)
    e2 = i2 - float(N_GROUPS)
    ri = jnp.where(lane == 0, e1, jnp.where(lane == 1, e2, jnp.where(lane == 2, r1, jnp.where(lane == 3, r2, 0.0))))
    ri_ref[...] = ri.astype(jnp.int32)
    rw_ref[...] = jnp.where(lane == 0, w1, jnp.where(lane == 1, w2, 0.0))


def _route(lg, tm=512):
    n = lg.shape[0]
    row = lambda i: (i, 0)
    return pl.pallas_call(
        functools.partial(_route_kernel, tm=tm),
        out_shape=(jax.ShapeDtypeStruct((n, LANES), jnp.int32),
                   jax.ShapeDtypeStruct((n, LANES), F32),
                   jax.ShapeDtypeStruct((8, LANES), F32)),
        grid=(n // tm,),
        in_specs=[pl.BlockSpec((tm, LANES), row)],
        out_specs=(pl.BlockSpec((tm, LANES), row), pl.BlockSpec((tm, LANES), row),
                   pl.BlockSpec((8, LANES), lambda i: (0, 0))),
        compiler_params=_cparams(("arbitrary",)),
    )(lg)


def _scatter_kernel(pos_ref, u_ref, xs_in_ref, xs_ref, sem, *, tm):
    del xs_in_ref

    def row_copy(t, j):
        return pltpu.make_async_copy(u_ref.at[pl.ds(t, 1)], xs_ref.at[pl.ds(pos_ref[0, j, t], 1)], sem)

    def issue(t, carry):
        row_copy(t, 0).start()
        row_copy(t, 1).start()
        return carry

    def drain(t, carry):
        row_copy(t, 0).wait()
        row_copy(t, 1).wait()
        return carry

    lax.fori_loop(0, tm, issue, 0)
    lax.fori_loop(0, tm, drain, 0)


def _scatter(pos3, u2, n_rows, tm=256):
    n, d = u2.shape
    xs0 = jnp.zeros((n_rows, d), F32)
    return pl.pallas_call(
        functools.partial(_scatter_kernel, tm=tm),
        out_shape=jax.ShapeDtypeStruct((n_rows, d), F32),
        grid=(n // tm,),
        in_specs=[pl.BlockSpec((1, 2, tm), lambda i: (i, 0, 0), memory_space=pltpu.SMEM),
                  pl.BlockSpec((tm, d), lambda i: (i, 0)),
                  pl.BlockSpec(memory_space=pl.ANY)],
        out_specs=pl.BlockSpec(memory_space=pl.ANY),
        scratch_shapes=[pltpu.SemaphoreType.DMA(())],
        input_output_aliases={2: 0},
        compiler_params=_cparams(("arbitrary",), has_side_effects=True),
    )(pos3, u2, xs0)


def _expert_kernel(be_ref, nu_ref, x_ref, wg_ref, wu_ref, wd_ref, y_ref):
    del be_ref
    used = pl.program_id(0) < nu_ref[0]

    @pl.when(used)
    def _():
        x = x_ref[...].astype(BF16)
        g = jnp.dot(x, wg_ref[0], preferred_element_type=F32)
        u = jnp.dot(x, wu_ref[0], preferred_element_type=F32)
        hid = (g * _sigmoid(g) * u).astype(BF16)
        y_ref[...] = jnp.dot(hid, wd_ref[0], preferred_element_type=F32)

    @pl.when(jnp.logical_not(used))
    def _():
        y_ref[...] = jnp.zeros(y_ref.shape, F32)


def _experts(blk_e, n_used, xs, wg, wu, wd, blk):
    n_rows, d = xs.shape
    de = wg.shape[-1]
    return pl.pallas_call(
        _expert_kernel,
        out_shape=jax.ShapeDtypeStruct((n_rows, d), F32),
        grid_spec=pltpu.PrefetchScalarGridSpec(
            num_scalar_prefetch=2,
            grid=(n_rows // blk,),
            in_specs=[pl.BlockSpec((blk, d), lambda i, be, nu: (i, 0)),
                      pl.BlockSpec((1, d, de), lambda i, be, nu: (be[i], 0, 0)),
                      pl.BlockSpec((1, d, de), lambda i, be, nu: (be[i], 0, 0)),
                      pl.BlockSpec((1, de, d), lambda i, be, nu: (be[i], 0, 0))],
            out_specs=pl.BlockSpec((blk, d), lambda i, be, nu: (i, 0))),
        compiler_params=_cparams(("arbitrary",)),
    )(blk_e, n_used, xs, wg, wu, wd)


def _combine_kernel(pos_ref, x1_ref, rw_ref, g2_ref, lng_ref, lnb_ref, yb_ref, o_ref,
                    y1_scr, y2_scr, sem, *, tm, alpha):
    def row_copy(t, j, dst):
        return pltpu.make_async_copy(yb_ref.at[pl.ds(pos_ref[0, j, t], 1)], dst.at[pl.ds(t, 1)], sem)

    def issue(t, carry):
        row_copy(t, 0, y1_scr).start()
        row_copy(t, 1, y2_scr).start()
        return carry

    def drain(t, carry):
        row_copy(t, 0, y1_scr).wait()
        row_copy(t, 1, y2_scr).wait()
        return carry

    lax.fori_loop(0, tm, issue, 0)
    lax.fori_loop(0, tm, drain, 0)
    rw = rw_ref[...]
    y = rw[:, 0:1] * y1_scr[...] + rw[:, 1:2] * y2_scr[...]
    o_ref[...] = _standardize(alpha * x1_ref[...] + g2_ref[0] * y) * lng_ref[...] + lnb_ref[...]


def _combine(pos3, x1, rw, g2, lng, lnb, yb, seq, alpha, tm=256):
    n, d = x1.shape
    tpb = seq // tm
    const = lambda i: (0, 0)
    return pl.pallas_call(
        functools.partial(_combine_kernel, tm=tm, alpha=alpha),
        out_shape=jax.ShapeDtypeStruct((n, d), F32),
        grid=(n // tm,),
        in_specs=[pl.BlockSpec((1, 2, tm), lambda i: (i, 0, 0), memory_space=pltpu.SMEM),
                  pl.BlockSpec((tm, d), lambda i: (i, 0)),
                  pl.BlockSpec((tm, LANES), lambda i: (i, 0)),
                  pl.BlockSpec((1, 1, d), lambda i: (i // tpb, 0, 0)),
                  pl.BlockSpec((1, d), const), pl.BlockSpec((1, d), const),
                  pl.BlockSpec(memory_space=pl.ANY)],
        out_specs=pl.BlockSpec((tm, d), lambda i: (i, 0)),
        scratch_shapes=[pltpu.VMEM((tm, d), F32), pltpu.VMEM((tm, d), F32),
                        pltpu.SemaphoreType.DMA(())],
        compiler_params=_cparams(("arbitrary",)),
    )(pos3, x1, rw, g2, lng, lnb, yb)


MOE_ROWS = 256


def kernel(x, c, w_ada, b_ada, w_in, b_gates, w_alpha, b_alpha, lambda_q1, lambda_k1, lambda_q2, lambda_k2, diff_norm_g, gla_norm_g, w_branch_a, w_branch_b, w_out, ln1_g, ln1_b, w_router_g, b_router_g, w_router_e, b_router_e, w_gate_e, w_up_e, w_down_e, ln2_g, ln2_b):
    b, s, d = x.shape
    depth = w_ada.shape[0]
    n = b * s
    alpha = (2.0 * depth) ** 0.25
    blk = MOE_ROWS
    n_rows = 2 * n + N_EXPERTS * blk
    n_blocks = n_rows // blk
    tm_rows = 256

    ada = _ada(c, w_ada, b_ada)
    x2 = x.reshape(n, d)
    for l in range(depth):
        sh1, sc1, g1, sh2, sc2, g2 = [ada[l, :, i * d:(i + 1) * d].reshape(b, 1, d) for i in range(6)]
        wl = w_in[l]
        w_main = jnp.concatenate([wl[:, :6144], wl[:, 6144 + GLA_RANK:]], axis=1).astype(BF16)
        w_ab = jnp.pad(wl[:, 6144:6144 + GLA_RANK], ((0, 0), (0, LANES - GLA_RANK))).astype(BF16)
        wal = jnp.pad(w_alpha[l], ((0, LANES - GLA_RANK), (0, 0))).astype(BF16)
        p, la = _inproj(x2, sh1, sc1, w_main, w_ab, wal, b_alpha[l].reshape(1, -1), s)
        p3 = p.reshape(b, s, W_MAIN)

        lam_init = 0.8 - 0.6 * math.exp(-0.3 * l)
        oa = _attn(p3, lambda_q1[l].reshape(1, -1), lambda_k1[l].reshape(1, -1),
                   lambda_q2[l].reshape(1, -1), lambda_k2[l].reshape(1, -1),
                   diff_norm_g[l].reshape(1, -1), lam_init)
        ob = _gla(p3, la.reshape(b, s, -1), gla_norm_g[l].reshape(1, -1))

        wr = jnp.pad(jnp.concatenate([w_router_g[l], w_router_e[l]], axis=1),
                     ((0, 0), (0, LANES - N_GROUPS - N_EXPERTS)))
        wrh = wr.astype(BF16)
        wrl = (wr - wrh.astype(F32)).astype(BF16)
        br = jnp.pad(jnp.concatenate([b_router_g[l], b_router_e[l]]),
                     (0, LANES - N_GROUPS - N_EXPERTS)).reshape(1, LANES)
        x1, u2, lg = _mix(oa.reshape(n, d), ob.reshape(n, d), p, x2, b_gates[l].reshape(1, -1),
                          g1, sh2, sc2, ln1_g[l].reshape(1, d), ln1_b[l].reshape(1, d),
                          w_branch_a[l].astype(BF16), w_branch_b[l].astype(BF16), w_out[l].astype(BF16),
                          wrh, wrl, br, s, alpha)

        ri, rw, cnt = _route(lg)
        counts = cnt[0, N_GROUPS:N_GROUPS + N_EXPERTS].astype(jnp.int32)
        padded = ((counts + blk - 1) // blk) * blk
        pends = jnp.cumsum(padded)
        pstarts = pends - padded
        pos = jnp.take(pstarts, ri[:, 0:2], axis=0) + ri[:, 2:4]
        pos3 = pos.reshape(n // tm_rows, tm_rows, 2).transpose(0, 2, 1)
        blk_start = jnp.arange(n_blocks, dtype=jnp.int32) * blk
        blk_e = jnp.minimum(jnp.searchsorted(pends, blk_start, side='right'), N_EXPERTS - 1).astype(jnp.int32)
        n_used = (pends[-1:] // blk).astype(jnp.int32)

        xs = _scatter(pos3, u2, n_rows, tm=tm_rows)
        yb = _experts(blk_e, n_used, xs, w_gate_e[l].astype(BF16), w_up_e[l].astype(BF16),
                      w_down_e[l].astype(BF16), blk)
        x2 = _combine(pos3, x1, rw, g2, ln2_g[l].reshape(1, d), ln2_b[l].reshape(1, d), yb, s, alpha,
                      tm=tm_rows)
    return x2.reshape(b, s, d)
```

```python
import functools
import math

import jax
import jax.numpy as jnp
import numpy as np
from jax import lax
from jax.experimental import pallas as pl
from jax.experimental.pallas import tpu as pltpu

F32 = jnp.float32
BF16 = jnp.bfloat16

N_HEADS_DIFF = 8
HEAD_DIM_DIFF = 64
N_HEADS_GLA = 4
KEY_DIM_GLA = 128
VAL_DIM_GLA = 256
GLA_RANK = 16
GLA_TAU = 16.0
GLA_CHUNK = 64
N_GROUPS = 4
EXPERTS_PER_GROUP = 8
N_EXPERTS = N_GROUPS * EXPERTS_PER_GROUP
LN_EPS = 1e-5
LANES = 128
NEG_BIG = -1e30

OFF_QA, OFF_KA, OFF_VA = 0, 1024, 2048
OFF_QB, OFF_KB, OFF_VB, OFF_GB, OFF_GATES = 3072, 3584, 4096, 5120, 6144
W_MAIN = 8192

VMEM_LIMIT = 56 * 1024 * 1024


def _cparams(sem, **kw):
    return pltpu.CompilerParams(dimension_semantics=sem, vmem_limit_bytes=VMEM_LIMIT, **kw)


def _standardize(x):
    mu = jnp.mean(x, axis=-1, keepdims=True)
    xc = x - mu
    var = jnp.mean(xc * xc, axis=-1, keepdims=True)
    return xc * lax.rsqrt(var + LN_EPS)


def _sigmoid(x):
    return 1.0 / (1.0 + jnp.exp(-x))


def _ada_kernel(c_ref, w_ref, b_ref, o_ref):
    c = c_ref[...]
    cond = c * _sigmoid(c)
    o_ref[0] = jnp.dot(cond, w_ref[0], preferred_element_type=F32,
                       precision=lax.Precision.HIGHEST) + b_ref[0]


def _ada(c, w_ada, b_ada):
    depth, d, d6 = w_ada.shape
    b = c.shape[0]
    return pl.pallas_call(
        _ada_kernel,
        out_shape=jax.ShapeDtypeStruct((depth, b, d6), F32),
        grid=(depth, d6 // d),
        in_specs=[pl.BlockSpec((b, d), lambda l, j: (0, 0)),
                  pl.BlockSpec((1, d, d), lambda l, j: (l, 0, j)),
                  pl.BlockSpec((1, 1, d), lambda l, j: (l, 0, j))],
        out_specs=pl.BlockSpec((1, b, d), lambda l, j: (l, 0, j)),
        compiler_params=_cparams(("arbitrary", "arbitrary")),
    )(c, w_ada, b_ada.reshape(depth, 1, d6))


def _inproj_kernel(x_ref, sh_ref, sc_ref, w_ref, wab_ref, wal_ref, bal_ref,
                   p_ref, la_ref, u_scr):
    @pl.when(pl.program_id(1) == 0)
    def _():
        u = _standardize(x_ref[...]) * (1.0 + sc_ref[0]) + sh_ref[0]
        ub = u.astype(BF16)
        u_scr[...] = ub
        ab = jnp.dot(ub, wab_ref[...], preferred_element_type=F32)
        pre = jnp.dot(ab.astype(BF16), wal_ref[...], preferred_element_type=F32) + bal_ref[...]
        la_ref[...] = (jnp.minimum(pre, 0.0) - jnp.log(1.0 + jnp.exp(-jnp.abs(pre)))) * (1.0 / GLA_TAU)

    p_ref[...] = jnp.dot(u_scr[...], w_ref[...], preferred_element_type=F32).astype(BF16)


def _inproj(x2, sh, sc, w_main, w_ab, w_alpha, b_alpha, seq, tm=1024, tn=1024):
    n, d = x2.shape
    tpb = seq // tm
    wq = w_alpha.shape[1]
    return pl.pallas_call(
        _inproj_kernel,
        out_shape=(jax.ShapeDtypeStruct((n, W_MAIN), BF16),
                   jax.ShapeDtypeStruct((n, wq), F32)),
        grid=(n // tm, W_MAIN // tn),
        in_specs=[pl.BlockSpec((tm, d), lambda i, j: (i, 0)),
                  pl.BlockSpec((1, 1, d), lambda i, j: (i // tpb, 0, 0)),
                  pl.BlockSpec((1, 1, d), lambda i, j: (i // tpb, 0, 0)),
                  pl.BlockSpec((d, tn), lambda i, j: (0, j)),
                  pl.BlockSpec((d, LANES), lambda i, j: (0, 0)),
                  pl.BlockSpec((LANES, wq), lambda i, j: (0, 0)),
                  pl.BlockSpec((1, wq), lambda i, j: (0, 0))],
        out_specs=(pl.BlockSpec((tm, tn), lambda i, j: (i, j)),
                   pl.BlockSpec((tm, wq), lambda i, j: (i, 0))),
        scratch_shapes=[pltpu.VMEM((tm, d), BF16)],
        compiler_params=_cparams(("arbitrary", "arbitrary")),
    )(x2, sh, sc, w_main, w_ab, w_alpha, b_alpha)


ONES_ROWS = 16


def _attn_kernel(slopes_ref, q_ref, k_ref, v_ref, lq1_ref, lk1_ref, lq2_ref, lk2_ref, g_ref,
                 o_ref, vt_scr, acc_scr, m_scr, *, tq, lam_init):
    h = pl.program_id(1)
    qi = pl.program_id(2)
    slope = slopes_ref[h]
    dh = HEAD_DIM_DIFF
    dv = LANES
    n_chunks = vt_scr.shape[0]

    @pl.when(qi == 0)
    def _():
        for j in range(n_chunks):
            vt_scr[j, 0:dv, :] = v_ref[j * tq:(j + 1) * tq, :].astype(F32).T.astype(BF16)
            vt_scr[j, dv:dv + ONES_ROWS, :] = jnp.ones((ONES_ROWS, tq), BF16)

    q = q_ref[...] * jnp.asarray(dh ** -0.5, BF16)
    lane = lax.broadcasted_iota(jnp.int32, q.shape, 1)
    zero = jnp.zeros_like(q)
    qq = jnp.concatenate([jnp.where(lane < dh, q, zero), jnp.where(lane >= dh, q, zero)], axis=0)

    kpos = lax.broadcasted_iota(jnp.int32, (tq, 2 * tq), 0)
    qpos = lax.broadcasted_iota(jnp.int32, (tq, 2 * tq), 1)
    rel = jnp.where(qpos >= tq, qpos - tq, qpos) - kpos
    nb = (-slope) * rel.astype(F32)

    m_scr[...] = jnp.full(m_scr.shape, NEG_BIG, F32)
    acc_scr[...] = jnp.zeros(acc_scr.shape, F32)

    def step(k, vt, bias, cj):
        s = lax.dot_general(k, qq, (((1,), (1,)), ((), ())), preferred_element_type=F32) + bias
        m_prev = m_scr[...]
        m_new = jnp.maximum(m_prev, jnp.max(s, axis=0, keepdims=True) + cj)
        p = jnp.exp(s - (m_new - cj))
        alpha = jnp.exp(m_prev - m_new)
        acc_scr[...] = alpha * acc_scr[...] + jnp.dot(vt, p.astype(BF16), preferred_element_type=F32)
        m_scr[...] = m_new

    def body(j, carry):
        off = pl.multiple_of(j * tq, tq)
        cj = (-slope) * ((qi - j) * tq).astype(F32)
        step(k_ref[pl.ds(off, tq), :], vt_scr[j], nb, cj)
        return carry

    lax.fori_loop(0, qi, body, 0)
    off = pl.multiple_of(qi * tq, tq)
    step(k_ref[pl.ds(off, tq), :], vt_scr[qi], jnp.where(rel >= 0, nb, NEG_BIG), jnp.float32(0.0))

    lam = (jnp.exp(jnp.sum(lq1_ref[...] * lk1_ref[...], axis=-1, keepdims=True))
           - jnp.exp(jnp.sum(lq2_ref[...] * lk2_ref[...], axis=-1, keepdims=True)) + lam_init)
    acc = acc_scr[...]
    ot = acc[0:dv] / acc[dv:dv + 1]
    o = (ot[:, :tq] - lam * ot[:, tq:]).T
    o = o * lax.rsqrt(jnp.mean(o * o, axis=-1, keepdims=True) + LN_EPS) * g_ref[...] * (1.0 - lam_init)
    o_ref[...] = o.astype(o_ref.dtype)


def _attn(p3, lq1, lk1, lq2, lk2, g, lam_init, tq=512):
    b, s, _ = p3.shape
    h = N_HEADS_DIFF
    slopes = jnp.asarray(2.0 ** (-8.0 * np.arange(1, h + 1) / h), dtype=F32)
    vec = pl.BlockSpec((1, HEAD_DIM_DIFF), lambda bi, hi, qi, sl: (0, 0))
    return pl.pallas_call(
        functools.partial(_attn_kernel, tq=tq, lam_init=lam_init),
        out_shape=jax.ShapeDtypeStruct((b, s, h * LANES), BF16),
        grid_spec=pltpu.PrefetchScalarGridSpec(
            num_scalar_prefetch=1,
            grid=(b, h, s // tq),
            in_specs=[pl.BlockSpec((None, tq, LANES), lambda bi, hi, qi, sl: (bi, qi, OFF_QA // LANES + hi)),
                      pl.BlockSpec((None, s, LANES), lambda bi, hi, qi, sl: (bi, 0, OFF_KA // LANES + hi)),
                      pl.BlockSpec((None, s, LANES), lambda bi, hi, qi, sl: (bi, 0, OFF_VA // LANES + hi)),
                      vec, vec, vec, vec,
                      pl.BlockSpec((1, LANES), lambda bi, hi, qi, sl: (0, 0))],
            out_specs=pl.BlockSpec((None, tq, LANES), lambda bi, hi, qi, sl: (bi, qi, hi)),
            scratch_shapes=[pltpu.VMEM((s // tq, LANES + ONES_ROWS, tq), BF16),
                            pltpu.VMEM((LANES + ONES_ROWS, 2 * tq), F32),
                            pltpu.VMEM((1, 2 * tq), F32)]),
        compiler_params=_cparams(("arbitrary", "arbitrary", "arbitrary")),
    )(slopes, p3, p3, p3, lq1, lk1, lq2, lk2, g)


def _gla_kernel(q_ref, k_ref, v_ref, gb_ref, la_ref, g_ref, o_ref, state_scr, *, tt):
    c = GLA_CHUNK

    @pl.when(pl.program_id(2) == 0)
    def _():
        state_scr[...] = jnp.zeros(state_scr.shape, F32)

    rr = lax.broadcasted_iota(jnp.int32, (c, c), 0)
    cr = lax.broadcasted_iota(jnp.int32, (c, c), 1)
    causal = cr <= rr
    tri = jnp.where(causal, 1.0, 0.0).astype(BF16)
    qscale = KEY_DIM_GLA ** -0.5
    dk = la_ref.shape[-1]

    st = state_scr[...]
    for ci in range(tt // c):
        sl = slice(ci * c, (ci + 1) * c)
        la = la_ref[sl, :]
        la_hi = la.astype(BF16)
        la_lo = (la - la_hi.astype(F32)).astype(BF16)
        cum2 = jnp.dot(tri, jnp.concatenate([la_hi, la_lo], axis=1), preferred_element_type=F32)
        cum_c = cum2[:, :dk] + cum2[:, dk:]
        last = cum_c[c - 1:c]
        mid = cum_c[c // 2:c // 2 + 1]
        q_c = q_ref[sl, :].astype(F32) * qscale
        k_c = k_ref[sl, :].astype(F32)
        v_c = v_ref[sl, :]
        qe = (q_c * jnp.exp(cum_c)).astype(BF16)
        o_inter = lax.dot_general(qe, st.astype(BF16), (((1,), (1,)), ((), ())),
                                  preferred_element_type=F32)
        q2 = (q_c * jnp.exp(cum_c - mid)).astype(BF16)
        k2 = (k_c * jnp.exp(mid - cum_c)).astype(BF16)
        att = lax.dot_general(q2, k2, (((1,), (1,)), ((), ())), preferred_element_type=F32)
        att = jnp.where(causal, att, 0.0).astype(BF16)
        o = o_inter + jnp.dot(att, v_c, preferred_element_type=F32)
        kd = (k_c * jnp.exp(last - cum_c)).astype(BF16)
        v_t = v_c.astype(F32).T.astype(BF16)
        st = st * jnp.exp(last) + jnp.dot(v_t, kd, preferred_element_type=F32)

        o = o * lax.rsqrt(jnp.mean(o * o, axis=-1, keepdims=True) + LN_EPS) * g_ref[...]
        gate = gb_ref[sl, :].astype(F32)
        o_ref[sl, :] = (o * (gate * _sigmoid(gate))).astype(o_ref.dtype)
    state_scr[...] = st


def _gla(p3, la3, g, tt=512):
    b, s, _ = p3.shape
    h = N_HEADS_GLA
    dk, dv = KEY_DIM_GLA, VAL_DIM_GLA
    return pl.pallas_call(
        functools.partial(_gla_kernel, tt=tt),
        out_shape=jax.ShapeDtypeStruct((b, s, h * dv), BF16),
        grid=(b, h, s // tt),
        in_specs=[pl.BlockSpec((None, tt, dk), lambda bi, hi, ti: (bi, ti, OFF_QB // dk + hi)),
                  pl.BlockSpec((None, tt, dk), lambda bi, hi, ti: (bi, ti, OFF_KB // dk + hi)),
                  pl.BlockSpec((None, tt, dv), lambda bi, hi, ti: (bi, ti, OFF_VB // dv + hi)),
                  pl.BlockSpec((None, tt, dv), lambda bi, hi, ti: (bi, ti, OFF_GB // dv + hi)),
                  pl.BlockSpec((None, tt, dk), lambda bi, hi, ti: (bi, ti, hi)),
                  pl.BlockSpec((1, dv), lambda bi, hi, ti: (0, 0))],
        out_specs=pl.BlockSpec((None, tt, dv), lambda bi, hi, ti: (bi, ti, hi)),
        scratch_shapes=[pltpu.VMEM((dv, dk), F32)],
        compiler_params=_cparams(("arbitrary", "arbitrary", "arbitrary")),
    )(p3, p3, p3, p3, la3, g)


def _mix_kernel(oa_ref, ob_ref, gt_ref, x_ref, bg_ref, g1_ref, sh2_ref, sc2_ref, lng_ref, lnb_ref,
                wba_ref, wbb_ref, wo_ref, wrh_ref, wrl_ref, br_ref,
                x1_ref, u2_ref, lg_ref, *, alpha):
    d = x_ref.shape[-1]
    a = jnp.dot(oa_ref[...], wba_ref[...], preferred_element_type=F32)
    bm = jnp.dot(ob_ref[...], wbb_ref[...], preferred_element_type=F32)
    gates = _sigmoid(gt_ref[...].astype(F32) + bg_ref[...])
    mixed = gates[:, :d] * a + gates[:, d:] * bm
    y = jnp.dot(mixed.astype(BF16), wo_ref[...], preferred_element_type=F32)
    x1 = _standardize(alpha * x_ref[...] + g1_ref[0] * y) * lng_ref[...] + lnb_ref[...]
    x1_ref[...] = x1
    u2 = _standardize(x1) * (1.0 + sc2_ref[0]) + sh2_ref[0]
    u2_ref[...] = u2
    uh = u2.astype(BF16)
    ul = (u2 - uh.astype(F32)).astype(BF16)
    lg_ref[...] = (jnp.dot(uh, wrh_ref[...], preferred_element_type=F32)
                   + jnp.dot(uh, wrl_ref[...], preferred_element_type=F32)
                   + jnp.dot(ul, wrh_ref[...], preferred_element_type=F32) + br_ref[...])


def _mix(oa, ob, p, x2, bg, g1, sh2, sc2, lng, lnb, wba, wbb, wo, wrh, wrl, br, seq, alpha, tm=512):
    n, d = x2.shape
    tpb = seq // tm
    row = lambda i: (i, 0)
    const = lambda i: (0, 0)
    per_b = pl.BlockSpec((1, 1, d), lambda i: (i // tpb, 0, 0))
    return pl.pallas_call(
        functools.partial(_mix_kernel, alpha=alpha),
        out_shape=(jax.ShapeDtypeStruct((n, d), F32),
                   jax.ShapeDtypeStruct((n, d), F32),
                   jax.ShapeDtypeStruct((n, LANES), F32)),
        grid=(n // tm,),
        in_specs=[pl.BlockSpec((tm, d), row), pl.BlockSpec((tm, d), row),
                  pl.BlockSpec((tm, 2 * d), lambda i: (i, OFF_GATES // (2 * d))),
                  pl.BlockSpec((tm, d), row),
                  pl.BlockSpec((1, 2 * d), const), per_b, per_b, per_b,
                  pl.BlockSpec((1, d), const), pl.BlockSpec((1, d), const),
                  pl.BlockSpec((d, d), const), pl.BlockSpec((d, d), const), pl.BlockSpec((d, d), const),
                  pl.BlockSpec((d, LANES), const), pl.BlockSpec((d, LANES), const),
                  pl.BlockSpec((1, LANES), const)],
        out_specs=(pl.BlockSpec((tm, d), row), pl.BlockSpec((tm, d), row),
                   pl.BlockSpec((tm, LANES), row)),
        compiler_params=_cparams(("arbitrary",)),
    )(oa, ob, p, x2, bg, g1, sh2, sc2, lng, lnb, wba, wbb, wo, wrh, wrl, br)


def _route_kernel(lg_ref, ri_ref, rw_ref, cnt_ref, *, tm):
    @pl.when(pl.program_id(0) == 0)
    def _():
        cnt_ref[...] = jnp.zeros(cnt_ref.shape, F32)

    lg = lg_ref[...]
    lane = lax.broadcasted_iota(jnp.int32, lg.shape, 1)
    lanef = lane.astype(F32)
    far = float(LANES)
    is_g = lane < N_GROUPS
    gl = jnp.where(is_g, lg, NEG_BIG)
    gmax = jnp.max(gl, axis=-1, keepdims=True)
    gidx = jnp.min(jnp.where(gl == gmax, lanef, far), axis=-1, keepdims=True)
    gw = 1.0 / jnp.sum(jnp.where(is_g, jnp.exp(gl - gmax), 0.0), axis=-1, keepdims=True)
    lo = N_GROUPS + gidx * EXPERTS_PER_GROUP
    in_g = (lanef >= lo) & (lanef < lo + EXPERTS_PER_GROUP)
    el = jnp.where(in_g, lg, NEG_BIG)
    v1 = jnp.max(el, axis=-1, keepdims=True)
    i1 = jnp.min(jnp.where(in_g & (el == v1), lanef, far), axis=-1, keepdims=True)
    in_g2 = in_g & (lanef != i1)
    el2 = jnp.where(in_g2, lg, NEG_BIG)
    v2 = jnp.max(el2, axis=-1, keepdims=True)
    i2 = jnp.min(jnp.where(in_g2 & (el2 == v2), lanef, far), axis=-1, keepdims=True)
    t = jnp.exp(v2 - v1)
    w1 = gw / (1.0 + t)
    w2 = gw * t / (1.0 + t)

    oh1 = lanef == i1
    oh2 = lanef == i2
    oh = jnp.where(oh1 | oh2, 1.0, 0.0)
    r = lax.broadcasted_iota(jnp.int32, (tm, tm), 0)
    c = lax.broadcasted_iota(jnp.int32, (tm, tm), 1)
    lower = jnp.where(c < r, 1.0, 0.0).astype(BF16)
    base = jnp.dot(lower, oh.astype(BF16), preferred_element_type=F32) + cnt_ref[0:1, :]
    r1 = jnp.sum(jnp.where(oh1, base, 0.0), axis=-1, keepdims=True)
    r2 = jnp.sum(jnp.where(oh2, base, 0.0), axis=-1, keepdims=True)
    cnt_ref[...] = cnt_ref[...] + jnp.sum(oh, axis=0, keepdims=True)

    e1 = i1 - float(N_GROUPS)
    e2 = i2 - float(N_GROUPS)
    ri = jnp.where(lane == 0, e1, jnp.where(lane == 1, e2, jnp.where(lane == 2, r1, jnp.where(lane == 3, r2, 0.0))))
    ri_ref[...] = ri.astype(jnp.int32)
    rw_ref[...] = jnp.where(lane == 0, w1, jnp.where(lane == 1, w2, 0.0))


def _route(lg, tm=512):
    n = lg.shape[0]
    row = lambda i: (i, 0)
    return pl.pallas_call(
        functools.partial(_route_kernel, tm=tm),
        out_shape=(jax.ShapeDtypeStruct((n, LANES), jnp.int32),
                   jax.ShapeDtypeStruct((n, LANES), F32),
                   jax.ShapeDtypeStruct((8, LANES), F32)),
        grid=(n // tm,),
        in_specs=[pl.BlockSpec((tm, LANES), row)],
        out_specs=(pl.BlockSpec((tm, LANES), row), pl.BlockSpec((tm, LANES), row),
                   pl.BlockSpec((8, LANES), lambda i: (0, 0))),
        compiler_params=_cparams(("arbitrary",)),
    )(lg)


def _scatter_kernel(pos_ref, u_ref, xs_in_ref, xs_ref, sem, *, tm):
    del xs_in_ref

    def row_copy(t, j):
        return pltpu.make_async_copy(u_ref.at[pl.ds(t, 1)], xs_ref.at[pl.ds(pos_ref[0, j, t], 1)], sem)

    def issue(t, carry):
        row_copy(t, 0).start()
        row_copy(t, 1).start()
        return carry

    lax.fori_loop(0, tm, issue, 0)
    for _ in range(2):
        pltpu.make_async_copy(u_ref, xs_ref.at[pl.ds(0, tm)], sem).wait()


def _scatter(pos3, u2, n_rows, tm=256):
    n, d = u2.shape
    xs0 = jnp.zeros((n_rows, d), F32)
    return pl.pallas_call(
        functools.partial(_scatter_kernel, tm=tm),
        out_shape=jax.ShapeDtypeStruct((n_rows, d), F32),
        grid=(n // tm,),
        in_specs=[pl.BlockSpec((1, 2, tm), lambda i: (i, 0, 0), memory_space=pltpu.SMEM),
                  pl.BlockSpec((tm, d), lambda i: (i, 0)),
                  pl.BlockSpec(memory_space=pl.ANY)],
        out_specs=pl.BlockSpec(memory_space=pl.ANY),
        scratch_shapes=[pltpu.SemaphoreType.DMA(())],
        input_output_aliases={2: 0},
        compiler_params=_cparams(("arbitrary",), has_side_effects=True),
    )(pos3, u2, xs0)


def _expert_kernel(be_ref, nu_ref, x_ref, wg_ref, wu_ref, wd_ref, y_ref, wg_scr, wu_scr, wd_scr):
    i = pl.program_id(0)
    used = i < nu_ref[0]

    @pl.when(used & ((i == 0) | (be_ref[i] != be_ref[jnp.maximum(i - 1, 0)])))
    def _():
        wg_scr[...] = wg_ref[0].astype(BF16)
        wu_scr[...] = wu_ref[0].astype(BF16)
        wd_scr[...] = wd_ref[0].astype(BF16)

    @pl.when(used)
    def _():
        x = x_ref[...].astype(BF16)
        g = jnp.dot(x, wg_scr[...], preferred_element_type=F32)
        u = jnp.dot(x, wu_scr[...], preferred_element_type=F32)
        hid = (g * _sigmoid(g) * u).astype(BF16)
        y_ref[...] = jnp.dot(hid, wd_scr[...], preferred_element_type=F32)

    @pl.when(jnp.logical_not(used))
    def _():
        y_ref[...] = jnp.zeros(y_ref.shape, F32)


def _experts(blk_e, n_used, xs, wg, wu, wd, layer, blk):
    n_rows, d = xs.shape
    de = wg.shape[-1]
    wspec = lambda r, c: pl.BlockSpec((None, 1, r, c), lambda i, be, nu: (layer, be[i], 0, 0))
    return pl.pallas_call(
        _expert_kernel,
        out_shape=jax.ShapeDtypeStruct((n_rows, d), F32),
        grid_spec=pltpu.PrefetchScalarGridSpec(
            num_scalar_prefetch=2,
            grid=(n_rows // blk,),
            in_specs=[pl.BlockSpec((blk, d), lambda i, be, nu: (i, 0)),
                      wspec(d, de), wspec(d, de), wspec(de, d)],
            out_specs=pl.BlockSpec((blk, d), lambda i, be, nu: (i, 0)),
            scratch_shapes=[pltpu.VMEM((d, de), BF16), pltpu.VMEM((d, de), BF16),
                            pltpu.VMEM((de, d), BF16)]),
        compiler_params=_cparams(("arbitrary",)),
    )(blk_e, n_used, xs, wg, wu, wd)


def _combine_kernel(pos_ref, x1_ref, rw_ref, g2_ref, lng_ref, lnb_ref, yb_ref, o_ref,
                    y1_scr, y2_scr, sem, *, tm, alpha):
    def row_copy(t, j, dst):
        return pltpu.make_async_copy(yb_ref.at[pl.ds(pos_ref[0, j, t], 1)], dst.at[pl.ds(t, 1)], sem)

    def issue(t, carry):
        row_copy(t, 0, y1_scr).start()
        row_copy(t, 1, y2_scr).start()
        return carry

    lax.fori_loop(0, tm, issue, 0)
    for dst in (y1_scr, y2_scr):
        pltpu.make_async_copy(yb_ref.at[pl.ds(0, tm)], dst, sem).wait()
    rw = rw_ref[...]
    y = rw[:, 0:1] * y1_scr[...] + rw[:, 1:2] * y2_scr[...]
    o_ref[...] = _standardize(alpha * x1_ref[...] + g2_ref[0] * y) * lng_ref[...] + lnb_ref[...]


def _combine(pos3, x1, rw, g2, lng, lnb, yb, seq, alpha, tm=256):
    n, d = x1.shape
    tpb = seq // tm
    const = lambda i: (0, 0)
    return pl.pallas_call(
        functools.partial(_combine_kernel, tm=tm, alpha=alpha),
        out_shape=jax.ShapeDtypeStruct((n, d), F32),
        grid=(n // tm,),
        in_specs=[pl.BlockSpec((1, 2, tm), lambda i: (i, 0, 0), memory_space=pltpu.SMEM),
                  pl.BlockSpec((tm, d), lambda i: (i, 0)),
                  pl.BlockSpec((tm, LANES), lambda i: (i, 0)),
                  pl.BlockSpec((1, 1, d), lambda i: (i // tpb, 0, 0)),
                  pl.BlockSpec((1, d), const), pl.BlockSpec((1, d), const),
                  pl.BlockSpec(memory_space=pl.ANY)],
        out_specs=pl.BlockSpec((tm, d), lambda i: (i, 0)),
        scratch_shapes=[pltpu.VMEM((tm, d), F32), pltpu.VMEM((tm, d), F32),
                        pltpu.SemaphoreType.DMA(())],
        compiler_params=_cparams(("arbitrary",)),
    )(pos3, x1, rw, g2, lng, lnb, yb)


MOE_ROWS = 256


def kernel(x, c, w_ada, b_ada, w_in, b_gates, w_alpha, b_alpha, lambda_q1, lambda_k1, lambda_q2, lambda_k2, diff_norm_g, gla_norm_g, w_branch_a, w_branch_b, w_out, ln1_g, ln1_b, w_router_g, b_router_g, w_router_e, b_router_e, w_gate_e, w_up_e, w_down_e, ln2_g, ln2_b):
    b, s, d = x.shape
    depth = w_ada.shape[0]
    n = b * s
    alpha = (2.0 * depth) ** 0.25
    blk = MOE_ROWS
    n_rows = 2 * n + N_EXPERTS * blk
    n_blocks = n_rows // blk
    tm_rows = 256

    ada = _ada(c, w_ada, b_ada)
    x2 = x.reshape(n, d)
    for l in range(depth):
        sh1, sc1, g1, sh2, sc2, g2 = [ada[l, :, i * d:(i + 1) * d].reshape(b, 1, d) for i in range(6)]
        wl = w_in[l]
        w_main = jnp.concatenate([wl[:, :6144], wl[:, 6144 + GLA_RANK:]], axis=1).astype(BF16)
        w_ab = jnp.pad(wl[:, 6144:6144 + GLA_RANK], ((0, 0), (0, LANES - GLA_RANK))).astype(BF16)
        wal = jnp.pad(w_alpha[l], ((0, LANES - GLA_RANK), (0, 0))).astype(BF16)
        p, la = _inproj(x2, sh1, sc1, w_main, w_ab, wal, b_alpha[l].reshape(1, -1), s)
        p3 = p.reshape(b, s, W_MAIN)

        lam_init = 0.8 - 0.6 * math.exp(-0.3 * l)
        oa = _attn(p3, lambda_q1[l].reshape(1, -1), lambda_k1[l].reshape(1, -1),
                   lambda_q2[l].reshape(1, -1), lambda_k2[l].reshape(1, -1),
                   diff_norm_g[l].reshape(1, -1), lam_init)
        ob = _gla(p3, la.reshape(b, s, -1), gla_norm_g[l].reshape(1, -1))

        wr = jnp.pad(jnp.concatenate([w_router_g[l], w_router_e[l]], axis=1),
                     ((0, 0), (0, LANES - N_GROUPS - N_EXPERTS)))
        wrh = wr.astype(BF16)
        wrl = (wr - wrh.astype(F32)).astype(BF16)
        br = jnp.pad(jnp.concatenate([b_router_g[l], b_router_e[l]]),
                     (0, LANES - N_GROUPS - N_EXPERTS)).reshape(1, LANES)
        x1, u2, lg = _mix(oa.reshape(n, d), ob.reshape(n, d), p, x2, b_gates[l].reshape(1, -1),
                          g1, sh2, sc2, ln1_g[l].reshape(1, d), ln1_b[l].reshape(1, d),
                          w_branch_a[l].astype(BF16), w_branch_b[l].astype(BF16), w_out[l].astype(BF16),
                          wrh, wrl, br, s, alpha)

        ri, rw, cnt = _route(lg)
        counts = cnt[0, N_GROUPS:N_GROUPS + N_EXPERTS].astype(jnp.int32)
        padded = ((counts + blk - 1) // blk) * blk
        pends = jnp.cumsum(padded)
        pstarts = pends - padded
        pos = jnp.take(pstarts, ri[:, 0:2], axis=0) + ri[:, 2:4]
        pos3 = pos.reshape(n // tm_rows, tm_rows, 2).transpose(0, 2, 1)
        blk_start = jnp.arange(n_blocks, dtype=jnp.int32) * blk
        blk_e = jnp.minimum(jnp.sum(blk_start[:, None] >= pends[None, :], axis=1), N_EXPERTS - 1).astype(jnp.int32)
        n_used = (pends[-1:] // blk).astype(jnp.int32)

        xs = _scatter(pos3, u2, n_rows, tm=tm_rows)
        yb = _experts(blk_e, n_used, xs, w_gate_e, w_up_e, w_down_e, l, blk)
        x2 = _combine(pos3, x1, rw, g2, ln2_g[l].reshape(1, d), ln2_b[l].reshape(1, d), yb, s, alpha,
                      tm=tm_rows)
    return x2.reshape(b, s, d)
```

```python
import functools
import math

import jax
import jax.numpy as jnp
import numpy as np
from jax import lax
from jax.experimental import pallas as pl
from jax.experimental.pallas import tpu as pltpu

F32 = jnp.float32
BF16 = jnp.bfloat16

N_HEADS_DIFF = 8
HEAD_DIM_DIFF = 64
N_HEADS_GLA = 4
KEY_DIM_GLA = 128
VAL_DIM_GLA = 256
GLA_RANK = 16
GLA_TAU = 16.0
GLA_CHUNK = 64
N_GROUPS = 4
EXPERTS_PER_GROUP = 8
N_EXPERTS = N_GROUPS * EXPERTS_PER_GROUP
LN_EPS = 1e-5
LANES = 128
NEG_BIG = -1e30

OFF_QA, OFF_KA, OFF_VA = 0, 1024, 2048
OFF_QB, OFF_KB, OFF_VB, OFF_GB, OFF_GATES = 3072, 3584, 4096, 5120, 6144
W_MAIN = 8192

VMEM_LIMIT = 56 * 1024 * 1024


def _cparams(sem, **kw):
    return pltpu.CompilerParams(dimension_semantics=sem, vmem_limit_bytes=VMEM_LIMIT, **kw)


def _standardize(x):
    mu = jnp.mean(x, axis=-1, keepdims=True)
    xc = x - mu
    var = jnp.mean(xc * xc, axis=-1, keepdims=True)
    return xc * lax.rsqrt(var + LN_EPS)


def _sigmoid(x):
    return 1.0 / (1.0 + jnp.exp(-x))


def _ada_kernel(c_ref, w_ref, b_ref, o_ref):
    c = c_ref[...]
    cond = c * _sigmoid(c)
    o_ref[0] = jnp.dot(cond, w_ref[0], preferred_element_type=F32,
                       precision=lax.Precision.HIGHEST) + b_ref[0]


def _ada(c, w_ada, b_ada):
    depth, d, d6 = w_ada.shape
    b = c.shape[0]
    return pl.pallas_call(
        _ada_kernel,
        out_shape=jax.ShapeDtypeStruct((depth, b, d6), F32),
        grid=(depth, d6 // d),
        in_specs=[pl.BlockSpec((b, d), lambda l, j: (0, 0)),
                  pl.BlockSpec((1, d, d), lambda l, j: (l, 0, j)),
                  pl.BlockSpec((1, 1, d), lambda l, j: (l, 0, j))],
        out_specs=pl.BlockSpec((1, b, d), lambda l, j: (l, 0, j)),
        compiler_params=_cparams(("arbitrary", "arbitrary")),
    )(c, w_ada, b_ada.reshape(depth, 1, d6))


def _inproj_kernel(x_ref, sh_ref, sc_ref, w_ref, wab_ref, wal_ref, bal_ref,
                   p_ref, la_ref, u_scr):
    @pl.when(pl.program_id(1) == 0)
    def _():
        u = _standardize(x_ref[...]) * (1.0 + sc_ref[0]) + sh_ref[0]
        ub = u.astype(BF16)
        u_scr[...] = ub
        ab = jnp.dot(ub, wab_ref[...], preferred_element_type=F32)
        pre = jnp.dot(ab.astype(BF16), wal_ref[...], preferred_element_type=F32) + bal_ref[...]
        la_ref[...] = (jnp.minimum(pre, 0.0) - jnp.log(1.0 + jnp.exp(-jnp.abs(pre)))) * (1.0 / GLA_TAU)

    p_ref[...] = jnp.dot(u_scr[...], w_ref[...], preferred_element_type=F32).astype(BF16)


def _inproj(x2, sh, sc, w_main, w_ab, w_alpha, b_alpha, seq, tm=1024, tn=1024):
    n, d = x2.shape
    tpb = seq // tm
    wq = w_alpha.shape[1]
    return pl.pallas_call(
        _inproj_kernel,
        out_shape=(jax.ShapeDtypeStruct((n, W_MAIN), BF16),
                   jax.ShapeDtypeStruct((n, wq), F32)),
        grid=(n // tm, W_MAIN // tn),
        in_specs=[pl.BlockSpec((tm, d), lambda i, j: (i, 0)),
                  pl.BlockSpec((1, 1, d), lambda i, j: (i // tpb, 0, 0)),
                  pl.BlockSpec((1, 1, d), lambda i, j: (i // tpb, 0, 0)),
                  pl.BlockSpec((d, tn), lambda i, j: (0, j)),
                  pl.BlockSpec((d, LANES), lambda i, j: (0, 0)),
                  pl.BlockSpec((LANES, wq), lambda i, j: (0, 0)),
                  pl.BlockSpec((1, wq), lambda i, j: (0, 0))],
        out_specs=(pl.BlockSpec((tm, tn), lambda i, j: (i, j)),
                   pl.BlockSpec((tm, wq), lambda i, j: (i, 0))),
        scratch_shapes=[pltpu.VMEM((tm, d), BF16)],
        compiler_params=_cparams(("arbitrary", "arbitrary")),
    )(x2, sh, sc, w_main, w_ab, w_alpha, b_alpha)


ONES_ROWS = 16


QUERY_GROUP = 256
POS_SPLIT = 32


def _attn_kernel(slopes_ref, q_ref, k_ref, v_ref, lq1_ref, lk1_ref, lq2_ref, lk2_ref, g_ref,
                 o_ref, kaug_scr, vt_scr, dmask_scr, sa_scr, sb_scr, acc_scr, m_scr, *, tq, lam_init):
    h = pl.program_id(1)
    qi = pl.program_id(2)
    slope = slopes_ref[h]
    dh = HEAD_DIM_DIFF
    dv = LANES
    n_chunks = vt_scr.shape[0]

    @pl.when(qi == 0)
    def _():
        for j in range(n_chunks):
            vt_scr[j, 0:dv, :] = v_ref[j * tq:(j + 1) * tq, :].astype(F32).T.astype(BF16)
            vt_scr[j, dv:dv + ONES_ROWS, :] = jnp.ones((ONES_ROWS, tq), BF16)
        kaug_scr[:, 0:LANES] = k_ref[...]
        koff = lax.broadcasted_iota(jnp.int32, (tq, LANES), 0)
        flane = lax.broadcasted_iota(jnp.int32, (tq, LANES), 1)
        feat = jnp.where(flane == 0, koff // POS_SPLIT, jnp.where(flane == 1, koff % POS_SPLIT, 0))
        feat = feat.astype(F32).astype(BF16)
        for j in range(n_chunks):
            kaug_scr[j * tq:(j + 1) * tq, LANES:2 * LANES] = feat
        kpos = lax.broadcasted_iota(jnp.int32, (tq, 2 * tq), 0)
        qpos = lax.broadcasted_iota(jnp.int32, (tq, 2 * tq), 1)
        qoff = jnp.where(qpos >= tq, qpos - tq, qpos)
        dmask_scr[...] = jnp.where(qoff >= kpos, 0.0, NEG_BIG)

    q = q_ref[...] * jnp.asarray(dh ** -0.5, BF16)
    lane = lax.broadcasted_iota(jnp.int32, q.shape, 1)
    zero = jnp.zeros_like(q)
    qf = jnp.where(lane == 0, slope * POS_SPLIT, jnp.where(lane == 1, slope, 0.0)).astype(BF16)
    qq = jnp.concatenate([jnp.concatenate([jnp.where(lane < dh, q, zero), qf], axis=1),
                          jnp.concatenate([jnp.where(lane >= dh, q, zero), qf], axis=1)], axis=0)

    m_scr[...] = jnp.full(m_scr.shape, NEG_BIG, F32)
    acc_scr[...] = jnp.zeros(acc_scr.shape, F32)

    lanes = [slice(g * QUERY_GROUP, (g + 1) * QUERY_GROUP) for g in range(2 * tq // QUERY_GROUP)]

    def step(j_cur, src, j_next, dst, masked):
        cj = (-slope) * ((qi - j_cur) * tq).astype(F32)
        if dst is not None:
            kaug = kaug_scr[pl.ds(pl.multiple_of(j_next * tq, tq), tq), :]
        vt = vt_scr[j_cur]
        m_all = m_scr[...]
        acc_all = acc_scr[...]
        m_out, acc_out = [], []
        for ls in lanes:
            if dst is not None:
                dst[:, ls] = lax.dot_general(kaug, qq[ls], (((1,), (1,)), ((), ())),
                                             preferred_element_type=F32)
            s = src[:, ls]
            if masked:
                s = s + dmask_scr[:, ls]
            m_prev = m_all[:, ls]
            m_new = jnp.maximum(m_prev, jnp.max(s, axis=0, keepdims=True) + cj)
            p = jnp.exp(s - (m_new - cj))
            alpha = jnp.exp(m_prev - m_new)
            acc_out.append(alpha * acc_all[:, ls] + jnp.dot(vt, p.astype(BF16),
                                                            preferred_element_type=F32))
            m_out.append(m_new)
        m_scr[...] = jnp.concatenate(m_out, axis=1)
        acc_scr[...] = jnp.concatenate(acc_out, axis=1)

    kaug0 = kaug_scr[0:tq, :]
    for ls in lanes:
        sa_scr[:, ls] = lax.dot_general(kaug0, qq[ls], (((1,), (1,)), ((), ())),
                                        preferred_element_type=F32)

    def pair(jj, carry):
        j = 2 * jj
        step(j, sa_scr, j + 1, sb_scr, False)
        step(j + 1, sb_scr, j + 2, sa_scr, False)
        return carry

    lax.fori_loop(0, qi // 2, pair, 0)

    @pl.when(qi % 2 == 0)
    def _():
        step(qi, sa_scr, None, None, True)

    @pl.when(qi % 2 == 1)
    def _():
        step(qi - 1, sa_scr, qi, sb_scr, False)
        step(qi, sb_scr, None, None, True)

    lam = (jnp.exp(jnp.sum(lq1_ref[...] * lk1_ref[...], axis=-1, keepdims=True))
           - jnp.exp(jnp.sum(lq2_ref[...] * lk2_ref[...], axis=-1, keepdims=True)) + lam_init)
    acc = acc_scr[...]
    ot = acc[0:dv] / acc[dv:dv + 1]
    o = (ot[:, :tq] - lam * ot[:, tq:]).T
    o = o * lax.rsqrt(jnp.mean(o * o, axis=-1, keepdims=True) + LN_EPS) * g_ref[...] * (1.0 - lam_init)
    o_ref[...] = o.astype(o_ref.dtype)


def _attn(p3, lq1, lk1, lq2, lk2, g, lam_init, tq=512):
    b, s, _ = p3.shape
    h = N_HEADS_DIFF
    slopes = jnp.asarray(2.0 ** (-8.0 * np.arange(1, h + 1) / h), dtype=F32)
    vec = pl.BlockSpec((1, HEAD_DIM_DIFF), lambda bi, hi, qi, sl: (0, 0))
    return pl.pallas_call(
        functools.partial(_attn_kernel, tq=tq, lam_init=lam_init),
        out_shape=jax.ShapeDtypeStruct((b, s, h * LANES), BF16),
        grid_spec=pltpu.PrefetchScalarGridSpec(
            num_scalar_prefetch=1,
            grid=(b, h, s // tq),
            in_specs=[pl.BlockSpec((None, tq, LANES), lambda bi, hi, qi, sl: (bi, qi, OFF_QA // LANES + hi)),
                      pl.BlockSpec((None, s, LANES), lambda bi, hi, qi, sl: (bi, 0, OFF_KA // LANES + hi)),
                      pl.BlockSpec((None, s, LANES), lambda bi, hi, qi, sl: (bi, 0, OFF_VA // LANES + hi)),
                      vec, vec, vec, vec,
                      pl.BlockSpec((1, LANES), lambda bi, hi, qi, sl: (0, 0))],
            out_specs=pl.BlockSpec((None, tq, LANES), lambda bi, hi, qi, sl: (bi, qi, hi)),
            scratch_shapes=[pltpu.VMEM((s, 2 * LANES), BF16),
                            pltpu.VMEM((s // tq, LANES + ONES_ROWS, tq), BF16),
                            pltpu.VMEM((tq, 2 * tq), F32),
                            pltpu.VMEM((tq, 2 * tq), F32),
                            pltpu.VMEM((tq, 2 * tq), F32),
                            pltpu.VMEM((LANES + ONES_ROWS, 2 * tq), F32),
                            pltpu.VMEM((1, 2 * tq), F32)]),
        compiler_params=_cparams(("arbitrary", "arbitrary", "arbitrary")),
    )(slopes, p3, p3, p3, lq1, lk1, lq2, lk2, g)


GLA_HEADS_PER_STEP = 2


def _gla_kernel(q_ref, k_ref, v_ref, gb_ref, la_ref, g_ref, o_ref, state_scr, *, tt):
    c = GLA_CHUNK
    dk, dv = KEY_DIM_GLA, VAL_DIM_GLA
    hp = state_scr.shape[0]

    @pl.when(pl.program_id(2) == 0)
    def _():
        state_scr[...] = jnp.zeros(state_scr.shape, F32)

    rr = lax.broadcasted_iota(jnp.int32, (c, c), 0)
    cr = lax.broadcasted_iota(jnp.int32, (c, c), 1)
    causal = cr <= rr
    tri = jnp.where(causal, 1.0, 0.0).astype(BF16)
    qscale = dk ** -0.5

    sts = [state_scr[hh] for hh in range(hp)]
    for ci in range(tt // c):
        sl = slice(ci * c, (ci + 1) * c)
        for hh in range(hp):
            ks = slice(hh * dk, (hh + 1) * dk)
            vs = slice(hh * dv, (hh + 1) * dv)
            st = sts[hh]
            la = la_ref[sl, ks]
            la_hi = la.astype(BF16)
            la_lo = (la - la_hi.astype(F32)).astype(BF16)
            cum2 = jnp.dot(tri, jnp.concatenate([la_hi, la_lo], axis=1), preferred_element_type=F32)
            cum_c = cum2[:, :dk] + cum2[:, dk:]
            last = cum_c[c - 1:c]
            mid = cum_c[c // 2:c // 2 + 1]
            q_c = q_ref[sl, ks].astype(F32) * qscale
            k_c = k_ref[sl, ks].astype(F32)
            v_c = v_ref[sl, vs]
            qe = (q_c * jnp.exp(cum_c)).astype(BF16)
            o_inter = lax.dot_general(qe, st.astype(BF16), (((1,), (1,)), ((), ())),
                                      preferred_element_type=F32)
            q2 = (q_c * jnp.exp(cum_c - mid)).astype(BF16)
            k2 = (k_c * jnp.exp(mid - cum_c)).astype(BF16)
            att = lax.dot_general(q2, k2, (((1,), (1,)), ((), ())), preferred_element_type=F32)
            att = jnp.where(causal, att, 0.0).astype(BF16)
            o = o_inter + jnp.dot(att, v_c, preferred_element_type=F32)
            kd = (k_c * jnp.exp(last - cum_c)).astype(BF16)
            v_t = v_c.astype(F32).T.astype(BF16)
            sts[hh] = st * jnp.exp(last) + jnp.dot(v_t, kd, preferred_element_type=F32)

            o = o * lax.rsqrt(jnp.mean(o * o, axis=-1, keepdims=True) + LN_EPS) * g_ref[...]
            gate = gb_ref[sl, vs].astype(F32)
            o_ref[sl, vs] = (o * (gate * _sigmoid(gate))).astype(o_ref.dtype)
    for hh in range(hp):
        state_scr[hh] = sts[hh]


def _gla(p3, la3, g, tt=512):
    b, s, _ = p3.shape
    hp = GLA_HEADS_PER_STEP
    dk, dv = KEY_DIM_GLA * hp, VAL_DIM_GLA * hp
    return pl.pallas_call(
        functools.partial(_gla_kernel, tt=tt),
        out_shape=jax.ShapeDtypeStruct((b, s, N_HEADS_GLA * VAL_DIM_GLA), BF16),
        grid=(b, N_HEADS_GLA // hp, s // tt),
        in_specs=[pl.BlockSpec((None, tt, dk), lambda bi, hi, ti: (bi, ti, OFF_QB // dk + hi)),
                  pl.BlockSpec((None, tt, dk), lambda bi, hi, ti: (bi, ti, OFF_KB // dk + hi)),
                  pl.BlockSpec((None, tt, dv), lambda bi, hi, ti: (bi, ti, OFF_VB // dv + hi)),
                  pl.BlockSpec((None, tt, dv), lambda bi, hi, ti: (bi, ti, OFF_GB // dv + hi)),
                  pl.BlockSpec((None, tt, dk), lambda bi, hi, ti: (bi, ti, hi)),
                  pl.BlockSpec((1, VAL_DIM_GLA), lambda bi, hi, ti: (0, 0))],
        out_specs=pl.BlockSpec((None, tt, dv), lambda bi, hi, ti: (bi, ti, hi)),
        scratch_shapes=[pltpu.VMEM((hp, VAL_DIM_GLA, KEY_DIM_GLA), F32)],
        compiler_params=_cparams(("arbitrary", "arbitrary", "arbitrary")),
    )(p3, p3, p3, p3, la3, g)


def _mix_kernel(oa_ref, ob_ref, gt_ref, x_ref, bg_ref, g1_ref, sh2_ref, sc2_ref, lng_ref, lnb_ref,
                wba_ref, wbb_ref, wo_ref, wrh_ref, wrl_ref, br_ref,
                x1_ref, u2_ref, lg_ref, *, alpha):
    d = x_ref.shape[-1]
    a = jnp.dot(oa_ref[...], wba_ref[...], preferred_element_type=F32)
    bm = jnp.dot(ob_ref[...], wbb_ref[...], preferred_element_type=F32)
    gates = _sigmoid(gt_ref[...].astype(F32) + bg_ref[...])
    mixed = gates[:, :d] * a + gates[:, d:] * bm
    y = jnp.dot(mixed.astype(BF16), wo_ref[...], preferred_element_type=F32)
    x1 = _standardize(alpha * x_ref[...] + g1_ref[0] * y) * lng_ref[...] + lnb_ref[...]
    x1_ref[...] = x1
    u2 = _standardize(x1) * (1.0 + sc2_ref[0]) + sh2_ref[0]
    u2_ref[...] = u2
    uh = u2.astype(BF16)
    ul = (u2 - uh.astype(F32)).astype(BF16)
    lg_ref[...] = (jnp.dot(uh, wrh_ref[...], preferred_element_type=F32)
                   + jnp.dot(uh, wrl_ref[...], preferred_element_type=F32)
                   + jnp.dot(ul, wrh_ref[...], preferred_element_type=F32) + br_ref[...])


def _mix(oa, ob, p, x2, bg, g1, sh2, sc2, lng, lnb, wba, wbb, wo, wrh, wrl, br, seq, alpha, tm=512):
    n, d = x2.shape
    tpb = seq // tm
    row = lambda i: (i, 0)
    const = lambda i: (0, 0)
    per_b = pl.BlockSpec((1, 1, d), lambda i: (i // tpb, 0, 0))
    return pl.pallas_call(
        functools.partial(_mix_kernel, alpha=alpha),
        out_shape=(jax.ShapeDtypeStruct((n, d), F32),
                   jax.ShapeDtypeStruct((n, d), F32),
                   jax.ShapeDtypeStruct((n, LANES), F32)),
        grid=(n // tm,),
        in_specs=[pl.BlockSpec((tm, d), row), pl.BlockSpec((tm, d), row),
                  pl.BlockSpec((tm, 2 * d), lambda i: (i, OFF_GATES // (2 * d))),
                  pl.BlockSpec((tm, d), row),
                  pl.BlockSpec((1, 2 * d), const), per_b, per_b, per_b,
                  pl.BlockSpec((1, d), const), pl.BlockSpec((1, d), const),
                  pl.BlockSpec((d, d), const), pl.BlockSpec((d, d), const), pl.BlockSpec((d, d), const),
                  pl.BlockSpec((d, LANES), const), pl.BlockSpec((d, LANES), const),
                  pl.BlockSpec((1, LANES), const)],
        out_specs=(pl.BlockSpec((tm, d), row), pl.BlockSpec((tm, d), row),
                   pl.BlockSpec((tm, LANES), row)),
        compiler_params=_cparams(("arbitrary",)),
    )(oa, ob, p, x2, bg, g1, sh2, sc2, lng, lnb, wba, wbb, wo, wrh, wrl, br)


def _route_kernel(lg_ref, ri_ref, rw_ref, cnt_ref, *, tm):
    @pl.when(pl.program_id(0) == 0)
    def _():
        cnt_ref[...] = jnp.zeros(cnt_ref.shape, F32)

    lg = lg_ref[...]
    lane = lax.broadcasted_iota(jnp.int32, lg.shape, 1)
    lanef = lane.astype(F32)
    far = float(LANES)
    is_g = lane < N_GROUPS
    gl = jnp.where(is_g, lg, NEG_BIG)
    gmax = jnp.max(gl, axis=-1, keepdims=True)
    gidx = jnp.min(jnp.where(gl == gmax, lanef, far), axis=-1, keepdims=True)
    gw = 1.0 / jnp.sum(jnp.where(is_g, jnp.exp(gl - gmax), 0.0), axis=-1, keepdims=True)
    lo = N_GROUPS + gidx * EXPERTS_PER_GROUP
    in_g = (lanef >= lo) & (lanef < lo + EXPERTS_PER_GROUP)
    el = jnp.where(in_g, lg, NEG_BIG)
    v1 = jnp.max(el, axis=-1, keepdims=True)
    i1 = jnp.min(jnp.where(in_g & (el == v1), lanef, far), axis=-1, keepdims=True)
    in_g2 = in_g & (lanef != i1)
    el2 = jnp.where(in_g2, lg, NEG_BIG)
    v2 = jnp.max(el2, axis=-1, keepdims=True)
    i2 = jnp.min(jnp.where(in_g2 & (el2 == v2), lanef, far), axis=-1, keepdims=True)
    t = jnp.exp(v2 - v1)
    w1 = gw / (1.0 + t)
    w2 = gw * t / (1.0 + t)

    oh1 = lanef == i1
    oh2 = lanef == i2
    oh = jnp.where(oh1 | oh2, 1.0, 0.0)
    r = lax.broadcasted_iota(jnp.int32, (tm, tm), 0)
    c = lax.broadcasted_iota(jnp.int32, (tm, tm), 1)
    lower = jnp.where(c < r, 1.0, 0.0).astype(BF16)
    base = jnp.dot(lower, oh.astype(BF16), preferred_element_type=F32) + cnt_ref[0:1, :]
    r1 = jnp.sum(jnp.where(oh1, base, 0.0), axis=-1, keepdims=True)
    r2 = jnp.sum(jnp.where(oh2, base, 0.0), axis=-1, keepdims=True)
    cnt_ref[...] = cnt_ref[...] + jnp.sum(oh, axis=0, keepdims=True)

    e1 = i1 - float(N_GROUPS)
    e2 = i2 - float(N_GROUPS)
    ri = jnp.where(lane == 0, e1, jnp.where(lane == 1, e2, jnp.where(lane == 2, r1, jnp.where(lane == 3, r2, 0.0))))
    ri_ref[...] = ri.astype(jnp.int32)
    rw_ref[...] = jnp.where(lane == 0, w1, jnp.where(lane == 1, w2, 0.0))


def _route(lg, tm=512):
    n = lg.shape[0]
    row = lambda i: (i, 0)
    return pl.pallas_call(
        functools.partial(_route_kernel, tm=tm),
        out_shape=(jax.ShapeDtypeStruct((n, LANES), jnp.int32),
                   jax.ShapeDtypeStruct((n, LANES), F32),
                   jax.ShapeDtypeStruct((8, LANES), F32)),
        grid=(n // tm,),
        in_specs=[pl.BlockSpec((tm, LANES), row)],
        out_specs=(pl.BlockSpec((tm, LANES), row), pl.BlockSpec((tm, LANES), row),
                   pl.BlockSpec((8, LANES), lambda i: (0, 0))),
        compiler_params=_cparams(("arbitrary",)),
    )(lg)


def _scatter_kernel(pos_ref, u_ref, xs_in_ref, xs_ref, sem, *, tm):
    del xs_in_ref

    def row_copy(t, j):
        return pltpu.make_async_copy(u_ref.at[pl.ds(t, 1)], xs_ref.at[pl.ds(pos_ref[0, j, t], 1)], sem)

    def issue(t, carry):
        row_copy(t, 0).start()
        row_copy(t, 1).start()
        return carry

    lax.fori_loop(0, tm, issue, 0)
    for _ in range(2):
        pltpu.make_async_copy(u_ref, xs_ref.at[pl.ds(0, tm)], sem).wait()


def _scatter(pos3, u2, n_rows, tm=256):
    n, d = u2.shape
    xs0 = jnp.zeros((n_rows, d), F32)
    return pl.pallas_call(
        functools.partial(_scatter_kernel, tm=tm),
        out_shape=jax.ShapeDtypeStruct((n_rows, d), F32),
        grid=(n // tm,),
        in_specs=[pl.BlockSpec((1, 2, tm), lambda i: (i, 0, 0), memory_space=pltpu.SMEM),
                  pl.BlockSpec((tm, d), lambda i: (i, 0)),
                  pl.BlockSpec(memory_space=pl.ANY)],
        out_specs=pl.BlockSpec(memory_space=pl.ANY),
        scratch_shapes=[pltpu.SemaphoreType.DMA(())],
        input_output_aliases={2: 0},
        compiler_params=_cparams(("arbitrary",), has_side_effects=True),
    )(pos3, u2, xs0)


def _expert_kernel(be_ref, nu_ref, x_ref, wg_ref, wu_ref, wd_ref, y_ref, wg_scr, wu_scr, wd_scr):
    i = pl.program_id(0)
    used = i < nu_ref[0]

    @pl.when(used & ((i == 0) | (be_ref[i] != be_ref[jnp.maximum(i - 1, 0)])))
    def _():
        wg_scr[...] = wg_ref[0].astype(BF16)
        wu_scr[...] = wu_ref[0].astype(BF16)
        wd_scr[...] = wd_ref[0].astype(BF16)

    @pl.when(used)
    def _():
        x = x_ref[...].astype(BF16)
        g = jnp.dot(x, wg_scr[...], preferred_element_type=F32)
        u = jnp.dot(x, wu_scr[...], preferred_element_type=F32)
        hid = (g * _sigmoid(g) * u).astype(BF16)
        y_ref[...] = jnp.dot(hid, wd_scr[...], preferred_element_type=F32)

    @pl.when(jnp.logical_not(used))
    def _():
        y_ref[...] = jnp.zeros(y_ref.shape, F32)


def _experts(blk_e, n_used, xs, wg, wu, wd, layer, blk):
    n_rows, d = xs.shape
    de = wg.shape[-1]
    wspec = lambda r, c: pl.BlockSpec((None, 1, r, c), lambda i, be, nu: (layer, be[i], 0, 0))
    return pl.pallas_call(
        _expert_kernel,
        out_shape=jax.ShapeDtypeStruct((n_rows, d), F32),
        grid_spec=pltpu.PrefetchScalarGridSpec(
            num_scalar_prefetch=2,
            grid=(n_rows // blk,),
            in_specs=[pl.BlockSpec((blk, d), lambda i, be, nu: (i, 0)),
                      wspec(d, de), wspec(d, de), wspec(de, d)],
            out_specs=pl.BlockSpec((blk, d), lambda i, be, nu: (i, 0)),
            scratch_shapes=[pltpu.VMEM((d, de), BF16), pltpu.VMEM((d, de), BF16),
                            pltpu.VMEM((de, d), BF16)]),
        compiler_params=_cparams(("arbitrary",)),
    )(blk_e, n_used, xs, wg, wu, wd)


def _combine_kernel(pos_ref, x1_ref, rw_ref, g2_ref, lng_ref, lnb_ref, yb_ref, o_ref,
                    y1_scr, y2_scr, sem, *, tm, alpha):
    def row_copy(t, j, dst):
        return pltpu.make_async_copy(yb_ref.at[pl.ds(pos_ref[0, j, t], 1)], dst.at[pl.ds(t, 1)], sem)

    def issue(t, carry):
        row_copy(t, 0, y1_scr).start()
        row_copy(t, 1, y2_scr).start()
        return carry

    lax.fori_loop(0, tm, issue, 0)
    for dst in (y1_scr, y2_scr):
        pltpu.make_async_copy(yb_ref.at[pl.ds(0, tm)], dst, sem).wait()
    rw = rw_ref[...]
    y = rw[:, 0:1] * y1_scr[...] + rw[:, 1:2] * y2_scr[...]
    o_ref[...] = _standardize(alpha * x1_ref[...] + g2_ref[0] * y) * lng_ref[...] + lnb_ref[...]


def _combine(pos3, x1, rw, g2, lng, lnb, yb, seq, alpha, tm=256):
    n, d = x1.shape
    tpb = seq // tm
    const = lambda i: (0, 0)
    return pl.pallas_call(
        functools.partial(_combine_kernel, tm=tm, alpha=alpha),
        out_shape=jax.ShapeDtypeStruct((n, d), F32),
        grid=(n // tm,),
        in_specs=[pl.BlockSpec((1, 2, tm), lambda i: (i, 0, 0), memory_space=pltpu.SMEM),
                  pl.BlockSpec((tm, d), lambda i: (i, 0)),
                  pl.BlockSpec((tm, LANES), lambda i: (i, 0)),
                  pl.BlockSpec((1, 1, d), lambda i: (i // tpb, 0, 0)),
                  pl.BlockSpec((1, d), const), pl.BlockSpec((1, d), const),
                  pl.BlockSpec(memory_space=pl.ANY)],
        out_specs=pl.BlockSpec((tm, d), lambda i: (i, 0)),
        scratch_shapes=[pltpu.VMEM((tm, d), F32), pltpu.VMEM((tm, d), F32),
                        pltpu.SemaphoreType.DMA(())],
        compiler_params=_cparams(("arbitrary",)),
    )(pos3, x1, rw, g2, lng, lnb, yb)


MOE_ROWS = 256


def kernel(x, c, w_ada, b_ada, w_in, b_gates, w_alpha, b_alpha, lambda_q1, lambda_k1, lambda_q2, lambda_k2, diff_norm_g, gla_norm_g, w_branch_a, w_branch_b, w_out, ln1_g, ln1_b, w_router_g, b_router_g, w_router_e, b_router_e, w_gate_e, w_up_e, w_down_e, ln2_g, ln2_b):
    b, s, d = x.shape
    depth = w_ada.shape[0]
    n = b * s
    alpha = (2.0 * depth) ** 0.25
    blk = MOE_ROWS
    n_rows = 2 * n + N_EXPERTS * blk
    n_blocks = n_rows // blk
    tm_rows = 256

    ada = _ada(c, w_ada, b_ada)
    x2 = x.reshape(n, d)
    for l in range(depth):
        sh1, sc1, g1, sh2, sc2, g2 = [ada[l, :, i * d:(i + 1) * d].reshape(b, 1, d) for i in range(6)]
        wl = w_in[l]
        w_main = jnp.concatenate([wl[:, :6144], wl[:, 6144 + GLA_RANK:]], axis=1).astype(BF16)
        w_ab = jnp.pad(wl[:, 6144:6144 + GLA_RANK], ((0, 0), (0, LANES - GLA_RANK))).astype(BF16)
        wal = jnp.pad(w_alpha[l], ((0, LANES - GLA_RANK), (0, 0))).astype(BF16)
        p, la = _inproj(x2, sh1, sc1, w_main, w_ab, wal, b_alpha[l].reshape(1, -1), s)
        p3 = p.reshape(b, s, W_MAIN)

        lam_init = 0.8 - 0.6 * math.exp(-0.3 * l)
        oa = _attn(p3, lambda_q1[l].reshape(1, -1), lambda_k1[l].reshape(1, -1),
                   lambda_q2[l].reshape(1, -1), lambda_k2[l].reshape(1, -1),
                   diff_norm_g[l].reshape(1, -1), lam_init)
        ob = _gla(p3, la.reshape(b, s, -1), gla_norm_g[l].reshape(1, -1))

        wr = jnp.pad(jnp.concatenate([w_router_g[l], w_router_e[l]], axis=1),
                     ((0, 0), (0, LANES - N_GROUPS - N_EXPERTS)))
        wrh = wr.astype(BF16)
        wrl = (wr - wrh.astype(F32)).astype(BF16)
        br = jnp.pad(jnp.concatenate([b_router_g[l], b_router_e[l]]),
                     (0, LANES - N_GROUPS - N_EXPERTS)).reshape(1, LANES)
        x1, u2, lg = _mix(oa.reshape(n, d), ob.reshape(n, d), p, x2, b_gates[l].reshape(1, -1),
                          g1, sh2, sc2, ln1_g[l].reshape(1, d), ln1_b[l].reshape(1, d),
                          w_branch_a[l].astype(BF16), w_branch_b[l].astype(BF16), w_out[l].astype(BF16),
                          wrh, wrl, br, s, alpha)

        ri, rw, cnt = _route(lg)
        counts = cnt[0, N_GROUPS:N_GROUPS + N_EXPERTS].astype(jnp.int32)
        padded = ((counts + blk - 1) // blk) * blk
        pends = jnp.cumsum(padded)
        pstarts = pends - padded
        pos = jnp.take(pstarts, ri[:, 0:2], axis=0) + ri[:, 2:4]
        pos3 = pos.reshape(n // tm_rows, tm_rows, 2).transpose(0, 2, 1)
        blk_start = jnp.arange(n_blocks, dtype=jnp.int32) * blk
        blk_e = jnp.minimum(jnp.sum(blk_start[:, None] >= pends[None, :], axis=1), N_EXPERTS - 1).astype(jnp.int32)
        n_used = (pends[-1:] // blk).astype(jnp.int32)

        xs = _scatter(pos3, u2, n_rows, tm=tm_rows)
        yb = _experts(blk_e, n_used, xs, w_gate_e, w_up_e, w_down_e, l, blk)
        x2 = _combine(pos3, x1, rw, g2, ln2_g[l].reshape(1, d), ln2_b[l].reshape(1, d), yb, s, alpha,
                      tm=tm_rows)
    return x2.reshape(b, s, d)
```

```python
import functools
import math

import jax
import jax.numpy as jnp
import numpy as np
from jax import lax
from jax.experimental import pallas as pl
from jax.experimental.pallas import tpu as pltpu

F32 = jnp.float32
BF16 = jnp.bfloat16

N_HEADS_DIFF = 8
HEAD_DIM_DIFF = 64
N_HEADS_GLA = 4
KEY_DIM_GLA = 128
VAL_DIM_GLA = 256
GLA_RANK = 16
GLA_TAU = 16.0
GLA_CHUNK = 64
N_GROUPS = 4
EXPERTS_PER_GROUP = 8
N_EXPERTS = N_GROUPS * EXPERTS_PER_GROUP
LN_EPS = 1e-5
LANES = 128
NEG_BIG = -1e30

OFF_QA, OFF_KA, OFF_VA = 0, 1024, 2048
OFF_QB, OFF_KB, OFF_VB, OFF_GB, OFF_GATES = 3072, 3584, 4096, 5120, 6144
W_MAIN = 8192

VMEM_LIMIT = 56 * 1024 * 1024


def _cparams(sem, **kw):
    return pltpu.CompilerParams(dimension_semantics=sem, vmem_limit_bytes=VMEM_LIMIT, **kw)


def _standardize(x):
    mu = jnp.mean(x, axis=-1, keepdims=True)
    xc = x - mu
    var = jnp.mean(xc * xc, axis=-1, keepdims=True)
    return xc * lax.rsqrt(var + LN_EPS)


def _sigmoid(x):
    return 1.0 / (1.0 + jnp.exp(-x))


def _ada_kernel(c_ref, w_ref, b_ref, o_ref):
    c = c_ref[...]
    cond = c * _sigmoid(c)
    o_ref[0] = jnp.dot(cond, w_ref[0], preferred_element_type=F32,
                       precision=lax.Precision.HIGHEST) + b_ref[0]


def _ada(c, w_ada, b_ada):
    depth, d, d6 = w_ada.shape
    b = c.shape[0]
    return pl.pallas_call(
        _ada_kernel,
        out_shape=jax.ShapeDtypeStruct((depth, b, d6), F32),
        grid=(depth, d6 // d),
        in_specs=[pl.BlockSpec((b, d), lambda l, j: (0, 0)),
                  pl.BlockSpec((1, d, d), lambda l, j: (l, 0, j)),
                  pl.BlockSpec((1, 1, d), lambda l, j: (l, 0, j))],
        out_specs=pl.BlockSpec((1, b, d), lambda l, j: (l, 0, j)),
        compiler_params=_cparams(("arbitrary", "arbitrary")),
    )(c, w_ada, b_ada.reshape(depth, 1, d6))


def _inproj_kernel(x_ref, sh_ref, sc_ref, w_ref, wab_ref, wal_ref, bal_ref,
                   p_ref, la_ref, u_scr):
    @pl.when(pl.program_id(1) == 0)
    def _():
        u = _standardize(x_ref[...]) * (1.0 + sc_ref[0]) + sh_ref[0]
        ub = u.astype(BF16)
        u_scr[...] = ub
        ab = jnp.dot(ub, wab_ref[...], preferred_element_type=F32)
        pre = jnp.dot(ab.astype(BF16), wal_ref[...], preferred_element_type=F32) + bal_ref[...]
        la_ref[...] = (jnp.minimum(pre, 0.0) - jnp.log(1.0 + jnp.exp(-jnp.abs(pre)))) * (1.0 / GLA_TAU)

    p_ref[...] = jnp.dot(u_scr[...], w_ref[...], preferred_element_type=F32).astype(BF16)


def _inproj(x2, sh, sc, w_main, w_ab, w_alpha, b_alpha, seq, tm=1024, tn=1024):
    n, d = x2.shape
    tpb = seq // tm
    wq = w_alpha.shape[1]
    return pl.pallas_call(
        _inproj_kernel,
        out_shape=(jax.ShapeDtypeStruct((n, W_MAIN), BF16),
                   jax.ShapeDtypeStruct((n, wq), F32)),
        grid=(n // tm, W_MAIN // tn),
        in_specs=[pl.BlockSpec((tm, d), lambda i, j: (i, 0)),
                  pl.BlockSpec((1, 1, d), lambda i, j: (i // tpb, 0, 0)),
                  pl.BlockSpec((1, 1, d), lambda i, j: (i // tpb, 0, 0)),
                  pl.BlockSpec((d, tn), lambda i, j: (0, j)),
                  pl.BlockSpec((d, LANES), lambda i, j: (0, 0)),
                  pl.BlockSpec((LANES, wq), lambda i, j: (0, 0)),
                  pl.BlockSpec((1, wq), lambda i, j: (0, 0))],
        out_specs=(pl.BlockSpec((tm, tn), lambda i, j: (i, j)),
                   pl.BlockSpec((tm, wq), lambda i, j: (i, 0))),
        scratch_shapes=[pltpu.VMEM((tm, d), BF16)],
        compiler_params=_cparams(("arbitrary", "arbitrary")),
    )(x2, sh, sc, w_main, w_ab, w_alpha, b_alpha)


ONES_ROWS = 16


QUERY_GROUP = 256
POS_SPLIT = 32


def _attn_kernel(slopes_ref, q_ref, k_ref, v_ref, lq1_ref, lk1_ref, lq2_ref, lk2_ref, g_ref,
                 o_ref, kaug_scr, vt_scr, dmask_scr, sa_scr, sb_scr, acc_scr, m_scr, *, tq, lam_init):
    h = pl.program_id(1)
    qi = pl.program_id(2)
    slope = slopes_ref[h]
    dh = HEAD_DIM_DIFF
    dv = LANES
    n_chunks = vt_scr.shape[0]

    @pl.when(qi == 0)
    def _():
        for j in range(n_chunks):
            vt_scr[j, 0:dv, :] = v_ref[j * tq:(j + 1) * tq, :].astype(F32).T.astype(BF16)
            vt_scr[j, dv:dv + ONES_ROWS, :] = jnp.ones((ONES_ROWS, tq), BF16)
        kaug_scr[:, 0:LANES] = k_ref[...]
        koff = lax.broadcasted_iota(jnp.int32, (tq, LANES), 0)
        flane = lax.broadcasted_iota(jnp.int32, (tq, LANES), 1)
        feat = jnp.where(flane == 0, koff // POS_SPLIT, jnp.where(flane == 1, koff % POS_SPLIT, 0))
        feat = feat.astype(F32).astype(BF16)
        for j in range(n_chunks):
            kaug_scr[j * tq:(j + 1) * tq, LANES:2 * LANES] = feat
        kpos = lax.broadcasted_iota(jnp.int32, (tq, 2 * tq), 0)
        qpos = lax.broadcasted_iota(jnp.int32, (tq, 2 * tq), 1)
        qoff = jnp.where(qpos >= tq, qpos - tq, qpos)
        dmask_scr[...] = jnp.where(qoff >= kpos, 0.0, NEG_BIG)

    q = q_ref[...] * jnp.asarray(dh ** -0.5, BF16)
    lane = lax.broadcasted_iota(jnp.int32, q.shape, 1)
    zero = jnp.zeros_like(q)
    qf = jnp.where(lane == 0, slope * POS_SPLIT, jnp.where(lane == 1, slope, 0.0)).astype(BF16)
    qq = jnp.concatenate([jnp.concatenate([jnp.where(lane < dh, q, zero), qf], axis=1),
                          jnp.concatenate([jnp.where(lane >= dh, q, zero), qf], axis=1)], axis=0)

    m_scr[...] = jnp.full(m_scr.shape, NEG_BIG, F32)
    acc_scr[...] = jnp.zeros(acc_scr.shape, F32)

    lanes = [slice(g * QUERY_GROUP, (g + 1) * QUERY_GROUP) for g in range(2 * tq // QUERY_GROUP)]

    def step(j_cur, src, j_next, dst, masked):
        cj = (-slope) * ((qi - j_cur) * tq).astype(F32)
        if dst is not None:
            kaug = kaug_scr[pl.ds(pl.multiple_of(j_next * tq, tq), tq), :]
        vt = vt_scr[j_cur]
        m_all = m_scr[...]
        acc_all = acc_scr[...]
        m_out, acc_out = [], []
        for ls in lanes:
            if dst is not None:
                dst[:, ls] = lax.dot_general(kaug, qq[ls], (((1,), (1,)), ((), ())),
                                             preferred_element_type=F32)
            s = src[:, ls]
            if masked:
                s = s + dmask_scr[:, ls]
            m_prev = m_all[:, ls]
            m_new = jnp.maximum(m_prev, jnp.max(s, axis=0, keepdims=True) + cj)
            p = jnp.exp(s - (m_new - cj))
            alpha = jnp.exp(m_prev - m_new)
            acc_out.append(alpha * acc_all[:, ls] + jnp.dot(vt, p.astype(BF16),
                                                            preferred_element_type=F32))
            m_out.append(m_new)
        m_scr[...] = jnp.concatenate(m_out, axis=1)
        acc_scr[...] = jnp.concatenate(acc_out, axis=1)

    kaug0 = kaug_scr[0:tq, :]
    for ls in lanes:
        sa_scr[:, ls] = lax.dot_general(kaug0, qq[ls], (((1,), (1,)), ((), ())),
                                        preferred_element_type=F32)

    def pair(jj, carry):
        j = 2 * jj
        step(j, sa_scr, j + 1, sb_scr, False)
        step(j + 1, sb_scr, j + 2, sa_scr, False)
        return carry

    lax.fori_loop(0, qi // 2, pair, 0)

    @pl.when(qi % 2 == 0)
    def _():
        step(qi, sa_scr, None, None, True)

    @pl.when(qi % 2 == 1)
    def _():
        step(qi - 1, sa_scr, qi, sb_scr, False)
        step(qi, sb_scr, None, None, True)

    lam = (jnp.exp(jnp.sum(lq1_ref[...] * lk1_ref[...], axis=-1, keepdims=True))
           - jnp.exp(jnp.sum(lq2_ref[...] * lk2_ref[...], axis=-1, keepdims=True)) + lam_init)
    acc = acc_scr[...]
    ot = acc[0:dv] / acc[dv:dv + 1]
    o = (ot[:, :tq] - lam * ot[:, tq:]).T
    o = o * lax.rsqrt(jnp.mean(o * o, axis=-1, keepdims=True) + LN_EPS) * g_ref[...] * (1.0 - lam_init)
    o_ref[...] = o.astype(o_ref.dtype)


def _attn(p3, lq1, lk1, lq2, lk2, g, lam_init, tq=512):
    b, s, _ = p3.shape
    h = N_HEADS_DIFF
    slopes = jnp.asarray(2.0 ** (-8.0 * np.arange(1, h + 1) / h), dtype=F32)
    vec = pl.BlockSpec((1, HEAD_DIM_DIFF), lambda bi, hi, qi, sl: (0, 0))
    return pl.pallas_call(
        functools.partial(_attn_kernel, tq=tq, lam_init=lam_init),
        out_shape=jax.ShapeDtypeStruct((b, s, h * LANES), BF16),
        grid_spec=pltpu.PrefetchScalarGridSpec(
            num_scalar_prefetch=1,
            grid=(b, h, s // tq),
            in_specs=[pl.BlockSpec((None, tq, LANES), lambda bi, hi, qi, sl: (bi, qi, OFF_QA // LANES + hi)),
                      pl.BlockSpec((None, s, LANES), lambda bi, hi, qi, sl: (bi, 0, OFF_KA // LANES + hi)),
                      pl.BlockSpec((None, s, LANES), lambda bi, hi, qi, sl: (bi, 0, OFF_VA // LANES + hi)),
                      vec, vec, vec, vec,
                      pl.BlockSpec((1, LANES), lambda bi, hi, qi, sl: (0, 0))],
            out_specs=pl.BlockSpec((None, tq, LANES), lambda bi, hi, qi, sl: (bi, qi, hi)),
            scratch_shapes=[pltpu.VMEM((s, 2 * LANES), BF16),
                            pltpu.VMEM((s // tq, LANES + ONES_ROWS, tq), BF16),
                            pltpu.VMEM((tq, 2 * tq), F32),
                            pltpu.VMEM((tq, 2 * tq), F32),
                            pltpu.VMEM((tq, 2 * tq), F32),
                            pltpu.VMEM((LANES + ONES_ROWS, 2 * tq), F32),
                            pltpu.VMEM((1, 2 * tq), F32)]),
        compiler_params=_cparams(("arbitrary", "arbitrary", "arbitrary")),
    )(slopes, p3, p3, p3, lq1, lk1, lq2, lk2, g)


GLA_HEADS_PER_STEP = 2


def _gla_kernel(q_ref, k_ref, v_ref, gb_ref, la_ref, g_ref, o_ref, state_scr, *, tt):
    c = GLA_CHUNK
    dk, dv = KEY_DIM_GLA, VAL_DIM_GLA
    hp = state_scr.shape[0]

    @pl.when(pl.program_id(2) == 0)
    def _():
        state_scr[...] = jnp.zeros(state_scr.shape, F32)

    rr = lax.broadcasted_iota(jnp.int32, (c, c), 0)
    cr = lax.broadcasted_iota(jnp.int32, (c, c), 1)
    causal = cr <= rr
    tri = jnp.where(causal, 1.0, 0.0).astype(BF16)
    qscale = dk ** -0.5

    sts = [state_scr[hh] for hh in range(hp)]
    for ci in range(tt // c):
        sl = slice(ci * c, (ci + 1) * c)
        for hh in range(hp):
            ks = slice(hh * dk, (hh + 1) * dk)
            vs = slice(hh * dv, (hh + 1) * dv)
            st = sts[hh]
            la = la_ref[sl, ks]
            la_hi = la.astype(BF16)
            la_lo = (la - la_hi.astype(F32)).astype(BF16)
            cum2 = jnp.dot(tri, jnp.concatenate([la_hi, la_lo], axis=1), preferred_element_type=F32)
            cum_c = cum2[:, :dk] + cum2[:, dk:]
            last = cum_c[c - 1:c]
            mid = cum_c[c // 2:c // 2 + 1]
            q_c = q_ref[sl, ks].astype(F32) * qscale
            k_c = k_ref[sl, ks].astype(F32)
            v_c = v_ref[sl, vs]
            qe = (q_c * jnp.exp(cum_c)).astype(BF16)
            o_inter = lax.dot_general(qe, st.astype(BF16), (((1,), (1,)), ((), ())),
                                      preferred_element_type=F32)
            q2 = (q_c * jnp.exp(cum_c - mid)).astype(BF16)
            k2 = (k_c * jnp.exp(mid - cum_c)).astype(BF16)
            att = lax.dot_general(q2, k2, (((1,), (1,)), ((), ())), preferred_element_type=F32)
            att = jnp.where(causal, att, 0.0).astype(BF16)
            o = o_inter + jnp.dot(att, v_c, preferred_element_type=F32)
            kd = (k_c * jnp.exp(last - cum_c)).astype(BF16)
            v_t = v_c.astype(F32).T.astype(BF16)
            sts[hh] = st * jnp.exp(last) + jnp.dot(v_t, kd, preferred_element_type=F32)

            o = o * lax.rsqrt(jnp.mean(o * o, axis=-1, keepdims=True) + LN_EPS) * g_ref[...]
            gate = gb_ref[sl, vs].astype(F32)
            o_ref[sl, vs] = (o * (gate * _sigmoid(gate))).astype(o_ref.dtype)
    for hh in range(hp):
        state_scr[hh] = sts[hh]


def _gla(p3, la3, g, tt=512):
    b, s, _ = p3.shape
    hp = GLA_HEADS_PER_STEP
    dk, dv = KEY_DIM_GLA * hp, VAL_DIM_GLA * hp
    return pl.pallas_call(
        functools.partial(_gla_kernel, tt=tt),
        out_shape=jax.ShapeDtypeStruct((b, s, N_HEADS_GLA * VAL_DIM_GLA), BF16),
        grid=(b, N_HEADS_GLA // hp, s // tt),
        in_specs=[pl.BlockSpec((None, tt, dk), lambda bi, hi, ti: (bi, ti, OFF_QB // dk + hi)),
                  pl.BlockSpec((None, tt, dk), lambda bi, hi, ti: (bi, ti, OFF_KB // dk + hi)),
                  pl.BlockSpec((None, tt, dv), lambda bi, hi, ti: (bi, ti, OFF_VB // dv + hi)),
                  pl.BlockSpec((None, tt, dv), lambda bi, hi, ti: (bi, ti, OFF_GB // dv + hi)),
                  pl.BlockSpec((None, tt, dk), lambda bi, hi, ti: (bi, ti, hi)),
                  pl.BlockSpec((1, VAL_DIM_GLA), lambda bi, hi, ti: (0, 0))],
        out_specs=pl.BlockSpec((None, tt, dv), lambda bi, hi, ti: (bi, ti, hi)),
        scratch_shapes=[pltpu.VMEM((hp, VAL_DIM_GLA, KEY_DIM_GLA), F32)],
        compiler_params=_cparams(("arbitrary", "arbitrary", "arbitrary")),
    )(p3, p3, p3, p3, la3, g)


def _mix_kernel(oa_ref, ob_ref, gt_ref, x_ref, bg_ref, g1_ref, sh2_ref, sc2_ref, lng_ref, lnb_ref,
                wba_ref, wbb_ref, wo_ref, wrh_ref, wrl_ref, br_ref,
                x1_ref, u2_ref, lg_ref, *, alpha):
    d = x_ref.shape[-1]
    a = jnp.dot(oa_ref[...], wba_ref[...], preferred_element_type=F32)
    bm = jnp.dot(ob_ref[...], wbb_ref[...], preferred_element_type=F32)
    gates = _sigmoid(gt_ref[...].astype(F32) + bg_ref[...])
    mixed = gates[:, :d] * a + gates[:, d:] * bm
    y = jnp.dot(mixed.astype(BF16), wo_ref[...], preferred_element_type=F32)
    x1 = _standardize(alpha * x_ref[...] + g1_ref[0] * y) * lng_ref[...] + lnb_ref[...]
    x1_ref[...] = x1
    u2 = _standardize(x1) * (1.0 + sc2_ref[0]) + sh2_ref[0]
    u2_ref[...] = u2
    uh = u2.astype(BF16)
    ul = (u2 - uh.astype(F32)).astype(BF16)
    lg_ref[...] = (jnp.dot(uh, wrh_ref[...], preferred_element_type=F32)
                   + jnp.dot(uh, wrl_ref[...], preferred_element_type=F32)
                   + jnp.dot(ul, wrh_ref[...], preferred_element_type=F32) + br_ref[...])


def _mix(oa, ob, p, x2, bg, g1, sh2, sc2, lng, lnb, wba, wbb, wo, wrh, wrl, br, seq, alpha, tm=512):
    n, d = x2.shape
    tpb = seq // tm
    row = lambda i: (i, 0)
    const = lambda i: (0, 0)
    per_b = pl.BlockSpec((1, 1, d), lambda i: (i // tpb, 0, 0))
    return pl.pallas_call(
        functools.partial(_mix_kernel, alpha=alpha),
        out_shape=(jax.ShapeDtypeStruct((n, d), F32),
                   jax.ShapeDtypeStruct((n, d), F32),
                   jax.ShapeDtypeStruct((n, LANES), F32)),
        grid=(n // tm,),
        in_specs=[pl.BlockSpec((tm, d), row), pl.BlockSpec((tm, d), row),
                  pl.BlockSpec((tm, 2 * d), lambda i: (i, OFF_GATES // (2 * d))),
                  pl.BlockSpec((tm, d), row),
                  pl.BlockSpec((1, 2 * d), const), per_b, per_b, per_b,
                  pl.BlockSpec((1, d), const), pl.BlockSpec((1, d), const),
                  pl.BlockSpec((d, d), const), pl.BlockSpec((d, d), const), pl.BlockSpec((d, d), const),
                  pl.BlockSpec((d, LANES), const), pl.BlockSpec((d, LANES), const),
                  pl.BlockSpec((1, LANES), const)],
        out_specs=(pl.BlockSpec((tm, d), row), pl.BlockSpec((tm, d), row),
                   pl.BlockSpec((tm, LANES), row)),
        compiler_params=_cparams(("arbitrary",)),
    )(oa, ob, p, x2, bg, g1, sh2, sc2, lng, lnb, wba, wbb, wo, wrh, wrl, br)


def _route_kernel(lg_ref, ri_ref, rw_ref, cnt_ref, *, tm):
    @pl.when(pl.program_id(0) == 0)
    def _():
        cnt_ref[...] = jnp.zeros(cnt_ref.shape, F32)

    lg = lg_ref[...]
    lane = lax.broadcasted_iota(jnp.int32, lg.shape, 1)
    lanef = lane.astype(F32)
    far = float(LANES)
    is_g = lane < N_GROUPS
    gl = jnp.where(is_g, lg, NEG_BIG)
    gmax = jnp.max(gl, axis=-1, keepdims=True)
    gidx = jnp.min(jnp.where(gl == gmax, lanef, far), axis=-1, keepdims=True)
    gw = 1.0 / jnp.sum(jnp.where(is_g, jnp.exp(gl - gmax), 0.0), axis=-1, keepdims=True)
    lo = N_GROUPS + gidx * EXPERTS_PER_GROUP
    in_g = (lanef >= lo) & (lanef < lo + EXPERTS_PER_GROUP)
    el = jnp.where(in_g, lg, NEG_BIG)
    v1 = jnp.max(el, axis=-1, keepdims=True)
    i1 = jnp.min(jnp.where(in_g & (el == v1), lanef, far), axis=-1, keepdims=True)
    in_g2 = in_g & (lanef != i1)
    el2 = jnp.where(in_g2, lg, NEG_BIG)
    v2 = jnp.max(el2, axis=-1, keepdims=True)
    i2 = jnp.min(jnp.where(in_g2 & (el2 == v2), lanef, far), axis=-1, keepdims=True)
    t = jnp.exp(v2 - v1)
    w1 = gw / (1.0 + t)
    w2 = gw * t / (1.0 + t)

    oh1 = lanef == i1
    oh2 = lanef == i2
    oh = jnp.where(oh1 | oh2, 1.0, 0.0)
    r = lax.broadcasted_iota(jnp.int32, (tm, tm), 0)
    c = lax.broadcasted_iota(jnp.int32, (tm, tm), 1)
    lower = jnp.where(c < r, 1.0, 0.0).astype(BF16)
    base = jnp.dot(lower, oh.astype(BF16), preferred_element_type=F32) + cnt_ref[0:1, :]
    r1 = jnp.sum(jnp.where(oh1, base, 0.0), axis=-1, keepdims=True)
    r2 = jnp.sum(jnp.where(oh2, base, 0.0), axis=-1, keepdims=True)
    cnt_ref[...] = cnt_ref[...] + jnp.sum(oh, axis=0, keepdims=True)

    e1 = i1 - float(N_GROUPS)
    e2 = i2 - float(N_GROUPS)
    ri = jnp.where(lane == 0, e1, jnp.where(lane == 1, e2, jnp.where(lane == 2, r1, jnp.where(lane == 3, r2, 0.0))))
    ri_ref[...] = ri.astype(jnp.int32)
    rw_ref[...] = jnp.where(lane == 0, w1, jnp.where(lane == 1, w2, 0.0))


def _route(lg, tm=512):
    n = lg.shape[0]
    row = lambda i: (i, 0)
    return pl.pallas_call(
        functools.partial(_route_kernel, tm=tm),
        out_shape=(jax.ShapeDtypeStruct((n, LANES), jnp.int32),
                   jax.ShapeDtypeStruct((n, LANES), F32),
                   jax.ShapeDtypeStruct((8, LANES), F32)),
        grid=(n // tm,),
        in_specs=[pl.BlockSpec((tm, LANES), row)],
        out_specs=(pl.BlockSpec((tm, LANES), row), pl.BlockSpec((tm, LANES), row),
                   pl.BlockSpec((8, LANES), lambda i: (0, 0))),
        compiler_params=_cparams(("arbitrary",)),
    )(lg)


MOE_ROWS = 256


def _expert_kernel(be_ref, nu_ref, tok_a_ref, tok_b_ref, dst_ref, u_ref, wg_ref, wu_ref, wd_ref, y_ref,
                   xb0, xb1, yb0, yb1, wg_scr, wu_scr, wd_scr, gsem, ssem):
    i = pl.program_id(0)
    n_used = nu_ref[0]
    blk = xb0.shape[0]
    xbs, ybs = (xb0, xb1), (yb0, yb1)

    def gather_issue(tok_ref, xb, sem):
        for r in range(blk):
            pltpu.make_async_copy(u_ref.at[pl.ds(tok_ref[0, 0, r], 1)], xb.at[pl.ds(r, 1)], sem).start()

    def gather_wait(xb, sem):
        pltpu.make_async_copy(u_ref.at[pl.ds(0, blk)], xb, sem).wait()

    def scatter_issue(yb, sem):
        for r in range(blk):
            pltpu.make_async_copy(yb.at[pl.ds(r, 1)], y_ref.at[pl.ds(dst_ref[0, 0, r], 1)], sem).start()

    def scatter_wait(yb, sem):
        pltpu.make_async_copy(yb, y_ref.at[pl.ds(0, blk)], sem).wait()

    def load_weights():
        @pl.when((i == 0) | (be_ref[i] != be_ref[jnp.maximum(i - 1, 0)]))
        def _():
            wg_scr[...] = wg_ref[0].astype(BF16)
            wu_scr[...] = wu_ref[0].astype(BF16)
            wd_scr[...] = wd_ref[0].astype(BF16)

    def compute(xb, yb):
        x = xb[...].astype(BF16)
        g = jnp.dot(x, wg_scr[...], preferred_element_type=F32)
        u = jnp.dot(x, wu_scr[...], preferred_element_type=F32)
        hid = (g * _sigmoid(g) * u).astype(BF16)
        yb[...] = jnp.dot(hid, wd_scr[...], preferred_element_type=F32)

    @pl.when(i == 0)
    def _():
        gather_issue(tok_a_ref, xb0, gsem.at[0])
        gather_wait(xb0, gsem.at[0])
        load_weights()
        gather_issue(tok_b_ref, xb1, gsem.at[1])
        compute(xb0, yb0)

    for p in range(2):
        cur, oth = p, 1 - p

        @pl.when((i >= 1) & (i < n_used) & (i % 2 == p))
        def _():
            gather_wait(xbs[cur], gsem.at[cur])

            @pl.when(i >= 2)
            def _():
                scatter_wait(ybs[cur], ssem.at[cur])

            load_weights()
            gather_issue(tok_b_ref, xbs[oth], gsem.at[oth])
            scatter_issue(ybs[oth], ssem.at[oth])
            compute(xbs[cur], ybs[cur])

        @pl.when((i == n_used) & (i % 2 == p))
        def _():
            gather_wait(xbs[cur], gsem.at[cur])

            @pl.when(i >= 2)
            def _():
                scatter_wait(ybs[cur], ssem.at[cur])

            scatter_issue(ybs[oth], ssem.at[oth])

        @pl.when((i == n_used + 1) & (i % 2 == p))
        def _():
            scatter_wait(ybs[cur], ssem.at[cur])


def _experts(blk_e, n_used, src_tok3, dst_row3, u2, wg, wu, wd, layer, n_rows):
    n, d = u2.shape
    de = wg.shape[-1]
    blk = MOE_ROWS
    n_steps = blk_e.shape[0]
    wspec = lambda r, c: pl.BlockSpec((None, 1, r, c), lambda i, be, nu: (layer, be[i], 0, 0))
    ispec = lambda f: pl.BlockSpec((1, 1, blk), lambda i, be, nu: (f(i, nu), 0, 0), memory_space=pltpu.SMEM)
    return pl.pallas_call(
        _expert_kernel,
        out_shape=jax.ShapeDtypeStruct((n_rows, d), F32),
        grid_spec=pltpu.PrefetchScalarGridSpec(
            num_scalar_prefetch=2,
            grid=(n_steps,),
            in_specs=[ispec(lambda i, nu: jnp.minimum(i, nu[0] - 1)),
                      ispec(lambda i, nu: jnp.minimum(i + 1, nu[0] - 1)),
                      ispec(lambda i, nu: jnp.clip(i - 1, 0, nu[0] - 1)),
                      pl.BlockSpec(memory_space=pl.ANY),
                      wspec(d, de), wspec(d, de), wspec(de, d)],
            out_specs=pl.BlockSpec(memory_space=pl.ANY),
            scratch_shapes=[pltpu.VMEM((blk, d), F32), pltpu.VMEM((blk, d), F32),
                            pltpu.VMEM((blk, d), F32), pltpu.VMEM((blk, d), F32),
                            pltpu.VMEM((d, de), BF16), pltpu.VMEM((d, de), BF16),
                            pltpu.VMEM((de, d), BF16),
                            pltpu.SemaphoreType.DMA((2,)), pltpu.SemaphoreType.DMA((2,))]),
        compiler_params=_cparams(("arbitrary",), has_side_effects=True),
    )(blk_e, n_used, src_tok3, src_tok3, dst_row3, u2, wg, wu, wd)


def _combine_kernel(x1_ref, y1_ref, y2_ref, rw_ref, g2_ref, lng_ref, lnb_ref, o_ref, *, alpha):
    rw = rw_ref[...]
    y = rw[:, 0:1] * y1_ref[...] + rw[:, 1:2] * y2_ref[...]
    o_ref[...] = _standardize(alpha * x1_ref[...] + g2_ref[0] * y) * lng_ref[...] + lnb_ref[...]


def _combine(x1, y_tok, rw, g2, lng, lnb, seq, alpha, tm=512):
    n, d = x1.shape
    tpb = seq // tm
    const = lambda i: (0, 0)
    return pl.pallas_call(
        functools.partial(_combine_kernel, alpha=alpha),
        out_shape=jax.ShapeDtypeStruct((n, d), F32),
        grid=(n // tm,),
        in_specs=[pl.BlockSpec((tm, d), lambda i: (i, 0)),
                  pl.BlockSpec((tm, d), lambda i: (i, 0)),
                  pl.BlockSpec((tm, d), lambda i: (n // tm + i, 0)),
                  pl.BlockSpec((tm, LANES), lambda i: (i, 0)),
                  pl.BlockSpec((1, 1, d), lambda i: (i // tpb, 0, 0)),
                  pl.BlockSpec((1, d), const), pl.BlockSpec((1, d), const)],
        out_specs=pl.BlockSpec((tm, d), lambda i: (i, 0)),
        compiler_params=_cparams(("arbitrary",)),
    )(x1, y_tok, y_tok, rw, g2, lng, lnb)


def _moe_plan(ri, cnt, n):
    blk = MOE_ROWS
    n_rows = 2 * n + N_EXPERTS * blk
    n_blocks = n_rows // blk
    n_steps = n_blocks + 2
    counts = cnt[0, N_GROUPS:N_GROUPS + N_EXPERTS].astype(jnp.int32)
    padded = ((counts + blk - 1) // blk) * blk
    pends = jnp.cumsum(padded)
    pstarts = pends - padded
    cstarts = jnp.cumsum(counts) - counts
    pos = jnp.take(pstarts, ri[:, 0:2], axis=0) + ri[:, 2:4]
    blk_start = jnp.arange(n_steps, dtype=jnp.int32) * blk
    blk_e = jnp.minimum(jnp.sum(blk_start[:, None] >= pends[None, :], axis=1), N_EXPERTS - 1).astype(jnp.int32)
    n_used = (pends[-1:] // blk).astype(jnp.int32)
    prow = jnp.arange(n_rows, dtype=jnp.int32)
    pe = jnp.repeat(blk_e[:n_blocks], blk)
    spare = 2 * n + prow - (cstarts[pe] + jnp.minimum(prow - pstarts[pe], counts[pe]))
    tok = jnp.arange(n, dtype=jnp.int32)
    src_tok = (jnp.zeros((n_rows,), jnp.int32)
               .at[pos[:, 0]].set(tok, unique_indices=True).at[pos[:, 1]].set(tok, unique_indices=True))
    dst_row = (spare.at[pos[:, 0]].set(tok, unique_indices=True)
               .at[pos[:, 1]].set(n + tok, unique_indices=True))
    return blk_e, n_used, src_tok.reshape(n_blocks, 1, blk), dst_row.reshape(n_blocks, 1, blk)


def kernel(x, c, w_ada, b_ada, w_in, b_gates, w_alpha, b_alpha, lambda_q1, lambda_k1, lambda_q2, lambda_k2, diff_norm_g, gla_norm_g, w_branch_a, w_branch_b, w_out, ln1_g, ln1_b, w_router_g, b_router_g, w_router_e, b_router_e, w_gate_e, w_up_e, w_down_e, ln2_g, ln2_b):
    b, s, d = x.shape
    depth = w_ada.shape[0]
    n = b * s
    alpha = (2.0 * depth) ** 0.25

    ada = _ada(c, w_ada, b_ada)
    x2 = x.reshape(n, d)
    for l in range(depth):
        sh1, sc1, g1, sh2, sc2, g2 = [ada[l, :, i * d:(i + 1) * d].reshape(b, 1, d) for i in range(6)]
        wl = w_in[l]
        w_main = jnp.concatenate([wl[:, :6144], wl[:, 6144 + GLA_RANK:]], axis=1).astype(BF16)
        w_ab = jnp.pad(wl[:, 6144:6144 + GLA_RANK], ((0, 0), (0, LANES - GLA_RANK))).astype(BF16)
        wal = jnp.pad(w_alpha[l], ((0, LANES - GLA_RANK), (0, 0))).astype(BF16)
        p, la = _inproj(x2, sh1, sc1, w_main, w_ab, wal, b_alpha[l].reshape(1, -1), s)
        p3 = p.reshape(b, s, W_MAIN)

        lam_init = 0.8 - 0.6 * math.exp(-0.3 * l)
        oa = _attn(p3, lambda_q1[l].reshape(1, -1), lambda_k1[l].reshape(1, -1),
                   lambda_q2[l].reshape(1, -1), lambda_k2[l].reshape(1, -1),
                   diff_norm_g[l].reshape(1, -1), lam_init)
        ob = _gla(p3, la.reshape(b, s, -1), gla_norm_g[l].reshape(1, -1))

        wr = jnp.pad(jnp.concatenate([w_router_g[l], w_router_e[l]], axis=1),
                     ((0, 0), (0, LANES - N_GROUPS - N_EXPERTS)))
        wrh = wr.astype(BF16)
        wrl = (wr - wrh.astype(F32)).astype(BF16)
        br = jnp.pad(jnp.concatenate([b_router_g[l], b_router_e[l]]),
                     (0, LANES - N_GROUPS - N_EXPERTS)).reshape(1, LANES)
        x1, u2, lg = _mix(oa.reshape(n, d), ob.reshape(n, d), p, x2, b_gates[l].reshape(1, -1),
                          g1, sh2, sc2, ln1_g[l].reshape(1, d), ln1_b[l].reshape(1, d),
                          w_branch_a[l].astype(BF16), w_branch_b[l].astype(BF16), w_out[l].astype(BF16),
                          wrh, wrl, br, s, alpha)

        ri, rw, cnt = _route(lg)
        blk_e, n_used, src_tok3, dst_row3 = _moe_plan(ri, cnt, n)
        y_tok = _experts(blk_e, n_used, src_tok3, dst_row3, u2, w_gate_e, w_up_e, w_down_e, l,
                         2 * n + N_EXPERTS * MOE_ROWS)
        x2 = _combine(x1, y_tok, rw, g2, ln2_g[l].reshape(1, d), ln2_b[l].reshape(1, d), s, alpha)
    return x2.reshape(b, s, d)
```

```python
import functools
import math

import jax
import jax.numpy as jnp
import numpy as np
from jax import lax
from jax.experimental import pallas as pl
from jax.experimental.pallas import tpu as pltpu

F32 = jnp.float32
BF16 = jnp.bfloat16

N_HEADS_DIFF = 8
HEAD_DIM_DIFF = 64
N_HEADS_GLA = 4
KEY_DIM_GLA = 128
VAL_DIM_GLA = 256
GLA_RANK = 16
GLA_TAU = 16.0
GLA_CHUNK = 64
N_GROUPS = 4
EXPERTS_PER_GROUP = 8
N_EXPERTS = N_GROUPS * EXPERTS_PER_GROUP
LN_EPS = 1e-5
LANES = 128
NEG_BIG = -1e30

OFF_QA, OFF_KA, OFF_VA = 0, 1024, 2048
OFF_QB, OFF_KB, OFF_VB, OFF_GB, OFF_GATES = 3072, 3584, 4096, 5120, 6144
W_MAIN = 8192

VMEM_LIMIT = 56 * 1024 * 1024


def _cparams(sem, **kw):
    return pltpu.CompilerParams(dimension_semantics=sem, vmem_limit_bytes=VMEM_LIMIT, **kw)


def _standardize(x):
    mu = jnp.mean(x, axis=-1, keepdims=True)
    xc = x - mu
    var = jnp.mean(xc * xc, axis=-1, keepdims=True)
    return xc * lax.rsqrt(var + LN_EPS)


def _sigmoid(x):
    return 1.0 / (1.0 + jnp.exp(-x))


ROW_TILES = 8


def _store_token_tiles(ref, x):
    t = x.shape[0]
    for s in range(ROW_TILES):
        ref[pl.ds(s, t, stride=ROW_TILES), :] = x[:, s * LANES:(s + 1) * LANES]


def _load_token_tiles(ref):
    t = ref.shape[0] // ROW_TILES
    return jnp.concatenate([ref[pl.ds(s, t, stride=ROW_TILES), :] for s in range(ROW_TILES)], axis=1)


def _ada_kernel(c_ref, w_ref, b_ref, o_ref):
    c = c_ref[...]
    cond = c * _sigmoid(c)
    o_ref[0] = jnp.dot(cond, w_ref[0], preferred_element_type=F32,
                       precision=lax.Precision.HIGHEST) + b_ref[0]


def _ada(c, w_ada, b_ada):
    depth, d, d6 = w_ada.shape
    b = c.shape[0]
    return pl.pallas_call(
        _ada_kernel,
        out_shape=jax.ShapeDtypeStruct((depth, b, d6), F32),
        grid=(depth, d6 // d),
        in_specs=[pl.BlockSpec((b, d), lambda l, j: (0, 0)),
                  pl.BlockSpec((1, d, d), lambda l, j: (l, 0, j)),
                  pl.BlockSpec((1, 1, d), lambda l, j: (l, 0, j))],
        out_specs=pl.BlockSpec((1, b, d), lambda l, j: (l, 0, j)),
        compiler_params=_cparams(("arbitrary", "arbitrary")),
    )(c, w_ada, b_ada.reshape(depth, 1, d6))


def _inproj_kernel(x_ref, sh_ref, sc_ref, w_ref, wab_ref, wal_ref, bal_ref,
                   p_ref, la_ref, u_scr):
    @pl.when(pl.program_id(1) == 0)
    def _():
        u = _standardize(x_ref[...]) * (1.0 + sc_ref[0]) + sh_ref[0]
        ub = u.astype(BF16)
        u_scr[...] = ub
        ab = jnp.dot(ub, wab_ref[...], preferred_element_type=F32)
        pre = jnp.dot(ab.astype(BF16), wal_ref[...], preferred_element_type=F32) + bal_ref[...]
        la_ref[...] = (jnp.minimum(pre, 0.0) - jnp.log(1.0 + jnp.exp(-jnp.abs(pre)))) * (1.0 / GLA_TAU)

    p_ref[...] = jnp.dot(u_scr[...], w_ref[...], preferred_element_type=F32).astype(BF16)


def _inproj(x2, sh, sc, w_main, w_ab, w_alpha, b_alpha, seq, tm=1024, tn=1024):
    n, d = x2.shape
    tpb = seq // tm
    wq = w_alpha.shape[1]
    return pl.pallas_call(
        _inproj_kernel,
        out_shape=(jax.ShapeDtypeStruct((n, W_MAIN), BF16),
                   jax.ShapeDtypeStruct((n, wq), F32)),
        grid=(n // tm, W_MAIN // tn),
        in_specs=[pl.BlockSpec((tm, d), lambda i, j: (i, 0)),
                  pl.BlockSpec((1, 1, d), lambda i, j: (i // tpb, 0, 0)),
                  pl.BlockSpec((1, 1, d), lambda i, j: (i // tpb, 0, 0)),
                  pl.BlockSpec((d, tn), lambda i, j: (0, j)),
                  pl.BlockSpec((d, LANES), lambda i, j: (0, 0)),
                  pl.BlockSpec((LANES, wq), lambda i, j: (0, 0)),
                  pl.BlockSpec((1, wq), lambda i, j: (0, 0))],
        out_specs=(pl.BlockSpec((tm, tn), lambda i, j: (i, j)),
                   pl.BlockSpec((tm, wq), lambda i, j: (i, 0))),
        scratch_shapes=[pltpu.VMEM((tm, d), BF16)],
        compiler_params=_cparams(("arbitrary", "arbitrary")),
    )(x2, sh, sc, w_main, w_ab, w_alpha, b_alpha)


ONES_ROWS = 16


QUERY_GROUP = 256
POS_SPLIT = 32


def _attn_kernel(slopes_ref, q_ref, k_ref, v_ref, lq1_ref, lk1_ref, lq2_ref, lk2_ref, g_ref,
                 o_ref, kaug_scr, vt_scr, dmask_scr, sa_scr, sb_scr, acc_scr, m_scr, *, tq, lam_init):
    h = pl.program_id(1)
    qi = pl.program_id(2)
    slope = slopes_ref[h]
    dh = HEAD_DIM_DIFF
    dv = LANES
    n_chunks = vt_scr.shape[0]

    @pl.when(qi == 0)
    def _():
        for j in range(n_chunks):
            vt_scr[j, 0:dv, :] = v_ref[j * tq:(j + 1) * tq, :].astype(F32).T.astype(BF16)
            vt_scr[j, dv:dv + ONES_ROWS, :] = jnp.ones((ONES_ROWS, tq), BF16)
        kaug_scr[:, 0:LANES] = k_ref[...]
        koff = lax.broadcasted_iota(jnp.int32, (tq, LANES), 0)
        flane = lax.broadcasted_iota(jnp.int32, (tq, LANES), 1)
        feat = jnp.where(flane == 0, koff // POS_SPLIT, jnp.where(flane == 1, koff % POS_SPLIT, 0))
        feat = feat.astype(F32).astype(BF16)
        for j in range(n_chunks):
            kaug_scr[j * tq:(j + 1) * tq, LANES:2 * LANES] = feat
        kpos = lax.broadcasted_iota(jnp.int32, (tq, 2 * tq), 0)
        qpos = lax.broadcasted_iota(jnp.int32, (tq, 2 * tq), 1)
        qoff = jnp.where(qpos >= tq, qpos - tq, qpos)
        dmask_scr[...] = jnp.where(qoff >= kpos, 0.0, NEG_BIG)

    q = q_ref[...] * jnp.asarray(dh ** -0.5, BF16)
    lane = lax.broadcasted_iota(jnp.int32, q.shape, 1)
    zero = jnp.zeros_like(q)
    qf = jnp.where(lane == 0, slope * POS_SPLIT, jnp.where(lane == 1, slope, 0.0)).astype(BF16)
    qq = jnp.concatenate([jnp.concatenate([jnp.where(lane < dh, q, zero), qf], axis=1),
                          jnp.concatenate([jnp.where(lane >= dh, q, zero), qf], axis=1)], axis=0)

    m_scr[...] = jnp.full(m_scr.shape, NEG_BIG, F32)
    acc_scr[...] = jnp.zeros(acc_scr.shape, F32)

    lanes = [slice(g * QUERY_GROUP, (g + 1) * QUERY_GROUP) for g in range(2 * tq // QUERY_GROUP)]

    def step(j_cur, src, j_next, dst, masked):
        cj = (-slope) * ((qi - j_cur) * tq).astype(F32)
        if dst is not None:
            kaug = kaug_scr[pl.ds(pl.multiple_of(j_next * tq, tq), tq), :]
        vt = vt_scr[j_cur]
        m_all = m_scr[...]
        acc_all = acc_scr[...]
        m_out, acc_out = [], []
        for ls in lanes:
            if dst is not None:
                dst[:, ls] = lax.dot_general(kaug, qq[ls], (((1,), (1,)), ((), ())),
                                             preferred_element_type=F32)
            s = src[:, ls]
            if masked:
                s = s + dmask_scr[:, ls]
            m_prev = m_all[:, ls]
            m_new = jnp.maximum(m_prev, jnp.max(s, axis=0, keepdims=True) + cj)
            p = jnp.exp(s - (m_new - cj))
            alpha = jnp.exp(m_prev - m_new)
            acc_out.append(alpha * acc_all[:, ls] + jnp.dot(vt, p.astype(BF16),
                                                            preferred_element_type=F32))
            m_out.append(m_new)
        m_scr[...] = jnp.concatenate(m_out, axis=1)
        acc_scr[...] = jnp.concatenate(acc_out, axis=1)

    kaug0 = kaug_scr[0:tq, :]
    for ls in lanes:
        sa_scr[:, ls] = lax.dot_general(kaug0, qq[ls], (((1,), (1,)), ((), ())),
                                        preferred_element_type=F32)

    def pair(jj, carry):
        j = 2 * jj
        step(j, sa_scr, j + 1, sb_scr, False)
        step(j + 1, sb_scr, j + 2, sa_scr, False)
        return carry

    lax.fori_loop(0, qi // 2, pair, 0)

    @pl.when(qi % 2 == 0)
    def _():
        step(qi, sa_scr, None, None, True)

    @pl.when(qi % 2 == 1)
    def _():
        step(qi - 1, sa_scr, qi, sb_scr, False)
        step(qi, sb_scr, None, None, True)

    lam = (jnp.exp(jnp.sum(lq1_ref[...] * lk1_ref[...], axis=-1, keepdims=True))
           - jnp.exp(jnp.sum(lq2_ref[...] * lk2_ref[...], axis=-1, keepdims=True)) + lam_init)
    acc = acc_scr[...]
    ot = acc[0:dv] / acc[dv:dv + 1]
    o = (ot[:, :tq] - lam * ot[:, tq:]).T
    o = o * lax.rsqrt(jnp.mean(o * o, axis=-1, keepdims=True) + LN_EPS) * g_ref[...] * (1.0 - lam_init)
    o_ref[...] = o.astype(o_ref.dtype)


def _attn(p3, lq1, lk1, lq2, lk2, g, lam_init, tq=512):
    b, s, _ = p3.shape
    h = N_HEADS_DIFF
    slopes = jnp.asarray(2.0 ** (-8.0 * np.arange(1, h + 1) / h), dtype=F32)
    vec = pl.BlockSpec((1, HEAD_DIM_DIFF), lambda bi, hi, qi, sl: (0, 0))
    return pl.pallas_call(
        functools.partial(_attn_kernel, tq=tq, lam_init=lam_init),
        out_shape=jax.ShapeDtypeStruct((b, s, h * LANES), BF16),
        grid_spec=pltpu.PrefetchScalarGridSpec(
            num_scalar_prefetch=1,
            grid=(b, h, s // tq),
            in_specs=[pl.BlockSpec((None, tq, LANES), lambda bi, hi, qi, sl: (bi, qi, OFF_QA // LANES + hi)),
                      pl.BlockSpec((None, s, LANES), lambda bi, hi, qi, sl: (bi, 0, OFF_KA // LANES + hi)),
                      pl.BlockSpec((None, s, LANES), lambda bi, hi, qi, sl: (bi, 0, OFF_VA // LANES + hi)),
                      vec, vec, vec, vec,
                      pl.BlockSpec((1, LANES), lambda bi, hi, qi, sl: (0, 0))],
            out_specs=pl.BlockSpec((None, tq, LANES), lambda bi, hi, qi, sl: (bi, qi, hi)),
            scratch_shapes=[pltpu.VMEM((s, 2 * LANES), BF16),
                            pltpu.VMEM((s // tq, LANES + ONES_ROWS, tq), BF16),
                            pltpu.VMEM((tq, 2 * tq), F32),
                            pltpu.VMEM((tq, 2 * tq), F32),
                            pltpu.VMEM((tq, 2 * tq), F32),
                            pltpu.VMEM((LANES + ONES_ROWS, 2 * tq), F32),
                            pltpu.VMEM((1, 2 * tq), F32)]),
        compiler_params=_cparams(("arbitrary", "arbitrary", "arbitrary")),
    )(slopes, p3, p3, p3, lq1, lk1, lq2, lk2, g)


GLA_HEADS_PER_STEP = 2


def _gla_kernel(q_ref, k_ref, v_ref, gb_ref, la_ref, g_ref, o_ref, state_scr, *, tt):
    c = GLA_CHUNK
    dk, dv = KEY_DIM_GLA, VAL_DIM_GLA
    hp = state_scr.shape[0]

    @pl.when(pl.program_id(2) == 0)
    def _():
        state_scr[...] = jnp.zeros(state_scr.shape, F32)

    rr = lax.broadcasted_iota(jnp.int32, (c, c), 0)
    cr = lax.broadcasted_iota(jnp.int32, (c, c), 1)
    causal = cr <= rr
    tri = jnp.where(causal, 1.0, 0.0).astype(BF16)
    qscale = dk ** -0.5

    sts = [state_scr[hh] for hh in range(hp)]
    for ci in range(tt // c):
        sl = slice(ci * c, (ci + 1) * c)
        for hh in range(hp):
            ks = slice(hh * dk, (hh + 1) * dk)
            vs = slice(hh * dv, (hh + 1) * dv)
            st = sts[hh]
            la = la_ref[sl, ks]
            la_hi = la.astype(BF16)
            la_lo = (la - la_hi.astype(F32)).astype(BF16)
            cum2 = jnp.dot(tri, jnp.concatenate([la_hi, la_lo], axis=1), preferred_element_type=F32)
            cum_c = cum2[:, :dk] + cum2[:, dk:]
            last = cum_c[c - 1:c]
            mid = cum_c[c // 2:c // 2 + 1]
            q_c = q_ref[sl, ks].astype(F32) * qscale
            k_c = k_ref[sl, ks].astype(F32)
            v_c = v_ref[sl, vs]
            qe = (q_c * jnp.exp(cum_c)).astype(BF16)
            o_inter = lax.dot_general(qe, st.astype(BF16), (((1,), (1,)), ((), ())),
                                      preferred_element_type=F32)
            q2 = (q_c * jnp.exp(cum_c - mid)).astype(BF16)
            k2 = (k_c * jnp.exp(mid - cum_c)).astype(BF16)
            att = lax.dot_general(q2, k2, (((1,), (1,)), ((), ())), preferred_element_type=F32)
            att = jnp.where(causal, att, 0.0).astype(BF16)
            o = o_inter + jnp.dot(att, v_c, preferred_element_type=F32)
            kd = (k_c * jnp.exp(last - cum_c)).astype(BF16)
            v_t = v_c.astype(F32).T.astype(BF16)
            sts[hh] = st * jnp.exp(last) + jnp.dot(v_t, kd, preferred_element_type=F32)

            o = o * lax.rsqrt(jnp.mean(o * o, axis=-1, keepdims=True) + LN_EPS) * g_ref[...]
            gate = gb_ref[sl, vs].astype(F32)
            o_ref[sl, vs] = (o * (gate * _sigmoid(gate))).astype(o_ref.dtype)
    for hh in range(hp):
        state_scr[hh] = sts[hh]


def _gla(p3, la3, g, tt=512):
    b, s, _ = p3.shape
    hp = GLA_HEADS_PER_STEP
    dk, dv = KEY_DIM_GLA * hp, VAL_DIM_GLA * hp
    return pl.pallas_call(
        functools.partial(_gla_kernel, tt=tt),
        out_shape=jax.ShapeDtypeStruct((b, s, N_HEADS_GLA * VAL_DIM_GLA), BF16),
        grid=(b, N_HEADS_GLA // hp, s // tt),
        in_specs=[pl.BlockSpec((None, tt, dk), lambda bi, hi, ti: (bi, ti, OFF_QB // dk + hi)),
                  pl.BlockSpec((None, tt, dk), lambda bi, hi, ti: (bi, ti, OFF_KB // dk + hi)),
                  pl.BlockSpec((None, tt, dv), lambda bi, hi, ti: (bi, ti, OFF_VB // dv + hi)),
                  pl.BlockSpec((None, tt, dv), lambda bi, hi, ti: (bi, ti, OFF_GB // dv + hi)),
                  pl.BlockSpec((None, tt, dk), lambda bi, hi, ti: (bi, ti, hi)),
                  pl.BlockSpec((1, VAL_DIM_GLA), lambda bi, hi, ti: (0, 0))],
        out_specs=pl.BlockSpec((None, tt, dv), lambda bi, hi, ti: (bi, ti, hi)),
        scratch_shapes=[pltpu.VMEM((hp, VAL_DIM_GLA, KEY_DIM_GLA), F32)],
        compiler_params=_cparams(("arbitrary", "arbitrary", "arbitrary")),
    )(p3, p3, p3, p3, la3, g)


def _mix_kernel(oa_ref, ob_ref, gt_ref, x_ref, bg_ref, g1_ref, sh2_ref, sc2_ref, lng_ref, lnb_ref,
                wba_ref, wbb_ref, wo_ref, wrh_ref, wrl_ref, br_ref,
                x1_ref, u2_ref, lg_ref, *, alpha):
    d = x_ref.shape[-1]
    a = jnp.dot(oa_ref[...], wba_ref[...], preferred_element_type=F32)
    bm = jnp.dot(ob_ref[...], wbb_ref[...], preferred_element_type=F32)
    gates = _sigmoid(gt_ref[...].astype(F32) + bg_ref[...])
    mixed = gates[:, :d] * a + gates[:, d:] * bm
    y = jnp.dot(mixed.astype(BF16), wo_ref[...], preferred_element_type=F32)
    x1 = _standardize(alpha * x_ref[...] + g1_ref[0] * y) * lng_ref[...] + lnb_ref[...]
    x1_ref[...] = x1
    u2 = _standardize(x1) * (1.0 + sc2_ref[0]) + sh2_ref[0]
    _store_token_tiles(u2_ref, u2)
    uh = u2.astype(BF16)
    ul = (u2 - uh.astype(F32)).astype(BF16)
    lg_ref[...] = (jnp.dot(uh, wrh_ref[...], preferred_element_type=F32)
                   + jnp.dot(uh, wrl_ref[...], preferred_element_type=F32)
                   + jnp.dot(ul, wrh_ref[...], preferred_element_type=F32) + br_ref[...])


def _mix(oa, ob, p, x2, bg, g1, sh2, sc2, lng, lnb, wba, wbb, wo, wrh, wrl, br, seq, alpha, tm=512):
    n, d = x2.shape
    tpb = seq // tm
    row = lambda i: (i, 0)
    const = lambda i: (0, 0)
    per_b = pl.BlockSpec((1, 1, d), lambda i: (i // tpb, 0, 0))
    return pl.pallas_call(
        functools.partial(_mix_kernel, alpha=alpha),
        out_shape=(jax.ShapeDtypeStruct((n, d), F32),
                   jax.ShapeDtypeStruct((n * (d // LANES), LANES), F32),
                   jax.ShapeDtypeStruct((n, LANES), F32)),
        grid=(n // tm,),
        in_specs=[pl.BlockSpec((tm, d), row), pl.BlockSpec((tm, d), row),
                  pl.BlockSpec((tm, 2 * d), lambda i: (i, OFF_GATES // (2 * d))),
                  pl.BlockSpec((tm, d), row),
                  pl.BlockSpec((1, 2 * d), const), per_b, per_b, per_b,
                  pl.BlockSpec((1, d), const), pl.BlockSpec((1, d), const),
                  pl.BlockSpec((d, d), const), pl.BlockSpec((d, d), const), pl.BlockSpec((d, d), const),
                  pl.BlockSpec((d, LANES), const), pl.BlockSpec((d, LANES), const),
                  pl.BlockSpec((1, LANES), const)],
        out_specs=(pl.BlockSpec((tm, d), row), pl.BlockSpec((tm * (d // LANES), LANES), row),
                   pl.BlockSpec((tm, LANES), row)),
        compiler_params=_cparams(("arbitrary",)),
    )(oa, ob, p, x2, bg, g1, sh2, sc2, lng, lnb, wba, wbb, wo, wrh, wrl, br)


def _route_kernel(lg_ref, ri_ref, rw_ref, cnt_ref, *, tm):
    @pl.when(pl.program_id(0) == 0)
    def _():
        cnt_ref[...] = jnp.zeros(cnt_ref.shape, F32)

    lg = lg_ref[...]
    lane = lax.broadcasted_iota(jnp.int32, lg.shape, 1)
    lanef = lane.astype(F32)
    far = float(LANES)
    is_g = lane < N_GROUPS
    gl = jnp.where(is_g, lg, NEG_BIG)
    gmax = jnp.max(gl, axis=-1, keepdims=True)
    gidx = jnp.min(jnp.where(gl == gmax, lanef, far), axis=-1, keepdims=True)
    gw = 1.0 / jnp.sum(jnp.where(is_g, jnp.exp(gl - gmax), 0.0), axis=-1, keepdims=True)
    lo = N_GROUPS + gidx * EXPERTS_PER_GROUP
    in_g = (lanef >= lo) & (lanef < lo + EXPERTS_PER_GROUP)
    el = jnp.where(in_g, lg, NEG_BIG)
    v1 = jnp.max(el, axis=-1, keepdims=True)
    i1 = jnp.min(jnp.where(in_g & (el == v1), lanef, far), axis=-1, keepdims=True)
    in_g2 = in_g & (lanef != i1)
    el2 = jnp.where(in_g2, lg, NEG_BIG)
    v2 = jnp.max(el2, axis=-1, keepdims=True)
    i2 = jnp.min(jnp.where(in_g2 & (el2 == v2), lanef, far), axis=-1, keepdims=True)
    t = jnp.exp(v2 - v1)
    w1 = gw / (1.0 + t)
    w2 = gw * t / (1.0 + t)

    oh1 = lanef == i1
    oh2 = lanef == i2
    oh = jnp.where(oh1 | oh2, 1.0, 0.0)
    r = lax.broadcasted_iota(jnp.int32, (tm, tm), 0)
    c = lax.broadcasted_iota(jnp.int32, (tm, tm), 1)
    lower = jnp.where(c < r, 1.0, 0.0).astype(BF16)
    base = jnp.dot(lower, oh.astype(BF16), preferred_element_type=F32) + cnt_ref[0:1, :]
    r1 = jnp.sum(jnp.where(oh1, base, 0.0), axis=-1, keepdims=True)
    r2 = jnp.sum(jnp.where(oh2, base, 0.0), axis=-1, keepdims=True)
    cnt_ref[...] = cnt_ref[...] + jnp.sum(oh, axis=0, keepdims=True)

    e1 = i1 - float(N_GROUPS)
    e2 = i2 - float(N_GROUPS)
    ri = jnp.where(lane == 0, e1, jnp.where(lane == 1, e2, jnp.where(lane == 2, r1, jnp.where(lane == 3, r2, 0.0))))
    ri_ref[...] = ri.astype(jnp.int32)
    rw_ref[...] = jnp.where(lane == 0, w1, jnp.where(lane == 1, w2, 0.0))


def _route(lg, tm=512):
    n = lg.shape[0]
    row = lambda i: (i, 0)
    return pl.pallas_call(
        functools.partial(_route_kernel, tm=tm),
        out_shape=(jax.ShapeDtypeStruct((n, LANES), jnp.int32),
                   jax.ShapeDtypeStruct((n, LANES), F32),
                   jax.ShapeDtypeStruct((8, LANES), F32)),
        grid=(n // tm,),
        in_specs=[pl.BlockSpec((tm, LANES), row)],
        out_specs=(pl.BlockSpec((tm, LANES), row), pl.BlockSpec((tm, LANES), row),
                   pl.BlockSpec((8, LANES), lambda i: (0, 0))),
        compiler_params=_cparams(("arbitrary",)),
    )(lg)


MOE_ROWS = 256


def _expert_kernel(be_ref, nu_ref, tok_a_ref, tok_b_ref, dst_ref, u_ref, wg_ref, wu_ref, wd_ref, y_ref,
                   xb0, xb1, yb0, yb1, wg_scr, wu_scr, wd_scr, gsem, ssem):
    i = pl.program_id(0)
    n_used = nu_ref[0]
    blk = tok_a_ref.shape[-1]
    xbs, ybs = (xb0, xb1), (yb0, yb1)

    def tile(ref, t):
        return ref.at[pl.ds(pl.multiple_of(t * ROW_TILES, ROW_TILES), ROW_TILES)]

    def gather_issue(tok_ref, xb, sem):
        for r in range(blk):
            pltpu.make_async_copy(tile(u_ref, tok_ref[0, 0, r]), tile(xb, r), sem).start()

    def gather_wait(xb, sem):
        pltpu.make_async_copy(u_ref.at[pl.ds(0, blk * ROW_TILES)], xb, sem).wait()

    def scatter_issue(yb, sem):
        for r in range(blk):
            pltpu.make_async_copy(tile(yb, r), tile(y_ref, dst_ref[0, 0, r]), sem).start()

    def scatter_wait(yb, sem):
        pltpu.make_async_copy(yb, y_ref.at[pl.ds(0, blk * ROW_TILES)], sem).wait()

    def load_weights():
        @pl.when((i == 0) | (be_ref[i] != be_ref[jnp.maximum(i - 1, 0)]))
        def _():
            wg_scr[...] = wg_ref[0].astype(BF16)
            wu_scr[...] = wu_ref[0].astype(BF16)
            wd_scr[...] = wd_ref[0].astype(BF16)

    def compute(xb, yb):
        x = _load_token_tiles(xb).astype(BF16)
        g = jnp.dot(x, wg_scr[...], preferred_element_type=F32)
        u = jnp.dot(x, wu_scr[...], preferred_element_type=F32)
        hid = (g * _sigmoid(g) * u).astype(BF16)
        _store_token_tiles(yb, jnp.dot(hid, wd_scr[...], preferred_element_type=F32))

    @pl.when(i == 0)
    def _():
        gather_issue(tok_a_ref, xb0, gsem.at[0])
        gather_wait(xb0, gsem.at[0])
        load_weights()
        gather_issue(tok_b_ref, xb1, gsem.at[1])
        compute(xb0, yb0)

    for p in range(2):
        cur, oth = p, 1 - p

        @pl.when((i >= 1) & (i < n_used) & (i % 2 == p))
        def _():
            gather_wait(xbs[cur], gsem.at[cur])

            @pl.when(i >= 2)
            def _():
                scatter_wait(ybs[cur], ssem.at[cur])

            load_weights()
            gather_issue(tok_b_ref, xbs[oth], gsem.at[oth])
            scatter_issue(ybs[oth], ssem.at[oth])
            compute(xbs[cur], ybs[cur])

        @pl.when((i == n_used) & (i % 2 == p))
        def _():
            gather_wait(xbs[cur], gsem.at[cur])

            @pl.when(i >= 2)
            def _():
                scatter_wait(ybs[cur], ssem.at[cur])

            scatter_issue(ybs[oth], ssem.at[oth])

        @pl.when((i == n_used + 1) & (i % 2 == p))
        def _():
            scatter_wait(ybs[cur], ssem.at[cur])


def _experts(blk_e, n_used, src_tok3, dst_row3, u2t, wg, wu, wd, layer, n_rows):
    d, de = wg.shape[-2:]
    blk = MOE_ROWS
    rt = d // LANES
    n_steps = blk_e.shape[0]
    wspec = lambda r, c: pl.BlockSpec((None, 1, r, c), lambda i, be, nu: (layer, be[i], 0, 0))
    ispec = lambda f: pl.BlockSpec((1, 1, blk), lambda i, be, nu: (f(i, nu), 0, 0), memory_space=pltpu.SMEM)
    return pl.pallas_call(
        _expert_kernel,
        out_shape=jax.ShapeDtypeStruct((n_rows * rt, LANES), F32),
        grid_spec=pltpu.PrefetchScalarGridSpec(
            num_scalar_prefetch=2,
            grid=(n_steps,),
            in_specs=[ispec(lambda i, nu: jnp.minimum(i, nu[0] - 1)),
                      ispec(lambda i, nu: jnp.minimum(i + 1, nu[0] - 1)),
                      ispec(lambda i, nu: jnp.clip(i - 1, 0, nu[0] - 1)),
                      pl.BlockSpec(memory_space=pl.ANY),
                      wspec(d, de), wspec(d, de), wspec(de, d)],
            out_specs=pl.BlockSpec(memory_space=pl.ANY),
            scratch_shapes=[pltpu.VMEM((blk * rt, LANES), F32), pltpu.VMEM((blk * rt, LANES), F32),
                            pltpu.VMEM((blk * rt, LANES), F32), pltpu.VMEM((blk * rt, LANES), F32),
                            pltpu.VMEM((d, de), BF16), pltpu.VMEM((d, de), BF16),
                            pltpu.VMEM((de, d), BF16),
                            pltpu.SemaphoreType.DMA((2,)), pltpu.SemaphoreType.DMA((2,))]),
        compiler_params=_cparams(("arbitrary",), has_side_effects=True),
    )(blk_e, n_used, src_tok3, src_tok3, dst_row3, u2t, wg, wu, wd)


def _combine_kernel(x1_ref, y1_ref, y2_ref, rw_ref, g2_ref, lng_ref, lnb_ref, o_ref, *, alpha):
    rw = rw_ref[...]
    y = rw[:, 0:1] * _load_token_tiles(y1_ref) + rw[:, 1:2] * _load_token_tiles(y2_ref)
    o_ref[...] = _standardize(alpha * x1_ref[...] + g2_ref[0] * y) * lng_ref[...] + lnb_ref[...]


def _combine(x1, y_tok, rw, g2, lng, lnb, seq, alpha, tm=512):
    n, d = x1.shape
    tpb = seq // tm
    const = lambda i: (0, 0)
    return pl.pallas_call(
        functools.partial(_combine_kernel, alpha=alpha),
        out_shape=jax.ShapeDtypeStruct((n, d), F32),
        grid=(n // tm,),
        in_specs=[pl.BlockSpec((tm, d), lambda i: (i, 0)),
                  pl.BlockSpec((tm * (d // LANES), LANES), lambda i: (i, 0)),
                  pl.BlockSpec((tm * (d // LANES), LANES), lambda i: (n // tm + i, 0)),
                  pl.BlockSpec((tm, LANES), lambda i: (i, 0)),
                  pl.BlockSpec((1, 1, d), lambda i: (i // tpb, 0, 0)),
                  pl.BlockSpec((1, d), const), pl.BlockSpec((1, d), const)],
        out_specs=pl.BlockSpec((tm, d), lambda i: (i, 0)),
        compiler_params=_cparams(("arbitrary",)),
    )(x1, y_tok, y_tok, rw, g2, lng, lnb)


def _moe_plan(ri, cnt, n):
    blk = MOE_ROWS
    n_rows = 2 * n + N_EXPERTS * blk
    n_blocks = n_rows // blk
    n_steps = n_blocks + 2
    counts = cnt[0, N_GROUPS:N_GROUPS + N_EXPERTS].astype(jnp.int32)
    padded = ((counts + blk - 1) // blk) * blk
    pends = jnp.cumsum(padded)
    pstarts = pends - padded
    cstarts = jnp.cumsum(counts) - counts
    pos = jnp.take(pstarts, ri[:, 0:2], axis=0) + ri[:, 2:4]
    blk_start = jnp.arange(n_steps, dtype=jnp.int32) * blk
    blk_e = jnp.minimum(jnp.sum(blk_start[:, None] >= pends[None, :], axis=1), N_EXPERTS - 1).astype(jnp.int32)
    n_used = (pends[-1:] // blk).astype(jnp.int32)
    prow = jnp.arange(n_rows, dtype=jnp.int32)
    pe = jnp.repeat(blk_e[:n_blocks], blk)
    spare = 2 * n + prow - (cstarts[pe] + jnp.minimum(prow - pstarts[pe], counts[pe]))
    dst_row = spare.at[pos.T.reshape(-1)].set(jnp.arange(2 * n, dtype=jnp.int32), unique_indices=True)
    src_tok = dst_row % n
    return blk_e, n_used, src_tok.reshape(n_blocks, 1, blk), dst_row.reshape(n_blocks, 1, blk)


def kernel(x, c, w_ada, b_ada, w_in, b_gates, w_alpha, b_alpha, lambda_q1, lambda_k1, lambda_q2, lambda_k2, diff_norm_g, gla_norm_g, w_branch_a, w_branch_b, w_out, ln1_g, ln1_b, w_router_g, b_router_g, w_router_e, b_router_e, w_gate_e, w_up_e, w_down_e, ln2_g, ln2_b):
    b, s, d = x.shape
    depth = w_ada.shape[0]
    n = b * s
    alpha = (2.0 * depth) ** 0.25

    ada = _ada(c, w_ada, b_ada)
    x2 = x.reshape(n, d)
    for l in range(depth):
        sh1, sc1, g1, sh2, sc2, g2 = [ada[l, :, i * d:(i + 1) * d].reshape(b, 1, d) for i in range(6)]
        wl = w_in[l]
        w_main = jnp.concatenate([wl[:, :6144], wl[:, 6144 + GLA_RANK:]], axis=1).astype(BF16)
        w_ab = jnp.pad(wl[:, 6144:6144 + GLA_RANK], ((0, 0), (0, LANES - GLA_RANK))).astype(BF16)
        wal = jnp.pad(w_alpha[l], ((0, LANES - GLA_RANK), (0, 0))).astype(BF16)
        p, la = _inproj(x2, sh1, sc1, w_main, w_ab, wal, b_alpha[l].reshape(1, -1), s)
        p3 = p.reshape(b, s, W_MAIN)

        lam_init = 0.8 - 0.6 * math.exp(-0.3 * l)
        oa = _attn(p3, lambda_q1[l].reshape(1, -1), lambda_k1[l].reshape(1, -1),
                   lambda_q2[l].reshape(1, -1), lambda_k2[l].reshape(1, -1),
                   diff_norm_g[l].reshape(1, -1), lam_init)
        ob = _gla(p3, la.reshape(b, s, -1), gla_norm_g[l].reshape(1, -1))

        wr = jnp.pad(jnp.concatenate([w_router_g[l], w_router_e[l]], axis=1),
                     ((0, 0), (0, LANES - N_GROUPS - N_EXPERTS)))
        wrh = wr.astype(BF16)
        wrl = (wr - wrh.astype(F32)).astype(BF16)
        br = jnp.pad(jnp.concatenate([b_router_g[l], b_router_e[l]]),
                     (0, LANES - N_GROUPS - N_EXPERTS)).reshape(1, LANES)
        x1, u2, lg = _mix(oa.reshape(n, d), ob.reshape(n, d), p, x2, b_gates[l].reshape(1, -1),
                          g1, sh2, sc2, ln1_g[l].reshape(1, d), ln1_b[l].reshape(1, d),
                          w_branch_a[l].astype(BF16), w_branch_b[l].astype(BF16), w_out[l].astype(BF16),
                          wrh, wrl, br, s, alpha)

        ri, rw, cnt = _route(lg)
        blk_e, n_used, src_tok3, dst_row3 = _moe_plan(ri, cnt, n)
        y_tok = _experts(blk_e, n_used, src_tok3, dst_row3, u2, w_gate_e, w_up_e, w_down_e, l,
                         2 * n + N_EXPERTS * MOE_ROWS)
        x2 = _combine(x1, y_tok, rw, g2, ln2_g[l].reshape(1, d), ln2_b[l].reshape(1, d), s, alpha)
    return x2.reshape(b, s, d)
```

```python
import functools
import math

import jax
import jax.numpy as jnp
import numpy as np
from jax import lax
from jax.experimental import pallas as pl
from jax.experimental.pallas import tpu as pltpu

F32 = jnp.float32
BF16 = jnp.bfloat16

N_HEADS_DIFF = 8
HEAD_DIM_DIFF = 64
N_HEADS_GLA = 4
KEY_DIM_GLA = 128
VAL_DIM_GLA = 256
GLA_RANK = 16
GLA_TAU = 16.0
GLA_CHUNK = 64
N_GROUPS = 4
EXPERTS_PER_GROUP = 8
N_EXPERTS = N_GROUPS * EXPERTS_PER_GROUP
LN_EPS = 1e-5
LANES = 128
NEG_BIG = -1e30

OFF_QA, OFF_KA, OFF_VA = 0, 1024, 2048
OFF_QB, OFF_KB, OFF_VB, OFF_GB, OFF_GATES = 3072, 3584, 4096, 5120, 6144
W_MAIN = 8192

VMEM_LIMIT = 56 * 1024 * 1024


def _cparams(sem, **kw):
    return pltpu.CompilerParams(dimension_semantics=sem, vmem_limit_bytes=VMEM_LIMIT, **kw)


def _standardize(x):
    mu = jnp.mean(x, axis=-1, keepdims=True)
    xc = x - mu
    var = jnp.mean(xc * xc, axis=-1, keepdims=True)
    return xc * lax.rsqrt(var + LN_EPS)


def _sigmoid(x):
    return 1.0 / (1.0 + jnp.exp(-x))


ROW_TILES = 8


def _store_token_tiles(ref, x):
    t = x.shape[0]
    for s in range(ROW_TILES):
        ref[pl.ds(s, t, stride=ROW_TILES), :] = x[:, s * LANES:(s + 1) * LANES]


def _load_token_tiles(ref):
    t = ref.shape[0] // ROW_TILES
    return jnp.concatenate([ref[pl.ds(s, t, stride=ROW_TILES), :] for s in range(ROW_TILES)], axis=1)


def _ada_kernel(c_ref, w_ref, b_ref, o_ref):
    c = c_ref[...]
    cond = c * _sigmoid(c)
    o_ref[0] = jnp.dot(cond, w_ref[0], preferred_element_type=F32,
                       precision=lax.Precision.HIGHEST) + b_ref[0]


def _ada(c, w_ada, b_ada):
    depth, d, d6 = w_ada.shape
    b = c.shape[0]
    return pl.pallas_call(
        _ada_kernel,
        out_shape=jax.ShapeDtypeStruct((depth, b, d6), F32),
        grid=(depth, d6 // d),
        in_specs=[pl.BlockSpec((b, d), lambda l, j: (0, 0)),
                  pl.BlockSpec((1, d, d), lambda l, j: (l, 0, j)),
                  pl.BlockSpec((1, 1, d), lambda l, j: (l, 0, j))],
        out_specs=pl.BlockSpec((1, b, d), lambda l, j: (l, 0, j)),
        compiler_params=_cparams(("arbitrary", "arbitrary")),
    )(c, w_ada, b_ada.reshape(depth, 1, d6))


def _inproj_kernel(x_ref, sh_ref, sc_ref, w_ref, wab_ref, wal_ref, bal_ref,
                   p_ref, la_ref, u_scr):
    @pl.when(pl.program_id(1) == 0)
    def _():
        u = _standardize(x_ref[...]) * (1.0 + sc_ref[0]) + sh_ref[0]
        ub = u.astype(BF16)
        u_scr[...] = ub
        ab = jnp.dot(ub, wab_ref[...], preferred_element_type=F32)
        pre = jnp.dot(ab.astype(BF16), wal_ref[...], preferred_element_type=F32) + bal_ref[...]
        la_ref[...] = (jnp.minimum(pre, 0.0) - jnp.log(1.0 + jnp.exp(-jnp.abs(pre)))) * (1.0 / GLA_TAU)

    p_ref[...] = jnp.dot(u_scr[...], w_ref[...], preferred_element_type=F32).astype(BF16)


def _inproj(x2, sh, sc, w_main, w_ab, w_alpha, b_alpha, seq, tm=1024, tn=1024):
    n, d = x2.shape
    tpb = seq // tm
    wq = w_alpha.shape[1]
    return pl.pallas_call(
        _inproj_kernel,
        out_shape=(jax.ShapeDtypeStruct((n, W_MAIN), BF16),
                   jax.ShapeDtypeStruct((n, wq), F32)),
        grid=(n // tm, W_MAIN // tn),
        in_specs=[pl.BlockSpec((tm, d), lambda i, j: (i, 0)),
                  pl.BlockSpec((1, 1, d), lambda i, j: (i // tpb, 0, 0)),
                  pl.BlockSpec((1, 1, d), lambda i, j: (i // tpb, 0, 0)),
                  pl.BlockSpec((d, tn), lambda i, j: (0, j)),
                  pl.BlockSpec((d, LANES), lambda i, j: (0, 0)),
                  pl.BlockSpec((LANES, wq), lambda i, j: (0, 0)),
                  pl.BlockSpec((1, wq), lambda i, j: (0, 0))],
        out_specs=(pl.BlockSpec((tm, tn), lambda i, j: (i, j)),
                   pl.BlockSpec((tm, wq), lambda i, j: (i, 0))),
        scratch_shapes=[pltpu.VMEM((tm, d), BF16)],
        compiler_params=_cparams(("arbitrary", "arbitrary")),
    )(x2, sh, sc, w_main, w_ab, w_alpha, b_alpha)


ONES_ROWS = 16


QUERY_GROUP = 256
POS_SPLIT = 32


def _attn_kernel(slopes_ref, q_ref, k_ref, v_ref, lq1_ref, lk1_ref, lq2_ref, lk2_ref, g_ref,
                 o_ref, kaug_scr, vt_scr, dmask_scr, sa_scr, sb_scr, acc_scr, m_scr, *, tk, lam_init):
    h = pl.program_id(1)
    qi = pl.program_id(2)
    slope = slopes_ref[h]
    dh = HEAD_DIM_DIFF
    dv = LANES
    n_chunks = vt_scr.shape[0]
    halves = q_ref.shape[0] // tk

    @pl.when(qi == 0)
    def _():
        for j in range(n_chunks):
            vt_scr[j, 0:dv, :] = v_ref[j * tk:(j + 1) * tk, :].astype(F32).T.astype(BF16)
            vt_scr[j, dv:dv + ONES_ROWS, :] = jnp.ones((ONES_ROWS, tk), BF16)
        kaug_scr[:, 0:LANES] = k_ref[...]
        koff = lax.broadcasted_iota(jnp.int32, (tk, LANES), 0)
        flane = lax.broadcasted_iota(jnp.int32, (tk, LANES), 1)
        feat = jnp.where(flane == 0, koff // POS_SPLIT, jnp.where(flane == 1, koff % POS_SPLIT, 0))
        feat = feat.astype(F32).astype(BF16)
        for j in range(n_chunks):
            kaug_scr[j * tk:(j + 1) * tk, LANES:2 * LANES] = feat
        kpos = lax.broadcasted_iota(jnp.int32, (tk, 2 * tk), 0)
        qpos = lax.broadcasted_iota(jnp.int32, (tk, 2 * tk), 1)
        qoff = jnp.where(qpos >= tk, qpos - tk, qpos)
        dmask_scr[...] = jnp.where(qoff >= kpos, 0.0, NEG_BIG)

    q = q_ref[...] * jnp.asarray(dh ** -0.5, BF16)
    lane = lax.broadcasted_iota(jnp.int32, q.shape, 1)
    zero = jnp.zeros_like(q)
    qf = jnp.where(lane == 0, slope * POS_SPLIT, jnp.where(lane == 1, slope, 0.0)).astype(BF16)
    q1 = jnp.concatenate([jnp.where(lane < dh, q, zero), qf], axis=1)
    q2 = jnp.concatenate([jnp.where(lane >= dh, q, zero), qf], axis=1)
    qq = jnp.concatenate([part[hq * tk:(hq + 1) * tk] for hq in range(halves) for part in (q1, q2)], axis=0)

    m_scr[...] = jnp.full(m_scr.shape, NEG_BIG, F32)
    acc_scr[...] = jnp.zeros(acc_scr.shape, F32)

    per_half = 2 * tk // QUERY_GROUP
    groups = [(hq, slice((hq * per_half + g) * QUERY_GROUP, (hq * per_half + g + 1) * QUERY_GROUP),
               slice(g * QUERY_GROUP, (g + 1) * QUERY_GROUP))
              for hq in range(halves) for g in range(per_half)]

    def score(kaug, ls):
        return lax.dot_general(kaug, qq[ls], (((1,), (1,)), ((), ())), preferred_element_type=F32)

    def step(j_cur, src, j_next, dst, first_half, masked_half, next_first_half=0):
        if dst is not None:
            kaug = kaug_scr[pl.ds(pl.multiple_of(j_next * tk, tk), tk), :]
        vt = vt_scr[j_cur]
        m_all = m_scr[...]
        acc_all = acc_scr[...]
        done = []
        for hq, ls, ms in groups:
            if dst is not None and hq >= next_first_half:
                dst[:, ls] = score(kaug, ls)
            if hq < first_half:
                continue
            cj = (-slope) * ((halves * qi + hq - j_cur) * tk).astype(F32)
            s = src[:, ls]
            if hq == masked_half:
                s = s + dmask_scr[:, ms]
            m_prev = m_all[:, ls]
            m_new = jnp.maximum(m_prev, jnp.max(s, axis=0, keepdims=True) + cj)
            p = jnp.exp(s - (m_new - cj))
            alpha = jnp.exp(m_prev - m_new)
            done.append((ls, m_new, alpha * acc_all[:, ls] + jnp.dot(vt, p.astype(BF16),
                                                                     preferred_element_type=F32)))
        for ls, m_new, acc_new in done:
            m_scr[:, ls] = m_new
            acc_scr[:, ls] = acc_new

    kaug0 = kaug_scr[0:tk, :]
    for _, ls, _ in groups:
        sa_scr[:, ls] = score(kaug0, ls)

    def pair(jj, carry):
        j = 2 * jj
        step(j, sa_scr, j + 1, sb_scr, 0, None)
        step(j + 1, sb_scr, j + 2, sa_scr, 0, None)
        return carry

    lax.fori_loop(0, qi, pair, 0)
    step(2 * qi, sa_scr, 2 * qi + 1, sb_scr, 0, 0, next_first_half=1)
    step(2 * qi + 1, sb_scr, None, None, 1, 1)

    lam = (jnp.exp(jnp.sum(lq1_ref[...] * lk1_ref[...], axis=-1, keepdims=True))
           - jnp.exp(jnp.sum(lq2_ref[...] * lk2_ref[...], axis=-1, keepdims=True)) + lam_init)
    acc = acc_scr[...]
    ot = acc[0:dv] / acc[dv:dv + 1]
    for hq in range(halves):
        base = hq * 2 * tk
        o = (ot[:, base:base + tk] - lam * ot[:, base + tk:base + 2 * tk]).T
        o = o * lax.rsqrt(jnp.mean(o * o, axis=-1, keepdims=True) + LN_EPS) * g_ref[...] * (1.0 - lam_init)
        o_ref[hq * tk:(hq + 1) * tk, :] = o.astype(o_ref.dtype)


def _attn(p3, lq1, lk1, lq2, lk2, g, lam_init, tk=512):
    b, s, _ = p3.shape
    h = N_HEADS_DIFF
    tq = 2 * tk
    slopes = jnp.asarray(2.0 ** (-8.0 * np.arange(1, h + 1) / h), dtype=F32)
    vec = pl.BlockSpec((1, HEAD_DIM_DIFF), lambda bi, hi, qi, sl: (0, 0))
    return pl.pallas_call(
        functools.partial(_attn_kernel, tk=tk, lam_init=lam_init),
        out_shape=jax.ShapeDtypeStruct((b, s, h * LANES), BF16),
        grid_spec=pltpu.PrefetchScalarGridSpec(
            num_scalar_prefetch=1,
            grid=(b, h, s // tq),
            in_specs=[pl.BlockSpec((None, tq, LANES), lambda bi, hi, qi, sl: (bi, qi, OFF_QA // LANES + hi)),
                      pl.BlockSpec((None, s, LANES), lambda bi, hi, qi, sl: (bi, 0, OFF_KA // LANES + hi)),
                      pl.BlockSpec((None, s, LANES), lambda bi, hi, qi, sl: (bi, 0, OFF_VA // LANES + hi)),
                      vec, vec, vec, vec,
                      pl.BlockSpec((1, LANES), lambda bi, hi, qi, sl: (0, 0))],
            out_specs=pl.BlockSpec((None, tq, LANES), lambda bi, hi, qi, sl: (bi, qi, hi)),
            scratch_shapes=[pltpu.VMEM((s, 2 * LANES), BF16),
                            pltpu.VMEM((s // tk, LANES + ONES_ROWS, tk), BF16),
                            pltpu.VMEM((tk, 2 * tk), F32),
                            pltpu.VMEM((tk, 2 * tq), F32),
                            pltpu.VMEM((tk, 2 * tq), F32),
                            pltpu.VMEM((LANES + ONES_ROWS, 2 * tq), F32),
                            pltpu.VMEM((1, 2 * tq), F32)]),
        compiler_params=_cparams(("arbitrary", "arbitrary", "arbitrary")),
    )(slopes, p3, p3, p3, lq1, lk1, lq2, lk2, g)


GLA_HEADS_PER_STEP = 2


def _gla_kernel(q_ref, k_ref, v_ref, gb_ref, la_ref, g_ref, o_ref, state_scr, *, tt):
    c = GLA_CHUNK
    dk, dv = KEY_DIM_GLA, VAL_DIM_GLA
    hp = state_scr.shape[0]

    @pl.when(pl.program_id(2) == 0)
    def _():
        state_scr[...] = jnp.zeros(state_scr.shape, F32)

    rr = lax.broadcasted_iota(jnp.int32, (c, c), 0)
    cr = lax.broadcasted_iota(jnp.int32, (c, c), 1)
    causal = cr <= rr
    tri = jnp.where(causal, 1.0, 0.0).astype(BF16)
    qscale = dk ** -0.5

    sts = [state_scr[hh] for hh in range(hp)]
    for ci in range(tt // c):
        sl = slice(ci * c, (ci + 1) * c)
        for hh in range(hp):
            ks = slice(hh * dk, (hh + 1) * dk)
            vs = slice(hh * dv, (hh + 1) * dv)
            st = sts[hh]
            la = la_ref[sl, ks]
            la_hi = la.astype(BF16)
            la_lo = (la - la_hi.astype(F32)).astype(BF16)
            cum2 = jnp.dot(tri, jnp.concatenate([la_hi, la_lo], axis=1), preferred_element_type=F32)
            cum_c = cum2[:, :dk] + cum2[:, dk:]
            last = cum_c[c - 1:c]
            mid = cum_c[c // 2:c // 2 + 1]
            q_c = q_ref[sl, ks].astype(F32) * qscale
            k_c = k_ref[sl, ks].astype(F32)
            v_c = v_ref[sl, vs]
            qe = (q_c * jnp.exp(cum_c)).astype(BF16)
            o_inter = lax.dot_general(qe, st.astype(BF16), (((1,), (1,)), ((), ())),
                                      preferred_element_type=F32)
            q2 = (q_c * jnp.exp(cum_c - mid)).astype(BF16)
            k2 = (k_c * jnp.exp(mid - cum_c)).astype(BF16)
            att = lax.dot_general(q2, k2, (((1,), (1,)), ((), ())), preferred_element_type=F32)
            att = jnp.where(causal, att, 0.0).astype(BF16)
            o = o_inter + jnp.dot(att, v_c, preferred_element_type=F32)
            kd = (k_c * jnp.exp(last - cum_c)).astype(BF16)
            v_t = v_c.astype(F32).T.astype(BF16)
            sts[hh] = st * jnp.exp(last) + jnp.dot(v_t, kd, preferred_element_type=F32)

            o = o * lax.rsqrt(jnp.mean(o * o, axis=-1, keepdims=True) + LN_EPS) * g_ref[...]
            gate = gb_ref[sl, vs].astype(F32)
            o_ref[sl, vs] = (o * (gate * _sigmoid(gate))).astype(o_ref.dtype)
    for hh in range(hp):
        state_scr[hh] = sts[hh]


def _gla(p3, la3, g, tt=512):
    b, s, _ = p3.shape
    hp = GLA_HEADS_PER_STEP
    dk, dv = KEY_DIM_GLA * hp, VAL_DIM_GLA * hp
    return pl.pallas_call(
        functools.partial(_gla_kernel, tt=tt),
        out_shape=jax.ShapeDtypeStruct((b, s, N_HEADS_GLA * VAL_DIM_GLA), BF16),
        grid=(b, N_HEADS_GLA // hp, s // tt),
        in_specs=[pl.BlockSpec((None, tt, dk), lambda bi, hi, ti: (bi, ti, OFF_QB // dk + hi)),
                  pl.BlockSpec((None, tt, dk), lambda bi, hi, ti: (bi, ti, OFF_KB // dk + hi)),
                  pl.BlockSpec((None, tt, dv), lambda bi, hi, ti: (bi, ti, OFF_VB // dv + hi)),
                  pl.BlockSpec((None, tt, dv), lambda bi, hi, ti: (bi, ti, OFF_GB // dv + hi)),
                  pl.BlockSpec((None, tt, dk), lambda bi, hi, ti: (bi, ti, hi)),
                  pl.BlockSpec((1, VAL_DIM_GLA), lambda bi, hi, ti: (0, 0))],
        out_specs=pl.BlockSpec((None, tt, dv), lambda bi, hi, ti: (bi, ti, hi)),
        scratch_shapes=[pltpu.VMEM((hp, VAL_DIM_GLA, KEY_DIM_GLA), F32)],
        compiler_params=_cparams(("arbitrary", "arbitrary", "arbitrary")),
    )(p3, p3, p3, p3, la3, g)


def _mix_kernel(oa_ref, ob_ref, gt_ref, x_ref, bg_ref, g1_ref, sh2_ref, sc2_ref, lng_ref, lnb_ref,
                wba_ref, wbb_ref, wo_ref, wrh_ref, wrl_ref, br_ref,
                x1_ref, u2_ref, lg_ref, *, alpha):
    d = x_ref.shape[-1]
    a = jnp.dot(oa_ref[...], wba_ref[...], preferred_element_type=F32)
    bm = jnp.dot(ob_ref[...], wbb_ref[...], preferred_element_type=F32)
    gates = _sigmoid(gt_ref[...].astype(F32) + bg_ref[...])
    mixed = gates[:, :d] * a + gates[:, d:] * bm
    y = jnp.dot(mixed.astype(BF16), wo_ref[...], preferred_element_type=F32)
    x1 = _standardize(alpha * x_ref[...] + g1_ref[0] * y) * lng_ref[...] + lnb_ref[...]
    x1_ref[...] = x1
    u2 = _standardize(x1) * (1.0 + sc2_ref[0]) + sh2_ref[0]
    _store_token_tiles(u2_ref, u2)
    uh = u2.astype(BF16)
    ul = (u2 - uh.astype(F32)).astype(BF16)
    lg_ref[...] = (jnp.dot(uh, wrh_ref[...], preferred_element_type=F32)
                   + jnp.dot(uh, wrl_ref[...], preferred_element_type=F32)
                   + jnp.dot(ul, wrh_ref[...], preferred_element_type=F32) + br_ref[...])


def _mix(oa, ob, p, x2, bg, g1, sh2, sc2, lng, lnb, wba, wbb, wo, wrh, wrl, br, seq, alpha, tm=512):
    n, d = x2.shape
    tpb = seq // tm
    row = lambda i: (i, 0)
    const = lambda i: (0, 0)
    per_b = pl.BlockSpec((1, 1, d), lambda i: (i // tpb, 0, 0))
    return pl.pallas_call(
        functools.partial(_mix_kernel, alpha=alpha),
        out_shape=(jax.ShapeDtypeStruct((n, d), F32),
                   jax.ShapeDtypeStruct((n * (d // LANES), LANES), F32),
                   jax.ShapeDtypeStruct((n, LANES), F32)),
        grid=(n // tm,),
        in_specs=[pl.BlockSpec((tm, d), row), pl.BlockSpec((tm, d), row),
                  pl.BlockSpec((tm, 2 * d), lambda i: (i, OFF_GATES // (2 * d))),
                  pl.BlockSpec((tm, d), row),
                  pl.BlockSpec((1, 2 * d), const), per_b, per_b, per_b,
                  pl.BlockSpec((1, d), const), pl.BlockSpec((1, d), const),
                  pl.BlockSpec((d, d), const), pl.BlockSpec((d, d), const), pl.BlockSpec((d, d), const),
                  pl.BlockSpec((d, LANES), const), pl.BlockSpec((d, LANES), const),
                  pl.BlockSpec((1, LANES), const)],
        out_specs=(pl.BlockSpec((tm, d), row), pl.BlockSpec((tm * (d // LANES), LANES), row),
                   pl.BlockSpec((tm, LANES), row)),
        compiler_params=_cparams(("arbitrary",)),
    )(oa, ob, p, x2, bg, g1, sh2, sc2, lng, lnb, wba, wbb, wo, wrh, wrl, br)


def _route_kernel(lg_ref, ri_ref, rw_ref, cnt_ref, *, tm):
    @pl.when(pl.program_id(0) == 0)
    def _():
        cnt_ref[...] = jnp.zeros(cnt_ref.shape, F32)

    lg = lg_ref[...]
    lane = lax.broadcasted_iota(jnp.int32, lg.shape, 1)
    lanef = lane.astype(F32)
    far = float(LANES)
    is_g = lane < N_GROUPS
    gl = jnp.where(is_g, lg, NEG_BIG)
    gmax = jnp.max(gl, axis=-1, keepdims=True)
    gidx = jnp.min(jnp.where(gl == gmax, lanef, far), axis=-1, keepdims=True)
    gw = 1.0 / jnp.sum(jnp.where(is_g, jnp.exp(gl - gmax), 0.0), axis=-1, keepdims=True)
    lo = N_GROUPS + gidx * EXPERTS_PER_GROUP
    in_g = (lanef >= lo) & (lanef < lo + EXPERTS_PER_GROUP)
    el = jnp.where(in_g, lg, NEG_BIG)
    v1 = jnp.max(el, axis=-1, keepdims=True)
    i1 = jnp.min(jnp.where(in_g & (el == v1), lanef, far), axis=-1, keepdims=True)
    in_g2 = in_g & (lanef != i1)
    el2 = jnp.where(in_g2, lg, NEG_BIG)
    v2 = jnp.max(el2, axis=-1, keepdims=True)
    i2 = jnp.min(jnp.where(in_g2 & (el2 == v2), lanef, far), axis=-1, keepdims=True)
    t = jnp.exp(v2 - v1)
    w1 = gw / (1.0 + t)
    w2 = gw * t / (1.0 + t)

    oh1 = lanef == i1
    oh2 = lanef == i2
    oh = jnp.where(oh1 | oh2, 1.0, 0.0)
    r = lax.broadcasted_iota(jnp.int32, (tm, tm), 0)
    c = lax.broadcasted_iota(jnp.int32, (tm, tm), 1)
    lower = jnp.where(c < r, 1.0, 0.0).astype(BF16)
    base = jnp.dot(lower, oh.astype(BF16), preferred_element_type=F32) + cnt_ref[0:1, :]
    r1 = jnp.sum(jnp.where(oh1, base, 0.0), axis=-1, keepdims=True)
    r2 = jnp.sum(jnp.where(oh2, base, 0.0), axis=-1, keepdims=True)
    cnt_ref[...] = cnt_ref[...] + jnp.sum(oh, axis=0, keepdims=True)

    e1 = i1 - float(N_GROUPS)
    e2 = i2 - float(N_GROUPS)
    ri = jnp.where(lane == 0, e1, jnp.where(lane == 1, e2, jnp.where(lane == 2, r1, jnp.where(lane == 3, r2, 0.0))))
    ri_ref[...] = ri.astype(jnp.int32)
    rw_ref[...] = jnp.where(lane == 0, w1, jnp.where(lane == 1, w2, 0.0))


def _route(lg, tm=512):
    n = lg.shape[0]
    row = lambda i: (i, 0)
    return pl.pallas_call(
        functools.partial(_route_kernel, tm=tm),
        out_shape=(jax.ShapeDtypeStruct((n, LANES), jnp.int32),
                   jax.ShapeDtypeStruct((n, LANES), F32),
                   jax.ShapeDtypeStruct((8, LANES), F32)),
        grid=(n // tm,),
        in_specs=[pl.BlockSpec((tm, LANES), row)],
        out_specs=(pl.BlockSpec((tm, LANES), row), pl.BlockSpec((tm, LANES), row),
                   pl.BlockSpec((8, LANES), lambda i: (0, 0))),
        compiler_params=_cparams(("arbitrary",)),
    )(lg)


MOE_ROWS = 256


def _expert_kernel(be_ref, nu_ref, tok_a_ref, tok_b_ref, dst_ref, u_ref, wg_ref, wu_ref, wd_ref, y_ref,
                   xb0, xb1, yb0, yb1, wg_scr, wu_scr, wd_scr, gsem, ssem):
    i = pl.program_id(0)
    n_used = nu_ref[0]
    blk = tok_a_ref.shape[-1]
    xbs, ybs = (xb0, xb1), (yb0, yb1)

    def tile(ref, t):
        return ref.at[pl.ds(pl.multiple_of(t * ROW_TILES, ROW_TILES), ROW_TILES)]

    def gather_issue(tok_ref, xb, sem):
        for r in range(blk):
            pltpu.make_async_copy(tile(u_ref, tok_ref[0, 0, r]), tile(xb, r), sem).start()

    def gather_wait(xb, sem):
        pltpu.make_async_copy(u_ref.at[pl.ds(0, blk * ROW_TILES)], xb, sem).wait()

    def scatter_issue(yb, sem):
        for r in range(blk):
            pltpu.make_async_copy(tile(yb, r), tile(y_ref, dst_ref[0, 0, r]), sem).start()

    def scatter_wait(yb, sem):
        pltpu.make_async_copy(yb, y_ref.at[pl.ds(0, blk * ROW_TILES)], sem).wait()

    def load_weights():
        @pl.when((i == 0) | (be_ref[i] != be_ref[jnp.maximum(i - 1, 0)]))
        def _():
            wg_scr[...] = wg_ref[0].astype(BF16)
            wu_scr[...] = wu_ref[0].astype(BF16)
            wd_scr[...] = wd_ref[0].astype(BF16)

    def compute(xb, yb):
        x = _load_token_tiles(xb).astype(BF16)
        g = jnp.dot(x, wg_scr[...], preferred_element_type=F32)
        u = jnp.dot(x, wu_scr[...], preferred_element_type=F32)
        hid = (g * _sigmoid(g) * u).astype(BF16)
        _store_token_tiles(yb, jnp.dot(hid, wd_scr[...], preferred_element_type=F32))

    @pl.when(i == 0)
    def _():
        gather_issue(tok_a_ref, xb0, gsem.at[0])
        gather_wait(xb0, gsem.at[0])
        load_weights()
        gather_issue(tok_b_ref, xb1, gsem.at[1])
        compute(xb0, yb0)

    for p in range(2):
        cur, oth = p, 1 - p

        @pl.when((i >= 1) & (i < n_used) & (i % 2 == p))
        def _():
            gather_wait(xbs[cur], gsem.at[cur])

            @pl.when(i >= 2)
            def _():
                scatter_wait(ybs[cur], ssem.at[cur])

            load_weights()
            gather_issue(tok_b_ref, xbs[oth], gsem.at[oth])
            scatter_issue(ybs[oth], ssem.at[oth])
            compute(xbs[cur], ybs[cur])

        @pl.when((i == n_used) & (i % 2 == p))
        def _():
            gather_wait(xbs[cur], gsem.at[cur])

            @pl.when(i >= 2)
            def _():
                scatter_wait(ybs[cur], ssem.at[cur])

            scatter_issue(ybs[oth], ssem.at[oth])

        @pl.when((i == n_used + 1) & (i % 2 == p))
        def _():
            scatter_wait(ybs[cur], ssem.at[cur])


def _experts(blk_e, n_used, src_tok3, dst_row3, u2t, wg, wu, wd, layer, n_rows):
    d, de = wg.shape[-2:]
    blk = MOE_ROWS
    rt = d // LANES
    n_steps = blk_e.shape[0]
    wspec = lambda r, c: pl.BlockSpec((None, 1, r, c), lambda i, be, nu: (layer, be[i], 0, 0))
    ispec = lambda f: pl.BlockSpec((1, 1, blk), lambda i, be, nu: (f(i, nu), 0, 0), memory_space=pltpu.SMEM)
    return pl.pallas_call(
        _expert_kernel,
        out_shape=jax.ShapeDtypeStruct((n_rows * rt, LANES), F32),
        grid_spec=pltpu.PrefetchScalarGridSpec(
            num_scalar_prefetch=2,
            grid=(n_steps,),
            in_specs=[ispec(lambda i, nu: jnp.minimum(i, nu[0] - 1)),
                      ispec(lambda i, nu: jnp.minimum(i + 1, nu[0] - 1)),
                      ispec(lambda i, nu: jnp.clip(i - 1, 0, nu[0] - 1)),
                      pl.BlockSpec(memory_space=pl.ANY),
                      wspec(d, de), wspec(d, de), wspec(de, d)],
            out_specs=pl.BlockSpec(memory_space=pl.ANY),
            scratch_shapes=[pltpu.VMEM((blk * rt, LANES), F32), pltpu.VMEM((blk * rt, LANES), F32),
                            pltpu.VMEM((blk * rt, LANES), F32), pltpu.VMEM((blk * rt, LANES), F32),
                            pltpu.VMEM((d, de), BF16), pltpu.VMEM((d, de), BF16),
                            pltpu.VMEM((de, d), BF16),
                            pltpu.SemaphoreType.DMA((2,)), pltpu.SemaphoreType.DMA((2,))]),
        compiler_params=_cparams(("arbitrary",), has_side_effects=True),
    )(blk_e, n_used, src_tok3, src_tok3, dst_row3, u2t, wg, wu, wd)


def _combine_kernel(x1_ref, y1_ref, y2_ref, rw_ref, g2_ref, lng_ref, lnb_ref, o_ref, *, alpha):
    rw = rw_ref[...]
    y = rw[:, 0:1] * _load_token_tiles(y1_ref) + rw[:, 1:2] * _load_token_tiles(y2_ref)
    o_ref[...] = _standardize(alpha * x1_ref[...] + g2_ref[0] * y) * lng_ref[...] + lnb_ref[...]


def _combine(x1, y_tok, rw, g2, lng, lnb, seq, alpha, tm=512):
    n, d = x1.shape
    tpb = seq // tm
    const = lambda i: (0, 0)
    return pl.pallas_call(
        functools.partial(_combine_kernel, alpha=alpha),
        out_shape=jax.ShapeDtypeStruct((n, d), F32),
        grid=(n // tm,),
        in_specs=[pl.BlockSpec((tm, d), lambda i: (i, 0)),
                  pl.BlockSpec((tm * (d // LANES), LANES), lambda i: (i, 0)),
                  pl.BlockSpec((tm * (d // LANES), LANES), lambda i: (n // tm + i, 0)),
                  pl.BlockSpec((tm, LANES), lambda i: (i, 0)),
                  pl.BlockSpec((1, 1, d), lambda i: (i // tpb, 0, 0)),
                  pl.BlockSpec((1, d), const), pl.BlockSpec((1, d), const)],
        out_specs=pl.BlockSpec((tm, d), lambda i: (i, 0)),
        compiler_params=_cparams(("arbitrary",)),
    )(x1, y_tok, y_tok, rw, g2, lng, lnb)


def _moe_plan(ri, cnt, n):
    blk = MOE_ROWS
    n_rows = 2 * n + N_EXPERTS * blk
    n_blocks = n_rows // blk
    n_steps = n_blocks + 2
    counts = cnt[0, N_GROUPS:N_GROUPS + N_EXPERTS].astype(jnp.int32)
    padded = ((counts + blk - 1) // blk) * blk
    pends = jnp.cumsum(padded)
    pstarts = pends - padded
    cstarts = jnp.cumsum(counts) - counts
    pos = jnp.take(pstarts, ri[:, 0:2], axis=0) + ri[:, 2:4]
    blk_start = jnp.arange(n_steps, dtype=jnp.int32) * blk
    blk_e = jnp.minimum(jnp.sum(blk_start[:, None] >= pends[None, :], axis=1), N_EXPERTS - 1).astype(jnp.int32)
    n_used = (pends[-1:] // blk).astype(jnp.int32)
    prow = jnp.arange(n_rows, dtype=jnp.int32)
    pe = jnp.repeat(blk_e[:n_blocks], blk)
    spare = 2 * n + prow - (cstarts[pe] + jnp.minimum(prow - pstarts[pe], counts[pe]))
    dst_row = spare.at[pos.T.reshape(-1)].set(jnp.arange(2 * n, dtype=jnp.int32), unique_indices=True)
    src_tok = dst_row % n
    return blk_e, n_used, src_tok.reshape(n_blocks, 1, blk), dst_row.reshape(n_blocks, 1, blk)


def kernel(x, c, w_ada, b_ada, w_in, b_gates, w_alpha, b_alpha, lambda_q1, lambda_k1, lambda_q2, lambda_k2, diff_norm_g, gla_norm_g, w_branch_a, w_branch_b, w_out, ln1_g, ln1_b, w_router_g, b_router_g, w_router_e, b_router_e, w_gate_e, w_up_e, w_down_e, ln2_g, ln2_b):
    b, s, d = x.shape
    depth = w_ada.shape[0]
    n = b * s
    alpha = (2.0 * depth) ** 0.25

    ada = _ada(c, w_ada, b_ada)
    x2 = x.reshape(n, d)
    for l in range(depth):
        sh1, sc1, g1, sh2, sc2, g2 = [ada[l, :, i * d:(i + 1) * d].reshape(b, 1, d) for i in range(6)]
        wl = w_in[l]
        w_main = jnp.concatenate([wl[:, :6144], wl[:, 6144 + GLA_RANK:]], axis=1).astype(BF16)
        w_ab = jnp.pad(wl[:, 6144:6144 + GLA_RANK], ((0, 0), (0, LANES - GLA_RANK))).astype(BF16)
        wal = jnp.pad(w_alpha[l], ((0, LANES - GLA_RANK), (0, 0))).astype(BF16)
        p, la = _inproj(x2, sh1, sc1, w_main, w_ab, wal, b_alpha[l].reshape(1, -1), s)
        p3 = p.reshape(b, s, W_MAIN)

        lam_init = 0.8 - 0.6 * math.exp(-0.3 * l)
        oa = _attn(p3, lambda_q1[l].reshape(1, -1), lambda_k1[l].reshape(1, -1),
                   lambda_q2[l].reshape(1, -1), lambda_k2[l].reshape(1, -1),
                   diff_norm_g[l].reshape(1, -1), lam_init)
        ob = _gla(p3, la.reshape(b, s, -1), gla_norm_g[l].reshape(1, -1))

        wr = jnp.pad(jnp.concatenate([w_router_g[l], w_router_e[l]], axis=1),
                     ((0, 0), (0, LANES - N_GROUPS - N_EXPERTS)))
        wrh = wr.astype(BF16)
        wrl = (wr - wrh.astype(F32)).astype(BF16)
        br = jnp.pad(jnp.concatenate([b_router_g[l], b_router_e[l]]),
                     (0, LANES - N_GROUPS - N_EXPERTS)).reshape(1, LANES)
        x1, u2, lg = _mix(oa.reshape(n, d), ob.reshape(n, d), p, x2, b_gates[l].reshape(1, -1),
                          g1, sh2, sc2, ln1_g[l].reshape(1, d), ln1_b[l].reshape(1, d),
                          w_branch_a[l].astype(BF16), w_branch_b[l].astype(BF16), w_out[l].astype(BF16),
                          wrh, wrl, br, s, alpha)

        ri, rw, cnt = _route(lg)
        blk_e, n_used, src_tok3, dst_row3 = _moe_plan(ri, cnt, n)
        y_tok = _experts(blk_e, n_used, src_tok3, dst_row3, u2, w_gate_e, w_up_e, w_down_e, l,
                         2 * n + N_EXPERTS * MOE_ROWS)
        x2 = _combine(x1, y_tok, rw, g2, ln2_g[l].reshape(1, d), ln2_b[l].reshape(1, d), s, alpha)
    return x2.reshape(b, s, d)
```

```python
import functools
import math

import jax
import jax.numpy as jnp
import numpy as np
from jax import lax
from jax.experimental import pallas as pl
from jax.experimental.pallas import tpu as pltpu

F32 = jnp.float32
BF16 = jnp.bfloat16

N_HEADS_DIFF = 8
HEAD_DIM_DIFF = 64
N_HEADS_GLA = 4
KEY_DIM_GLA = 128
VAL_DIM_GLA = 256
GLA_RANK = 16
GLA_TAU = 16.0
GLA_CHUNK = 64
N_GROUPS = 4
EXPERTS_PER_GROUP = 8
N_EXPERTS = N_GROUPS * EXPERTS_PER_GROUP
LN_EPS = 1e-5
LANES = 128
NEG_BIG = -1e30

OFF_QA, OFF_KA, OFF_VA = 0, 1024, 2048
OFF_QB, OFF_KB, OFF_VB, OFF_GB, OFF_GATES = 3072, 3584, 4096, 5120, 6144
W_MAIN = 8192

VMEM_LIMIT = 56 * 1024 * 1024


def _cparams(sem, **kw):
    return pltpu.CompilerParams(dimension_semantics=sem, vmem_limit_bytes=VMEM_LIMIT, **kw)


def _standardize(x):
    mu = jnp.mean(x, axis=-1, keepdims=True)
    xc = x - mu
    var = jnp.mean(xc * xc, axis=-1, keepdims=True)
    return xc * lax.rsqrt(var + LN_EPS)


def _sigmoid(x):
    return 1.0 / (1.0 + jnp.exp(-x))


ROW_TILES = 8


def _store_token_tiles(ref, x):
    t = x.shape[0]
    for s in range(ROW_TILES):
        ref[pl.ds(s, t, stride=ROW_TILES), :] = x[:, s * LANES:(s + 1) * LANES]


def _load_token_tiles(ref):
    t = ref.shape[0] // ROW_TILES
    return jnp.concatenate([ref[pl.ds(s, t, stride=ROW_TILES), :] for s in range(ROW_TILES)], axis=1)


def _ada_kernel(c_ref, w_ref, b_ref, o_ref):
    c = c_ref[...]
    cond = c * _sigmoid(c)
    o_ref[0] = jnp.dot(cond, w_ref[0], preferred_element_type=F32,
                       precision=lax.Precision.HIGHEST) + b_ref[0]


def _ada(c, w_ada, b_ada):
    depth, d, d6 = w_ada.shape
    b = c.shape[0]
    return pl.pallas_call(
        _ada_kernel,
        out_shape=jax.ShapeDtypeStruct((depth, b, d6), F32),
        grid=(depth, d6 // d),
        in_specs=[pl.BlockSpec((b, d), lambda l, j: (0, 0)),
                  pl.BlockSpec((1, d, d), lambda l, j: (l, 0, j)),
                  pl.BlockSpec((1, 1, d), lambda l, j: (l, 0, j))],
        out_specs=pl.BlockSpec((1, b, d), lambda l, j: (l, 0, j)),
        compiler_params=_cparams(("arbitrary", "arbitrary")),
    )(c, w_ada, b_ada.reshape(depth, 1, d6))


def _inproj_kernel(x_ref, sh_ref, sc_ref, w_ref, wab_ref, wal_ref, bal_ref,
                   p_ref, la_ref, u_scr):
    @pl.when(pl.program_id(1) == 0)
    def _():
        u = _standardize(x_ref[...]) * (1.0 + sc_ref[0]) + sh_ref[0]
        ub = u.astype(BF16)
        u_scr[...] = ub
        ab = jnp.dot(ub, wab_ref[...], preferred_element_type=F32)
        pre = jnp.dot(ab.astype(BF16), wal_ref[...], preferred_element_type=F32) + bal_ref[...]
        la_ref[...] = (jnp.minimum(pre, 0.0) - jnp.log(1.0 + jnp.exp(-jnp.abs(pre)))) * (1.0 / GLA_TAU)

    p_ref[...] = jnp.dot(u_scr[...], w_ref[...], preferred_element_type=F32).astype(BF16)


def _inproj(x2, sh, sc, w_main, w_ab, w_alpha, b_alpha, seq, tm=1024, tn=1024):
    n, d = x2.shape
    tpb = seq // tm
    wq = w_alpha.shape[1]
    return pl.pallas_call(
        _inproj_kernel,
        out_shape=(jax.ShapeDtypeStruct((n, W_MAIN), BF16),
                   jax.ShapeDtypeStruct((n, wq), F32)),
        grid=(n // tm, W_MAIN // tn),
        in_specs=[pl.BlockSpec((tm, d), lambda i, j: (i, 0)),
                  pl.BlockSpec((1, 1, d), lambda i, j: (i // tpb, 0, 0)),
                  pl.BlockSpec((1, 1, d), lambda i, j: (i // tpb, 0, 0)),
                  pl.BlockSpec((d, tn), lambda i, j: (0, j)),
                  pl.BlockSpec((d, LANES), lambda i, j: (0, 0)),
                  pl.BlockSpec((LANES, wq), lambda i, j: (0, 0)),
                  pl.BlockSpec((1, wq), lambda i, j: (0, 0))],
        out_specs=(pl.BlockSpec((tm, tn), lambda i, j: (i, j)),
                   pl.BlockSpec((tm, wq), lambda i, j: (i, 0))),
        scratch_shapes=[pltpu.VMEM((tm, d), BF16)],
        compiler_params=_cparams(("arbitrary", "arbitrary")),
    )(x2, sh, sc, w_main, w_ab, w_alpha, b_alpha)


ONES_ROWS = 16


QUERY_GROUP = 256
POS_SPLIT = 32


def _attn_kernel(slopes_ref, q_ref, k_ref, v_ref, lq1_ref, lk1_ref, lq2_ref, lk2_ref, g_ref,
                 o_ref, kaug_scr, vt_scr, dmask_scr, sa_scr, sb_scr, acc_scr, m_scr, *, tk, lam_init):
    h = pl.program_id(1)
    qi = pl.program_id(2)
    slope = slopes_ref[h]
    dh = HEAD_DIM_DIFF
    dv = LANES
    n_chunks = vt_scr.shape[0]
    halves = q_ref.shape[0] // tk

    @pl.when(qi == 0)
    def _():
        for j in range(n_chunks):
            vt_scr[j, 0:dv, :] = v_ref[j * tk:(j + 1) * tk, :].astype(F32).T.astype(BF16)
            vt_scr[j, dv:dv + ONES_ROWS, :] = jnp.ones((ONES_ROWS, tk), BF16)
        kaug_scr[:, 0:LANES] = k_ref[...]
        koff = lax.broadcasted_iota(jnp.int32, (tk, LANES), 0)
        flane = lax.broadcasted_iota(jnp.int32, (tk, LANES), 1)
        feat = jnp.where(flane == 0, koff // POS_SPLIT, jnp.where(flane == 1, koff % POS_SPLIT, 0))
        feat = feat.astype(F32).astype(BF16)
        for j in range(n_chunks):
            kaug_scr[j * tk:(j + 1) * tk, LANES:2 * LANES] = feat
        kpos = lax.broadcasted_iota(jnp.int32, (tk, 2 * tk), 0)
        qpos = lax.broadcasted_iota(jnp.int32, (tk, 2 * tk), 1)
        qoff = jnp.where(qpos >= tk, qpos - tk, qpos)
        dmask_scr[...] = jnp.where(qoff >= kpos, 0.0, NEG_BIG)

    q = q_ref[...] * jnp.asarray(dh ** -0.5, BF16)
    lane = lax.broadcasted_iota(jnp.int32, q.shape, 1)
    zero = jnp.zeros_like(q)
    qf = jnp.where(lane == 0, slope * POS_SPLIT, jnp.where(lane == 1, slope, 0.0)).astype(BF16)
    q1 = jnp.concatenate([jnp.where(lane < dh, q, zero), qf], axis=1)
    q2 = jnp.concatenate([jnp.where(lane >= dh, q, zero), qf], axis=1)
    qq = jnp.concatenate([part[hq * tk:(hq + 1) * tk] for hq in range(halves) for part in (q1, q2)], axis=0)

    m_scr[...] = jnp.full(m_scr.shape, NEG_BIG, F32)
    acc_scr[...] = jnp.zeros(acc_scr.shape, F32)

    per_half = 2 * tk // QUERY_GROUP
    groups = [(hq, slice((hq * per_half + g) * QUERY_GROUP, (hq * per_half + g + 1) * QUERY_GROUP),
               slice(g * QUERY_GROUP, (g + 1) * QUERY_GROUP))
              for hq in range(halves) for g in range(per_half)]

    def score(kaug, ls):
        return lax.dot_general(kaug, qq[ls], (((1,), (1,)), ((), ())), preferred_element_type=F32)

    def step(j_cur, src, j_next, dst, first_half, masked_half, next_first_half=0):
        if dst is not None:
            kaug = kaug_scr[pl.ds(pl.multiple_of(j_next * tk, tk), tk), :]
        vt = vt_scr[j_cur]
        m_all = m_scr[...]
        acc_all = acc_scr[...]
        done = []
        for hq, ls, ms in groups:
            if dst is not None and hq >= next_first_half:
                dst[:, ls] = score(kaug, ls)
            if hq < first_half:
                continue
            cj = (-slope) * ((halves * qi + hq - j_cur) * tk).astype(F32)
            s = src[:, ls]
            if hq == masked_half:
                s = s + dmask_scr[:, ms]
            m_prev = m_all[:, ls]
            m_new = jnp.maximum(m_prev, jnp.max(s, axis=0, keepdims=True) + cj)
            p = jnp.exp(s - (m_new - cj))
            alpha = jnp.exp(m_prev - m_new)
            done.append((ls, m_new, alpha * acc_all[:, ls] + jnp.dot(vt, p.astype(BF16),
                                                                     preferred_element_type=F32)))
        for ls, m_new, acc_new in done:
            m_scr[:, ls] = m_new
            acc_scr[:, ls] = acc_new

    kaug0 = kaug_scr[0:tk, :]
    for _, ls, _ in groups:
        sa_scr[:, ls] = score(kaug0, ls)

    def pair(jj, carry):
        j = 2 * jj
        step(j, sa_scr, j + 1, sb_scr, 0, None)
        step(j + 1, sb_scr, j + 2, sa_scr, 0, None)
        return carry

    lax.fori_loop(0, qi, pair, 0)
    step(2 * qi, sa_scr, 2 * qi + 1, sb_scr, 0, 0, next_first_half=1)
    step(2 * qi + 1, sb_scr, None, None, 1, 1)

    lam = (jnp.exp(jnp.sum(lq1_ref[...] * lk1_ref[...], axis=-1, keepdims=True))
           - jnp.exp(jnp.sum(lq2_ref[...] * lk2_ref[...], axis=-1, keepdims=True)) + lam_init)
    acc = acc_scr[...]
    ot = acc[0:dv] / acc[dv:dv + 1]
    for hq in range(halves):
        base = hq * 2 * tk
        o = (ot[:, base:base + tk] - lam * ot[:, base + tk:base + 2 * tk]).T
        o = o * lax.rsqrt(jnp.mean(o * o, axis=-1, keepdims=True) + LN_EPS) * g_ref[...] * (1.0 - lam_init)
        o_ref[hq * tk:(hq + 1) * tk, :] = o.astype(o_ref.dtype)


def _attn(p3, lq1, lk1, lq2, lk2, g, lam_init, tk=512):
    b, s, _ = p3.shape
    h = N_HEADS_DIFF
    tq = 2 * tk
    slopes = jnp.asarray(2.0 ** (-8.0 * np.arange(1, h + 1) / h), dtype=F32)
    vec = pl.BlockSpec((1, HEAD_DIM_DIFF), lambda bi, hi, qi, sl: (0, 0))
    return pl.pallas_call(
        functools.partial(_attn_kernel, tk=tk, lam_init=lam_init),
        out_shape=jax.ShapeDtypeStruct((b, s, h * LANES), BF16),
        grid_spec=pltpu.PrefetchScalarGridSpec(
            num_scalar_prefetch=1,
            grid=(b, h, s // tq),
            in_specs=[pl.BlockSpec((None, tq, LANES), lambda bi, hi, qi, sl: (bi, qi, OFF_QA // LANES + hi)),
                      pl.BlockSpec((None, s, LANES), lambda bi, hi, qi, sl: (bi, 0, OFF_KA // LANES + hi)),
                      pl.BlockSpec((None, s, LANES), lambda bi, hi, qi, sl: (bi, 0, OFF_VA // LANES + hi)),
                      vec, vec, vec, vec,
                      pl.BlockSpec((1, LANES), lambda bi, hi, qi, sl: (0, 0))],
            out_specs=pl.BlockSpec((None, tq, LANES), lambda bi, hi, qi, sl: (bi, qi, hi)),
            scratch_shapes=[pltpu.VMEM((s, 2 * LANES), BF16),
                            pltpu.VMEM((s // tk, LANES + ONES_ROWS, tk), BF16),
                            pltpu.VMEM((tk, 2 * tk), F32),
                            pltpu.VMEM((tk, 2 * tq), F32),
                            pltpu.VMEM((tk, 2 * tq), F32),
                            pltpu.VMEM((LANES + ONES_ROWS, 2 * tq), F32),
                            pltpu.VMEM((1, 2 * tq), F32)]),
        compiler_params=_cparams(("arbitrary", "arbitrary", "arbitrary")),
    )(slopes, p3, p3, p3, lq1, lk1, lq2, lk2, g)


GLA_HEADS_PER_STEP = 2


def _gla_kernel(q_ref, k_ref, v_ref, gb_ref, la_ref, g_ref, o_ref, state_scr, *, tt):
    c = GLA_CHUNK
    dk, dv = KEY_DIM_GLA, VAL_DIM_GLA
    hp = state_scr.shape[0]
    n_c = tt // c
    work = [(hh, ci) for hh in range(hp) for ci in range(n_c)]

    @pl.when(pl.program_id(2) == 0)
    def _():
        state_scr[...] = jnp.zeros(state_scr.shape, F32)

    rr = lax.broadcasted_iota(jnp.int32, (c, c), 0)
    cr = lax.broadcasted_iota(jnp.int32, (c, c), 1)
    causal = cr <= rr
    tri = jnp.where(causal, 1.0, 0.0).astype(BF16)
    qscale = dk ** -0.5
    rows = lambda ci: slice(ci * c, (ci + 1) * c)
    kcols = lambda hh: slice(hh * dk, (hh + 1) * dk)
    vcols = lambda hh: slice(hh * dv, (hh + 1) * dv)

    cum = {}
    for hh, ci in work:
        la = la_ref[rows(ci), kcols(hh)]
        la_hi = la.astype(BF16)
        la_lo = (la - la_hi.astype(F32)).astype(BF16)
        cum2 = jnp.dot(tri, jnp.concatenate([la_hi, la_lo], axis=1), preferred_element_type=F32)
        cum[hh, ci] = cum2[:, :dk] + cum2[:, dk:]

    qe, q2, k2, kd, dec, v = {}, {}, {}, {}, {}, {}
    for hh, ci in work:
        cum_c = cum[hh, ci]
        last = cum_c[c - 1:c]
        mid = cum_c[c // 2:c // 2 + 1]
        q_c = q_ref[rows(ci), kcols(hh)].astype(F32) * qscale
        k_c = k_ref[rows(ci), kcols(hh)].astype(F32)
        qe[hh, ci] = (q_c * jnp.exp(cum_c)).astype(BF16)
        q2[hh, ci] = (q_c * jnp.exp(cum_c - mid)).astype(BF16)
        k2[hh, ci] = (k_c * jnp.exp(mid - cum_c)).astype(BF16)
        kd[hh, ci] = (k_c * jnp.exp(last - cum_c)).astype(BF16)
        dec[hh, ci] = jnp.exp(last)
        v[hh, ci] = v_ref[rows(ci), vcols(hh)]

    att = {w: lax.dot_general(q2[w], k2[w], (((1,), (1,)), ((), ())), preferred_element_type=F32)
           for w in work}
    o_intra = {w: jnp.dot(jnp.where(causal, att[w], 0.0).astype(BF16), v[w], preferred_element_type=F32)
               for w in work}
    kv = {w: jnp.dot(v[w].astype(F32).T.astype(BF16), kd[w], preferred_element_type=F32) for w in work}

    o_inter = {}
    for hh in range(hp):
        st = state_scr[hh]
        for ci in range(n_c):
            o_inter[hh, ci] = lax.dot_general(qe[hh, ci], st.astype(BF16), (((1,), (1,)), ((), ())),
                                              preferred_element_type=F32)
            st = st * dec[hh, ci] + kv[hh, ci]
        state_scr[hh] = st

    for hh, ci in work:
        o = o_inter[hh, ci] + o_intra[hh, ci]
        o = o * lax.rsqrt(jnp.mean(o * o, axis=-1, keepdims=True) + LN_EPS) * g_ref[...]
        gate = gb_ref[rows(ci), vcols(hh)].astype(F32)
        o_ref[rows(ci), vcols(hh)] = (o * (gate * _sigmoid(gate))).astype(o_ref.dtype)


def _gla(p3, la3, g, tt=512):
    b, s, _ = p3.shape
    hp = GLA_HEADS_PER_STEP
    dk, dv = KEY_DIM_GLA * hp, VAL_DIM_GLA * hp
    return pl.pallas_call(
        functools.partial(_gla_kernel, tt=tt),
        out_shape=jax.ShapeDtypeStruct((b, s, N_HEADS_GLA * VAL_DIM_GLA), BF16),
        grid=(b, N_HEADS_GLA // hp, s // tt),
        in_specs=[pl.BlockSpec((None, tt, dk), lambda bi, hi, ti: (bi, ti, OFF_QB // dk + hi)),
                  pl.BlockSpec((None, tt, dk), lambda bi, hi, ti: (bi, ti, OFF_KB // dk + hi)),
                  pl.BlockSpec((None, tt, dv), lambda bi, hi, ti: (bi, ti, OFF_VB // dv + hi)),
                  pl.BlockSpec((None, tt, dv), lambda bi, hi, ti: (bi, ti, OFF_GB // dv + hi)),
                  pl.BlockSpec((None, tt, dk), lambda bi, hi, ti: (bi, ti, hi)),
                  pl.BlockSpec((1, VAL_DIM_GLA), lambda bi, hi, ti: (0, 0))],
        out_specs=pl.BlockSpec((None, tt, dv), lambda bi, hi, ti: (bi, ti, hi)),
        scratch_shapes=[pltpu.VMEM((hp, VAL_DIM_GLA, KEY_DIM_GLA), F32)],
        compiler_params=_cparams(("arbitrary", "arbitrary", "arbitrary")),
    )(p3, p3, p3, p3, la3, g)


def _mix_kernel(oa_ref, ob_ref, gt_ref, x_ref, bg_ref, g1_ref, sh2_ref, sc2_ref, lng_ref, lnb_ref,
                wba_ref, wbb_ref, wo_ref, wrh_ref, wrl_ref, br_ref,
                x1_ref, u2_ref, lg_ref, *, alpha):
    d = x_ref.shape[-1]
    a = jnp.dot(oa_ref[...], wba_ref[...], preferred_element_type=F32)
    bm = jnp.dot(ob_ref[...], wbb_ref[...], preferred_element_type=F32)
    gates = _sigmoid(gt_ref[...].astype(F32) + bg_ref[...])
    mixed = gates[:, :d] * a + gates[:, d:] * bm
    y = jnp.dot(mixed.astype(BF16), wo_ref[...], preferred_element_type=F32)
    x1 = _standardize(alpha * x_ref[...] + g1_ref[0] * y) * lng_ref[...] + lnb_ref[...]
    x1_ref[...] = x1
    u2 = _standardize(x1) * (1.0 + sc2_ref[0]) + sh2_ref[0]
    _store_token_tiles(u2_ref, u2)
    uh = u2.astype(BF16)
    ul = (u2 - uh.astype(F32)).astype(BF16)
    lg_ref[...] = (jnp.dot(uh, wrh_ref[...], preferred_element_type=F32)
                   + jnp.dot(uh, wrl_ref[...], preferred_element_type=F32)
                   + jnp.dot(ul, wrh_ref[...], preferred_element_type=F32) + br_ref[...])


def _mix(oa, ob, p, x2, bg, g1, sh2, sc2, lng, lnb, wba, wbb, wo, wrh, wrl, br, seq, alpha, tm=512):
    n, d = x2.shape
    tpb = seq // tm
    row = lambda i: (i, 0)
    const = lambda i: (0, 0)
    per_b = pl.BlockSpec((1, 1, d), lambda i: (i // tpb, 0, 0))
    return pl.pallas_call(
        functools.partial(_mix_kernel, alpha=alpha),
        out_shape=(jax.ShapeDtypeStruct((n, d), F32),
                   jax.ShapeDtypeStruct((n * (d // LANES), LANES), F32),
                   jax.ShapeDtypeStruct((n, LANES), F32)),
        grid=(n // tm,),
        in_specs=[pl.BlockSpec((tm, d), row), pl.BlockSpec((tm, d), row),
                  pl.BlockSpec((tm, 2 * d), lambda i: (i, OFF_GATES // (2 * d))),
                  pl.BlockSpec((tm, d), row),
                  pl.BlockSpec((1, 2 * d), const), per_b, per_b, per_b,
                  pl.BlockSpec((1, d), const), pl.BlockSpec((1, d), const),
                  pl.BlockSpec((d, d), const), pl.BlockSpec((d, d), const), pl.BlockSpec((d, d), const),
                  pl.BlockSpec((d, LANES), const), pl.BlockSpec((d, LANES), const),
                  pl.BlockSpec((1, LANES), const)],
        out_specs=(pl.BlockSpec((tm, d), row), pl.BlockSpec((tm * (d // LANES), LANES), row),
                   pl.BlockSpec((tm, LANES), row)),
        compiler_params=_cparams(("arbitrary",)),
    )(oa, ob, p, x2, bg, g1, sh2, sc2, lng, lnb, wba, wbb, wo, wrh, wrl, br)


def _route_kernel(lg_ref, ri_ref, rw_ref, cnt_ref, *, tm):
    @pl.when(pl.program_id(0) == 0)
    def _():
        cnt_ref[...] = jnp.zeros(cnt_ref.shape, F32)

    lg = lg_ref[...]
    lane = lax.broadcasted_iota(jnp.int32, lg.shape, 1)
    lanef = lane.astype(F32)
    far = float(LANES)
    is_g = lane < N_GROUPS
    gl = jnp.where(is_g, lg, NEG_BIG)
    gmax = jnp.max(gl, axis=-1, keepdims=True)
    gidx = jnp.min(jnp.where(gl == gmax, lanef, far), axis=-1, keepdims=True)
    gw = 1.0 / jnp.sum(jnp.where(is_g, jnp.exp(gl - gmax), 0.0), axis=-1, keepdims=True)
    lo = N_GROUPS + gidx * EXPERTS_PER_GROUP
    in_g = (lanef >= lo) & (lanef < lo + EXPERTS_PER_GROUP)
    el = jnp.where(in_g, lg, NEG_BIG)
    v1 = jnp.max(el, axis=-1, keepdims=True)
    i1 = jnp.min(jnp.where(in_g & (el == v1), lanef, far), axis=-1, keepdims=True)
    in_g2 = in_g & (lanef != i1)
    el2 = jnp.where(in_g2, lg, NEG_BIG)
    v2 = jnp.max(el2, axis=-1, keepdims=True)
    i2 = jnp.min(jnp.where(in_g2 & (el2 == v2), lanef, far), axis=-1, keepdims=True)
    t = jnp.exp(v2 - v1)
    w1 = gw / (1.0 + t)
    w2 = gw * t / (1.0 + t)

    oh1 = lanef == i1
    oh2 = lanef == i2
    oh = jnp.where(oh1 | oh2, 1.0, 0.0)
    r = lax.broadcasted_iota(jnp.int32, (tm, tm), 0)
    c = lax.broadcasted_iota(jnp.int32, (tm, tm), 1)
    lower = jnp.where(c < r, 1.0, 0.0).astype(BF16)
    base = jnp.dot(lower, oh.astype(BF16), preferred_element_type=F32) + cnt_ref[0:1, :]
    r1 = jnp.sum(jnp.where(oh1, base, 0.0), axis=-1, keepdims=True)
    r2 = jnp.sum(jnp.where(oh2, base, 0.0), axis=-1, keepdims=True)
    cnt_ref[...] = cnt_ref[...] + jnp.sum(oh, axis=0, keepdims=True)

    e1 = i1 - float(N_GROUPS)
    e2 = i2 - float(N_GROUPS)
    ri = jnp.where(lane == 0, e1, jnp.where(lane == 1, e2, jnp.where(lane == 2, r1, jnp.where(lane == 3, r2, 0.0))))
    ri_ref[...] = ri.astype(jnp.int32)
    rw_ref[...] = jnp.where(lane == 0, w1, jnp.where(lane == 1, w2, 0.0))


def _route(lg, tm=512):
    n = lg.shape[0]
    row = lambda i: (i, 0)
    return pl.pallas_call(
        functools.partial(_route_kernel, tm=tm),
        out_shape=(jax.ShapeDtypeStruct((n, LANES), jnp.int32),
                   jax.ShapeDtypeStruct((n, LANES), F32),
                   jax.ShapeDtypeStruct((8, LANES), F32)),
        grid=(n // tm,),
        in_specs=[pl.BlockSpec((tm, LANES), row)],
        out_specs=(pl.BlockSpec((tm, LANES), row), pl.BlockSpec((tm, LANES), row),
                   pl.BlockSpec((8, LANES), lambda i: (0, 0))),
        compiler_params=_cparams(("arbitrary",)),
    )(lg)


MOE_ROWS = 256


def _expert_kernel(be_ref, nu_ref, tok_a_ref, tok_b_ref, dst_ref, u_ref, wg_ref, wu_ref, wd_ref, y_ref,
                   xb0, xb1, yb0, yb1, wg_scr, wu_scr, wd_scr, gsem, ssem):
    i = pl.program_id(0)
    n_used = nu_ref[0]
    blk = tok_a_ref.shape[-1]
    xbs, ybs = (xb0, xb1), (yb0, yb1)

    def tile(ref, t):
        return ref.at[pl.ds(pl.multiple_of(t * ROW_TILES, ROW_TILES), ROW_TILES)]

    def gather_issue(tok_ref, xb, sem):
        for r in range(blk):
            pltpu.make_async_copy(tile(u_ref, tok_ref[0, 0, r]), tile(xb, r), sem).start()

    def gather_wait(xb, sem):
        pltpu.make_async_copy(u_ref.at[pl.ds(0, blk * ROW_TILES)], xb, sem).wait()

    def scatter_issue(yb, sem):
        for r in range(blk):
            pltpu.make_async_copy(tile(yb, r), tile(y_ref, dst_ref[0, 0, r]), sem).start()

    def scatter_wait(yb, sem):
        pltpu.make_async_copy(yb, y_ref.at[pl.ds(0, blk * ROW_TILES)], sem).wait()

    def load_weights():
        @pl.when((i == 0) | (be_ref[i] != be_ref[jnp.maximum(i - 1, 0)]))
        def _():
            wg_scr[...] = wg_ref[0].astype(BF16)
            wu_scr[...] = wu_ref[0].astype(BF16)
            wd_scr[...] = wd_ref[0].astype(BF16)

    def compute(xb, yb):
        x = _load_token_tiles(xb).astype(BF16)
        g = jnp.dot(x, wg_scr[...], preferred_element_type=F32)
        u = jnp.dot(x, wu_scr[...], preferred_element_type=F32)
        hid = (g * _sigmoid(g) * u).astype(BF16)
        _store_token_tiles(yb, jnp.dot(hid, wd_scr[...], preferred_element_type=F32))

    @pl.when(i == 0)
    def _():
        gather_issue(tok_a_ref, xb0, gsem.at[0])
        gather_wait(xb0, gsem.at[0])
        load_weights()
        gather_issue(tok_b_ref, xb1, gsem.at[1])
        compute(xb0, yb0)

    for p in range(2):
        cur, oth = p, 1 - p

        @pl.when((i >= 1) & (i < n_used) & (i % 2 == p))
        def _():
            gather_wait(xbs[cur], gsem.at[cur])

            @pl.when(i >= 2)
            def _():
                scatter_wait(ybs[cur], ssem.at[cur])

            load_weights()
            gather_issue(tok_b_ref, xbs[oth], gsem.at[oth])
            scatter_issue(ybs[oth], ssem.at[oth])
            compute(xbs[cur], ybs[cur])

        @pl.when((i == n_used) & (i % 2 == p))
        def _():
            gather_wait(xbs[cur], gsem.at[cur])

            @pl.when(i >= 2)
            def _():
                scatter_wait(ybs[cur], ssem.at[cur])

            scatter_issue(ybs[oth], ssem.at[oth])

        @pl.when((i == n_used + 1) & (i % 2 == p))
        def _():
            scatter_wait(ybs[cur], ssem.at[cur])


def _experts(blk_e, n_used, src_tok3, dst_row3, u2t, wg, wu, wd, layer, n_rows):
    d, de = wg.shape[-2:]
    blk = MOE_ROWS
    rt = d // LANES
    n_steps = blk_e.shape[0]
    wspec = lambda r, c: pl.BlockSpec((None, 1, r, c), lambda i, be, nu: (layer, be[i], 0, 0))
    ispec = lambda f: pl.BlockSpec((1, 1, blk), lambda i, be, nu: (f(i, nu), 0, 0), memory_space=pltpu.SMEM)
    return pl.pallas_call(
        _expert_kernel,
        out_shape=jax.ShapeDtypeStruct((n_rows * rt, LANES), F32),
        grid_spec=pltpu.PrefetchScalarGridSpec(
            num_scalar_prefetch=2,
            grid=(n_steps,),
            in_specs=[ispec(lambda i, nu: jnp.minimum(i, nu[0] - 1)),
                      ispec(lambda i, nu: jnp.minimum(i + 1, nu[0] - 1)),
                      ispec(lambda i, nu: jnp.clip(i - 1, 0, nu[0] - 1)),
                      pl.BlockSpec(memory_space=pl.ANY),
                      wspec(d, de), wspec(d, de), wspec(de, d)],
            out_specs=pl.BlockSpec(memory_space=pl.ANY),
            scratch_shapes=[pltpu.VMEM((blk * rt, LANES), F32), pltpu.VMEM((blk * rt, LANES), F32),
                            pltpu.VMEM((blk * rt, LANES), F32), pltpu.VMEM((blk * rt, LANES), F32),
                            pltpu.VMEM((d, de), BF16), pltpu.VMEM((d, de), BF16),
                            pltpu.VMEM((de, d), BF16),
                            pltpu.SemaphoreType.DMA((2,)), pltpu.SemaphoreType.DMA((2,))]),
        compiler_params=_cparams(("arbitrary",), has_side_effects=True),
    )(blk_e, n_used, src_tok3, src_tok3, dst_row3, u2t, wg, wu, wd)


def _combine_kernel(x1_ref, y1_ref, y2_ref, rw_ref, g2_ref, lng_ref, lnb_ref, o_ref, *, alpha):
    rw = rw_ref[...]
    y = rw[:, 0:1] * _load_token_tiles(y1_ref) + rw[:, 1:2] * _load_token_tiles(y2_ref)
    o_ref[...] = _standardize(alpha * x1_ref[...] + g2_ref[0] * y) * lng_ref[...] + lnb_ref[...]


def _combine(x1, y_tok, rw, g2, lng, lnb, seq, alpha, tm=512):
    n, d = x1.shape
    tpb = seq // tm
    const = lambda i: (0, 0)
    return pl.pallas_call(
        functools.partial(_combine_kernel, alpha=alpha),
        out_shape=jax.ShapeDtypeStruct((n, d), F32),
        grid=(n // tm,),
        in_specs=[pl.BlockSpec((tm, d), lambda i: (i, 0)),
                  pl.BlockSpec((tm * (d // LANES), LANES), lambda i: (i, 0)),
                  pl.BlockSpec((tm * (d // LANES), LANES), lambda i: (n // tm + i, 0)),
                  pl.BlockSpec((tm, LANES), lambda i: (i, 0)),
                  pl.BlockSpec((1, 1, d), lambda i: (i // tpb, 0, 0)),
                  pl.BlockSpec((1, d), const), pl.BlockSpec((1, d), const)],
        out_specs=pl.BlockSpec((tm, d), lambda i: (i, 0)),
        compiler_params=_cparams(("arbitrary",)),
    )(x1, y_tok, y_tok, rw, g2, lng, lnb)


def _moe_plan(ri, cnt, n):
    blk = MOE_ROWS
    n_rows = 2 * n + N_EXPERTS * blk
    n_blocks = n_rows // blk
    n_steps = n_blocks + 2
    counts = cnt[0, N_GROUPS:N_GROUPS + N_EXPERTS].astype(jnp.int32)
    padded = ((counts + blk - 1) // blk) * blk
    pends = jnp.cumsum(padded)
    pstarts = pends - padded
    cstarts = jnp.cumsum(counts) - counts
    pos = jnp.take(pstarts, ri[:, 0:2], axis=0) + ri[:, 2:4]
    blk_start = jnp.arange(n_steps, dtype=jnp.int32) * blk
    blk_e = jnp.minimum(jnp.sum(blk_start[:, None] >= pends[None, :], axis=1), N_EXPERTS - 1).astype(jnp.int32)
    n_used = (pends[-1:] // blk).astype(jnp.int32)
    prow = jnp.arange(n_rows, dtype=jnp.int32)
    pe = jnp.repeat(blk_e[:n_blocks], blk)
    spare = 2 * n + prow - (cstarts[pe] + jnp.minimum(prow - pstarts[pe], counts[pe]))
    dst_row = spare.at[pos.T.reshape(-1)].set(jnp.arange(2 * n, dtype=jnp.int32), unique_indices=True)
    src_tok = dst_row % n
    return blk_e, n_used, src_tok.reshape(n_blocks, 1, blk), dst_row.reshape(n_blocks, 1, blk)


def kernel(x, c, w_ada, b_ada, w_in, b_gates, w_alpha, b_alpha, lambda_q1, lambda_k1, lambda_q2, lambda_k2, diff_norm_g, gla_norm_g, w_branch_a, w_branch_b, w_out, ln1_g, ln1_b, w_router_g, b_router_g, w_router_e, b_router_e, w_gate_e, w_up_e, w_down_e, ln2_g, ln2_b):
    b, s, d = x.shape
    depth = w_ada.shape[0]
    n = b * s
    alpha = (2.0 * depth) ** 0.25

    ada = _ada(c, w_ada, b_ada)
    x2 = x.reshape(n, d)
    for l in range(depth):
        sh1, sc1, g1, sh2, sc2, g2 = [ada[l, :, i * d:(i + 1) * d].reshape(b, 1, d) for i in range(6)]
        wl = w_in[l]
        w_main = jnp.concatenate([wl[:, :6144], wl[:, 6144 + GLA_RANK:]], axis=1).astype(BF16)
        w_ab = jnp.pad(wl[:, 6144:6144 + GLA_RANK], ((0, 0), (0, LANES - GLA_RANK))).astype(BF16)
        wal = jnp.pad(w_alpha[l], ((0, LANES - GLA_RANK), (0, 0))).astype(BF16)
        p, la = _inproj(x2, sh1, sc1, w_main, w_ab, wal, b_alpha[l].reshape(1, -1), s)
        p3 = p.reshape(b, s, W_MAIN)

        lam_init = 0.8 - 0.6 * math.exp(-0.3 * l)
        oa = _attn(p3, lambda_q1[l].reshape(1, -1), lambda_k1[l].reshape(1, -1),
                   lambda_q2[l].reshape(1, -1), lambda_k2[l].reshape(1, -1),
                   diff_norm_g[l].reshape(1, -1), lam_init)
        ob = _gla(p3, la.reshape(b, s, -1), gla_norm_g[l].reshape(1, -1))

        wr = jnp.pad(jnp.concatenate([w_router_g[l], w_router_e[l]], axis=1),
                     ((0, 0), (0, LANES - N_GROUPS - N_EXPERTS)))
        wrh = wr.astype(BF16)
        wrl = (wr - wrh.astype(F32)).astype(BF16)
        br = jnp.pad(jnp.concatenate([b_router_g[l], b_router_e[l]]),
                     (0, LANES - N_GROUPS - N_EXPERTS)).reshape(1, LANES)
        x1, u2, lg = _mix(oa.reshape(n, d), ob.reshape(n, d), p, x2, b_gates[l].reshape(1, -1),
                          g1, sh2, sc2, ln1_g[l].reshape(1, d), ln1_b[l].reshape(1, d),
                          w_branch_a[l].astype(BF16), w_branch_b[l].astype(BF16), w_out[l].astype(BF16),
                          wrh, wrl, br, s, alpha)

        ri, rw, cnt = _route(lg)
        blk_e, n_used, src_tok3, dst_row3 = _moe_plan(ri, cnt, n)
        y_tok = _experts(blk_e, n_used, src_tok3, dst_row3, u2, w_gate_e, w_up_e, w_down_e, l,
                         2 * n + N_EXPERTS * MOE_ROWS)
        x2 = _combine(x1, y_tok, rw, g2, ln2_g[l].reshape(1, d), ln2_b[l].reshape(1, d), s, alpha)
    return x2.reshape(b, s, d)
```

```python
import functools
import math

import jax
import jax.numpy as jnp
import numpy as np
from jax import lax
from jax.experimental import pallas as pl
from jax.experimental.pallas import tpu as pltpu

F32 = jnp.float32
BF16 = jnp.bfloat16

N_HEADS_DIFF = 8
HEAD_DIM_DIFF = 64
N_HEADS_GLA = 4
KEY_DIM_GLA = 128
VAL_DIM_GLA = 256
GLA_RANK = 16
GLA_TAU = 16.0
GLA_CHUNK = 64
N_GROUPS = 4
EXPERTS_PER_GROUP = 8
N_EXPERTS = N_GROUPS * EXPERTS_PER_GROUP
LN_EPS = 1e-5
LANES = 128
NEG_BIG = -1e30

OFF_QA, OFF_KA, OFF_VA = 0, 1024, 2048
OFF_QB, OFF_KB, OFF_VB, OFF_GB, OFF_GATES = 3072, 3584, 4096, 5120, 6144
W_MAIN = 8192

VMEM_LIMIT = 56 * 1024 * 1024


def _cparams(sem, **kw):
    return pltpu.CompilerParams(dimension_semantics=sem, vmem_limit_bytes=VMEM_LIMIT, **kw)


def _standardize(x):
    mu = jnp.mean(x, axis=-1, keepdims=True)
    xc = x - mu
    var = jnp.mean(xc * xc, axis=-1, keepdims=True)
    return xc * lax.rsqrt(var + LN_EPS)


def _sigmoid(x):
    return 1.0 / (1.0 + jnp.exp(-x))


ROW_TILES = 8


def _store_token_tiles(ref, x):
    t = x.shape[0]
    for s in range(ROW_TILES):
        ref[pl.ds(s, t, stride=ROW_TILES), :] = x[:, s * LANES:(s + 1) * LANES]


def _load_token_tiles(ref):
    t = ref.shape[0] // ROW_TILES
    return jnp.concatenate([ref[pl.ds(s, t, stride=ROW_TILES), :] for s in range(ROW_TILES)], axis=1)


def _ada_kernel(c_ref, w_ref, b_ref, o_ref):
    c = c_ref[...]
    cond = c * _sigmoid(c)
    o_ref[0] = jnp.dot(cond, w_ref[0], preferred_element_type=F32,
                       precision=lax.Precision.HIGHEST) + b_ref[0]


def _ada(c, w_ada, b_ada):
    depth, d, d6 = w_ada.shape
    b = c.shape[0]
    return pl.pallas_call(
        _ada_kernel,
        out_shape=jax.ShapeDtypeStruct((depth, b, d6), F32),
        grid=(depth, d6 // d),
        in_specs=[pl.BlockSpec((b, d), lambda l, j: (0, 0)),
                  pl.BlockSpec((1, d, d), lambda l, j: (l, 0, j)),
                  pl.BlockSpec((1, 1, d), lambda l, j: (l, 0, j))],
        out_specs=pl.BlockSpec((1, b, d), lambda l, j: (l, 0, j)),
        compiler_params=_cparams(("arbitrary", "arbitrary")),
    )(c, w_ada, b_ada.reshape(depth, 1, d6))


def _prep_tile(x, sh_ref, sc_ref, wab_ref, wal_ref, bal_ref, la_ref, u_scr):
    u = _standardize(x) * (1.0 + sc_ref[0]) + sh_ref[0]
    ub = u.astype(BF16)
    u_scr[...] = ub
    ab = jnp.dot(ub, wab_ref[...], preferred_element_type=F32)
    pre = jnp.dot(ab.astype(BF16), wal_ref[...], preferred_element_type=F32) + bal_ref[...]
    la_ref[...] = (jnp.minimum(pre, 0.0) - jnp.log(1.0 + jnp.exp(-jnp.abs(pre)))) * (1.0 / GLA_TAU)


def _inproj_kernel(x_ref, sh_ref, sc_ref, w_ref, wab_ref, wal_ref, bal_ref,
                   p_ref, la_ref, u_scr):
    @pl.when(pl.program_id(1) == 0)
    def _():
        _prep_tile(x_ref[...], sh_ref, sc_ref, wab_ref, wal_ref, bal_ref, la_ref, u_scr)

    p_ref[...] = jnp.dot(u_scr[...], w_ref[...], preferred_element_type=F32).astype(BF16)


def _inproj(x2, sh, sc, w_main, w_ab, w_alpha, b_alpha, seq, tm=1024, tn=1024):
    n, d = x2.shape
    tpb = seq // tm
    wq = w_alpha.shape[1]
    return pl.pallas_call(
        _inproj_kernel,
        out_shape=(jax.ShapeDtypeStruct((n, W_MAIN), BF16),
                   jax.ShapeDtypeStruct((n, wq), F32)),
        grid=(n // tm, W_MAIN // tn),
        in_specs=[pl.BlockSpec((tm, d), lambda i, j: (i, 0)),
                  pl.BlockSpec((1, 1, d), lambda i, j: (i // tpb, 0, 0)),
                  pl.BlockSpec((1, 1, d), lambda i, j: (i // tpb, 0, 0)),
                  pl.BlockSpec((d, tn), lambda i, j: (0, j)),
                  pl.BlockSpec((d, LANES), lambda i, j: (0, 0)),
                  pl.BlockSpec((LANES, wq), lambda i, j: (0, 0)),
                  pl.BlockSpec((1, wq), lambda i, j: (0, 0))],
        out_specs=(pl.BlockSpec((tm, tn), lambda i, j: (i, j)),
                   pl.BlockSpec((tm, wq), lambda i, j: (i, 0))),
        scratch_shapes=[pltpu.VMEM((tm, d), BF16)],
        compiler_params=_cparams(("arbitrary", "arbitrary")),
    )(x2, sh, sc, w_main, w_ab, w_alpha, b_alpha)


def _row_tile(ref, t):
    return ref.at[pl.ds(pl.multiple_of(t * ROW_TILES, ROW_TILES), ROW_TILES)]


def _combine_tile(x1_ref, ya, yb, rw_ref, g2_ref, lng_ref, lnb_ref, alpha):
    rw = rw_ref[...]
    y = rw[:, 0:1] * _load_token_tiles(ya) + rw[:, 1:2] * _load_token_tiles(yb)
    return _standardize(alpha * x1_ref[...] + g2_ref[0] * y) * lng_ref[...] + lnb_ref[...]


def _combine_inproj_kernel(pos_a_ref, pos_b_ref, x1_ref, rw_ref, g2_ref, lng_ref, lnb_ref, sh_ref, sc_ref,
                           y_ref, w_ref, wab_ref, wal_ref, bal_ref,
                           x2_ref, p_ref, la_ref, ya0, yb0, ya1, yb1, u_scr, sem, *, alpha):
    i, j = pl.program_id(0), pl.program_id(1)
    n_i, n_j = pl.num_programs(0), pl.num_programs(1)
    tm = x1_ref.shape[0]
    per_step = tm // W_MAIN_STEPS
    bufs = ((ya0, yb0), (ya1, yb1))

    def issue(pos_ref, slot, t):
        for s in range(2):
            pltpu.make_async_copy(_row_tile(y_ref, pos_ref[0, s, t]), _row_tile(bufs[slot][s], t),
                                  sem.at[slot]).start()

    def wait_slot(slot):
        for s in range(2):
            pltpu.make_async_copy(y_ref.at[pl.ds(0, tm * ROW_TILES)], bufs[slot][s], sem.at[slot]).wait()

    for p in range(2):
        @pl.when(i % 2 == p)
        def _():
            @pl.when(j == 0)
            def _():
                if p == 0:
                    @pl.when(i == 0)
                    def _():
                        def first(t, carry):
                            issue(pos_a_ref, 0, t)
                            return carry
                        lax.fori_loop(0, tm, first, 0)
                wait_slot(p)
                x2 = _combine_tile(x1_ref, bufs[p][0], bufs[p][1], rw_ref, g2_ref, lng_ref, lnb_ref, alpha)
                x2_ref[...] = x2
                _prep_tile(x2, sh_ref, sc_ref, wab_ref, wal_ref, bal_ref, la_ref, u_scr)

            for r in range(per_step):
                issue(pos_b_ref, 1 - p, j * per_step + r)
            p_ref[...] = jnp.dot(u_scr[...], w_ref[...], preferred_element_type=F32).astype(BF16)

            @pl.when((i == n_i - 1) & (j == n_j - 1))
            def _():
                wait_slot(1 - p)


W_MAIN_STEPS = 8


def _combine_inproj(pos3, x1, y_sorted, rw, g2, lng, lnb, sh, sc, w_main, w_ab, w_alpha, b_alpha,
                    seq, alpha, tm=512):
    n, d = x1.shape
    tn = W_MAIN // W_MAIN_STEPS
    tpb = seq // tm
    n_t = n // tm
    wq = w_alpha.shape[1]
    const = lambda i, j: (0, 0)
    per_b = pl.BlockSpec((1, 1, d), lambda i, j: (i // tpb, 0, 0))
    ybuf = pltpu.VMEM((tm * ROW_TILES, LANES), F32)
    return pl.pallas_call(
        functools.partial(_combine_inproj_kernel, alpha=alpha),
        out_shape=(jax.ShapeDtypeStruct((n, d), F32),
                   jax.ShapeDtypeStruct((n, W_MAIN), BF16),
                   jax.ShapeDtypeStruct((n, wq), F32)),
        grid=(n_t, W_MAIN_STEPS),
        in_specs=[pl.BlockSpec((1, 2, tm), lambda i, j: (0, 0, 0), memory_space=pltpu.SMEM),
                  pl.BlockSpec((1, 2, tm), lambda i, j: (jnp.minimum(i + 1, n_t - 1), 0, 0),
                               memory_space=pltpu.SMEM),
                  pl.BlockSpec((tm, d), lambda i, j: (i, 0)),
                  pl.BlockSpec((tm, LANES), lambda i, j: (i, 0)),
                  per_b, pl.BlockSpec((1, d), const), pl.BlockSpec((1, d), const), per_b, per_b,
                  pl.BlockSpec(memory_space=pl.ANY),
                  pl.BlockSpec((d, tn), lambda i, j: (0, j)),
                  pl.BlockSpec((d, LANES), const),
                  pl.BlockSpec((LANES, wq), const),
                  pl.BlockSpec((1, wq), const)],
        out_specs=(pl.BlockSpec((tm, d), lambda i, j: (i, 0)),
                   pl.BlockSpec((tm, tn), lambda i, j: (i, j)),
                   pl.BlockSpec((tm, wq), lambda i, j: (i, 0))),
        scratch_shapes=[ybuf, ybuf, ybuf, ybuf, pltpu.VMEM((tm, d), BF16), pltpu.SemaphoreType.DMA((2,))],
        compiler_params=_cparams(("arbitrary", "arbitrary")),
    )(pos3, pos3, x1, rw, g2, lng, lnb, sh, sc, y_sorted, w_main, w_ab, w_alpha, b_alpha)


def _combine_gather_kernel(pos_ref, x1_ref, rw_ref, g2_ref, lng_ref, lnb_ref, y_ref, o_ref, ya, yb, sem, *, alpha):
    tm = x1_ref.shape[0]

    def issue(t, carry):
        for s, buf in enumerate((ya, yb)):
            pltpu.make_async_copy(_row_tile(y_ref, pos_ref[0, s, t]), _row_tile(buf, t), sem).start()
        return carry

    lax.fori_loop(0, tm, issue, 0)
    for buf in (ya, yb):
        pltpu.make_async_copy(y_ref.at[pl.ds(0, tm * ROW_TILES)], buf, sem).wait()
    o_ref[...] = _combine_tile(x1_ref, ya, yb, rw_ref, g2_ref, lng_ref, lnb_ref, alpha)


def _combine_gather(pos3, x1, y_sorted, rw, g2, lng, lnb, seq, alpha, tm=512):
    n, d = x1.shape
    tpb = seq // tm
    const = lambda i: (0, 0)
    ybuf = pltpu.VMEM((tm * ROW_TILES, LANES), F32)
    return pl.pallas_call(
        functools.partial(_combine_gather_kernel, alpha=alpha),
        out_shape=jax.ShapeDtypeStruct((n, d), F32),
        grid=(n // tm,),
        in_specs=[pl.BlockSpec((1, 2, tm), lambda i: (i, 0, 0), memory_space=pltpu.SMEM),
                  pl.BlockSpec((tm, d), lambda i: (i, 0)),
                  pl.BlockSpec((tm, LANES), lambda i: (i, 0)),
                  pl.BlockSpec((1, 1, d), lambda i: (i // tpb, 0, 0)),
                  pl.BlockSpec((1, d), const), pl.BlockSpec((1, d), const),
                  pl.BlockSpec(memory_space=pl.ANY)],
        out_specs=pl.BlockSpec((tm, d), lambda i: (i, 0)),
        scratch_shapes=[ybuf, ybuf, pltpu.SemaphoreType.DMA(())],
        compiler_params=_cparams(("arbitrary",)),
    )(pos3, x1, rw, g2, lng, lnb, y_sorted)


ONES_ROWS = 16


QUERY_GROUP = 256
POS_SPLIT = 32


def _attn_kernel(slopes_ref, q_ref, k_ref, v_ref, lq1_ref, lk1_ref, lq2_ref, lk2_ref, g_ref,
                 o_ref, kaug_scr, vt_scr, dmask_scr, sa_scr, sb_scr, acc_scr, m_scr, *, tk, lam_init):
    h = pl.program_id(1)
    qi = pl.program_id(2)
    slope = slopes_ref[h]
    dh = HEAD_DIM_DIFF
    dv = LANES
    n_chunks = vt_scr.shape[0]
    halves = q_ref.shape[0] // tk

    @pl.when(qi == 0)
    def _():
        for j in range(n_chunks):
            vt_scr[j, 0:dv, :] = v_ref[j * tk:(j + 1) * tk, :].astype(F32).T.astype(BF16)
            vt_scr[j, dv:dv + ONES_ROWS, :] = jnp.ones((ONES_ROWS, tk), BF16)
        kaug_scr[:, 0:LANES] = k_ref[...]
        koff = lax.broadcasted_iota(jnp.int32, (tk, LANES), 0)
        flane = lax.broadcasted_iota(jnp.int32, (tk, LANES), 1)
        feat = jnp.where(flane == 0, koff // POS_SPLIT, jnp.where(flane == 1, koff % POS_SPLIT, 0))
        feat = feat.astype(F32).astype(BF16)
        for j in range(n_chunks):
            kaug_scr[j * tk:(j + 1) * tk, LANES:2 * LANES] = feat
        kpos = lax.broadcasted_iota(jnp.int32, (tk, 2 * tk), 0)
        qpos = lax.broadcasted_iota(jnp.int32, (tk, 2 * tk), 1)
        qoff = jnp.where(qpos >= tk, qpos - tk, qpos)
        dmask_scr[...] = jnp.where(qoff >= kpos, 0.0, NEG_BIG)

    q = q_ref[...] * jnp.asarray(dh ** -0.5, BF16)
    lane = lax.broadcasted_iota(jnp.int32, q.shape, 1)
    zero = jnp.zeros_like(q)
    qf = jnp.where(lane == 0, slope * POS_SPLIT, jnp.where(lane == 1, slope, 0.0)).astype(BF16)
    q1 = jnp.concatenate([jnp.where(lane < dh, q, zero), qf], axis=1)
    q2 = jnp.concatenate([jnp.where(lane >= dh, q, zero), qf], axis=1)
    qq = jnp.concatenate([part[hq * tk:(hq + 1) * tk] for hq in range(halves) for part in (q1, q2)], axis=0)

    m_scr[...] = jnp.full(m_scr.shape, NEG_BIG, F32)
    acc_scr[...] = jnp.zeros(acc_scr.shape, F32)

    per_half = 2 * tk // QUERY_GROUP
    groups = [(hq, slice((hq * per_half + g) * QUERY_GROUP, (hq * per_half + g + 1) * QUERY_GROUP),
               slice(g * QUERY_GROUP, (g + 1) * QUERY_GROUP))
              for hq in range(halves) for g in range(per_half)]

    def score(kaug, ls):
        return lax.dot_general(kaug, qq[ls], (((1,), (1,)), ((), ())), preferred_element_type=F32)

    def step(j_cur, src, j_next, dst, first_half, masked_half, next_first_half=0):
        if dst is not None:
            kaug = kaug_scr[pl.ds(pl.multiple_of(j_next * tk, tk), tk), :]
        vt = vt_scr[j_cur]
        m_all = m_scr[...]
        acc_all = acc_scr[...]
        done = []
        for hq, ls, ms in groups:
            if dst is not None and hq >= next_first_half:
                dst[:, ls] = score(kaug, ls)
            if hq < first_half:
                continue
            cj = (-slope) * ((halves * qi + hq - j_cur) * tk).astype(F32)
            s = src[:, ls]
            if hq == masked_half:
                s = s + dmask_scr[:, ms]
            m_prev = m_all[:, ls]
            m_new = jnp.maximum(m_prev, jnp.max(s, axis=0, keepdims=True) + cj)
            p = jnp.exp(s - (m_new - cj))
            alpha = jnp.exp(m_prev - m_new)
            done.append((ls, m_new, alpha * acc_all[:, ls] + jnp.dot(vt, p.astype(BF16),
                                                                     preferred_element_type=F32)))
        for ls, m_new, acc_new in done:
            m_scr[:, ls] = m_new
            acc_scr[:, ls] = acc_new

    kaug0 = kaug_scr[0:tk, :]
    for _, ls, _ in groups:
        sa_scr[:, ls] = score(kaug0, ls)

    def pair(jj, carry):
        j = 2 * jj
        step(j, sa_scr, j + 1, sb_scr, 0, None)
        step(j + 1, sb_scr, j + 2, sa_scr, 0, None)
        return carry

    lax.fori_loop(0, qi, pair, 0)
    step(2 * qi, sa_scr, 2 * qi + 1, sb_scr, 0, 0, next_first_half=1)
    step(2 * qi + 1, sb_scr, None, None, 1, 1)

    lam = (jnp.exp(jnp.sum(lq1_ref[...] * lk1_ref[...], axis=-1, keepdims=True))
           - jnp.exp(jnp.sum(lq2_ref[...] * lk2_ref[...], axis=-1, keepdims=True)) + lam_init)
    acc = acc_scr[...]
    ot = acc[0:dv] / acc[dv:dv + 1]
    for hq in range(halves):
        base = hq * 2 * tk
        o = (ot[:, base:base + tk] - lam * ot[:, base + tk:base + 2 * tk]).T
        o = o * lax.rsqrt(jnp.mean(o * o, axis=-1, keepdims=True) + LN_EPS) * g_ref[...] * (1.0 - lam_init)
        o_ref[hq * tk:(hq + 1) * tk, :] = o.astype(o_ref.dtype)


def _attn(p3, lq1, lk1, lq2, lk2, g, lam_init, tk=512):
    b, s, _ = p3.shape
    h = N_HEADS_DIFF
    tq = 2 * tk
    slopes = jnp.asarray(2.0 ** (-8.0 * np.arange(1, h + 1) / h), dtype=F32)
    vec = pl.BlockSpec((1, HEAD_DIM_DIFF), lambda bi, hi, qi, sl: (0, 0))
    return pl.pallas_call(
        functools.partial(_attn_kernel, tk=tk, lam_init=lam_init),
        out_shape=jax.ShapeDtypeStruct((b, s, h * LANES), BF16),
        grid_spec=pltpu.PrefetchScalarGridSpec(
            num_scalar_prefetch=1,
            grid=(b, h, s // tq),
            in_specs=[pl.BlockSpec((None, tq, LANES), lambda bi, hi, qi, sl: (bi, qi, OFF_QA // LANES + hi)),
                      pl.BlockSpec((None, s, LANES), lambda bi, hi, qi, sl: (bi, 0, OFF_KA // LANES + hi)),
                      pl.BlockSpec((None, s, LANES), lambda bi, hi, qi, sl: (bi, 0, OFF_VA // LANES + hi)),
                      vec, vec, vec, vec,
                      pl.BlockSpec((1, LANES), lambda bi, hi, qi, sl: (0, 0))],
            out_specs=pl.BlockSpec((None, tq, LANES), lambda bi, hi, qi, sl: (bi, qi, hi)),
            scratch_shapes=[pltpu.VMEM((s, 2 * LANES), BF16),
                            pltpu.VMEM((s // tk, LANES + ONES_ROWS, tk), BF16),
                            pltpu.VMEM((tk, 2 * tk), F32),
                            pltpu.VMEM((tk, 2 * tq), F32),
                            pltpu.VMEM((tk, 2 * tq), F32),
                            pltpu.VMEM((LANES + ONES_ROWS, 2 * tq), F32),
                            pltpu.VMEM((1, 2 * tq), F32)]),
        compiler_params=_cparams(("arbitrary", "arbitrary", "arbitrary")),
    )(slopes, p3, p3, p3, lq1, lk1, lq2, lk2, g)


GLA_HEADS_PER_STEP = 2


def _gla_kernel(q_ref, k_ref, v_ref, gb_ref, la_ref, g_ref, o_ref, state_scr, *, tt):
    c = GLA_CHUNK
    dk, dv = KEY_DIM_GLA, VAL_DIM_GLA
    hp = state_scr.shape[0]
    n_c = tt // c
    work = [(hh, ci) for hh in range(hp) for ci in range(n_c)]

    @pl.when(pl.program_id(2) == 0)
    def _():
        state_scr[...] = jnp.zeros(state_scr.shape, F32)

    rr = lax.broadcasted_iota(jnp.int32, (c, c), 0)
    cr = lax.broadcasted_iota(jnp.int32, (c, c), 1)
    causal = cr <= rr
    tri = jnp.where(causal, 1.0, 0.0).astype(BF16)
    qscale = dk ** -0.5
    rows = lambda ci: slice(ci * c, (ci + 1) * c)
    kcols = lambda hh: slice(hh * dk, (hh + 1) * dk)
    vcols = lambda hh: slice(hh * dv, (hh + 1) * dv)

    cum = {}
    for hh, ci in work:
        la = la_ref[rows(ci), kcols(hh)]
        la_hi = la.astype(BF16)
        la_lo = (la - la_hi.astype(F32)).astype(BF16)
        cum2 = jnp.dot(tri, jnp.concatenate([la_hi, la_lo], axis=1), preferred_element_type=F32)
        cum[hh, ci] = cum2[:, :dk] + cum2[:, dk:]

    qe, q2, k2, kd, dec, v = {}, {}, {}, {}, {}, {}
    for hh, ci in work:
        cum_c = cum[hh, ci]
        last = cum_c[c - 1:c]
        mid = cum_c[c // 2:c // 2 + 1]
        q_c = q_ref[rows(ci), kcols(hh)].astype(F32) * qscale
        k_c = k_ref[rows(ci), kcols(hh)].astype(F32)
        qe[hh, ci] = (q_c * jnp.exp(cum_c)).astype(BF16)
        q2[hh, ci] = (q_c * jnp.exp(cum_c - mid)).astype(BF16)
        k2[hh, ci] = (k_c * jnp.exp(mid - cum_c)).astype(BF16)
        kd[hh, ci] = (k_c * jnp.exp(last - cum_c)).astype(BF16)
        dec[hh, ci] = jnp.exp(last)
        v[hh, ci] = v_ref[rows(ci), vcols(hh)]

    att = {w: lax.dot_general(q2[w], k2[w], (((1,), (1,)), ((), ())), preferred_element_type=F32)
           for w in work}
    o_intra = {w: jnp.dot(jnp.where(causal, att[w], 0.0).astype(BF16), v[w], preferred_element_type=F32)
               for w in work}
    kv = {w: jnp.dot(v[w].astype(F32).T.astype(BF16), kd[w], preferred_element_type=F32) for w in work}

    o_inter = {}
    for hh in range(hp):
        st = state_scr[hh]
        for ci in range(n_c):
            o_inter[hh, ci] = lax.dot_general(qe[hh, ci], st.astype(BF16), (((1,), (1,)), ((), ())),
                                              preferred_element_type=F32)
            st = st * dec[hh, ci] + kv[hh, ci]
        state_scr[hh] = st

    for hh, ci in work:
        o = o_inter[hh, ci] + o_intra[hh, ci]
        o = o * lax.rsqrt(jnp.mean(o * o, axis=-1, keepdims=True) + LN_EPS) * g_ref[...]
        gate = gb_ref[rows(ci), vcols(hh)].astype(F32)
        o_ref[rows(ci), vcols(hh)] = (o * (gate * _sigmoid(gate))).astype(o_ref.dtype)


def _gla(p3, la3, g, tt=512):
    b, s, _ = p3.shape
    hp = GLA_HEADS_PER_STEP
    dk, dv = KEY_DIM_GLA * hp, VAL_DIM_GLA * hp
    return pl.pallas_call(
        functools.partial(_gla_kernel, tt=tt),
        out_shape=jax.ShapeDtypeStruct((b, s, N_HEADS_GLA * VAL_DIM_GLA), BF16),
        grid=(b, N_HEADS_GLA // hp, s // tt),
        in_specs=[pl.BlockSpec((None, tt, dk), lambda bi, hi, ti: (bi, ti, OFF_QB // dk + hi)),
                  pl.BlockSpec((None, tt, dk), lambda bi, hi, ti: (bi, ti, OFF_KB // dk + hi)),
                  pl.BlockSpec((None, tt, dv), lambda bi, hi, ti: (bi, ti, OFF_VB // dv + hi)),
                  pl.BlockSpec((None, tt, dv), lambda bi, hi, ti: (bi, ti, OFF_GB // dv + hi)),
                  pl.BlockSpec((None, tt, dk), lambda bi, hi, ti: (bi, ti, hi)),
                  pl.BlockSpec((1, VAL_DIM_GLA), lambda bi, hi, ti: (0, 0))],
        out_specs=pl.BlockSpec((None, tt, dv), lambda bi, hi, ti: (bi, ti, hi)),
        scratch_shapes=[pltpu.VMEM((hp, VAL_DIM_GLA, KEY_DIM_GLA), F32)],
        compiler_params=_cparams(("arbitrary", "arbitrary", "arbitrary")),
    )(p3, p3, p3, p3, la3, g)


def _mix_kernel(oa_ref, ob_ref, gt_ref, x_ref, bg_ref, g1_ref, sh2_ref, sc2_ref, lng_ref, lnb_ref,
                wba_ref, wbb_ref, wo_ref, wrh_ref, wrl_ref, br_ref,
                x1_ref, u2_ref, lg_ref, *, alpha):
    d = x_ref.shape[-1]
    a = jnp.dot(oa_ref[...], wba_ref[...], preferred_element_type=F32)
    bm = jnp.dot(ob_ref[...], wbb_ref[...], preferred_element_type=F32)
    gates = _sigmoid(gt_ref[...].astype(F32) + bg_ref[...])
    mixed = gates[:, :d] * a + gates[:, d:] * bm
    y = jnp.dot(mixed.astype(BF16), wo_ref[...], preferred_element_type=F32)
    x1 = _standardize(alpha * x_ref[...] + g1_ref[0] * y) * lng_ref[...] + lnb_ref[...]
    x1_ref[...] = x1
    u2 = _standardize(x1) * (1.0 + sc2_ref[0]) + sh2_ref[0]
    _store_token_tiles(u2_ref, u2)
    uh = u2.astype(BF16)
    ul = (u2 - uh.astype(F32)).astype(BF16)
    lg_ref[...] = (jnp.dot(uh, wrh_ref[...], preferred_element_type=F32)
                   + jnp.dot(uh, wrl_ref[...], preferred_element_type=F32)
                   + jnp.dot(ul, wrh_ref[...], preferred_element_type=F32) + br_ref[...])


def _mix(oa, ob, p, x2, bg, g1, sh2, sc2, lng, lnb, wba, wbb, wo, wrh, wrl, br, seq, alpha, tm=512):
    n, d = x2.shape
    tpb = seq // tm
    row = lambda i: (i, 0)
    const = lambda i: (0, 0)
    per_b = pl.BlockSpec((1, 1, d), lambda i: (i // tpb, 0, 0))
    return pl.pallas_call(
        functools.partial(_mix_kernel, alpha=alpha),
        out_shape=(jax.ShapeDtypeStruct((n, d), F32),
                   jax.ShapeDtypeStruct((n * (d // LANES), LANES), F32),
                   jax.ShapeDtypeStruct((n, LANES), F32)),
        grid=(n // tm,),
        in_specs=[pl.BlockSpec((tm, d), row), pl.BlockSpec((tm, d), row),
                  pl.BlockSpec((tm, 2 * d), lambda i: (i, OFF_GATES // (2 * d))),
                  pl.BlockSpec((tm, d), row),
                  pl.BlockSpec((1, 2 * d), const), per_b, per_b, per_b,
                  pl.BlockSpec((1, d), const), pl.BlockSpec((1, d), const),
                  pl.BlockSpec((d, d), const), pl.BlockSpec((d, d), const), pl.BlockSpec((d, d), const),
                  pl.BlockSpec((d, LANES), const), pl.BlockSpec((d, LANES), const),
                  pl.BlockSpec((1, LANES), const)],
        out_specs=(pl.BlockSpec((tm, d), row), pl.BlockSpec((tm * (d // LANES), LANES), row),
                   pl.BlockSpec((tm, LANES), row)),
        compiler_params=_cparams(("arbitrary",)),
    )(oa, ob, p, x2, bg, g1, sh2, sc2, lng, lnb, wba, wbb, wo, wrh, wrl, br)


def _route_kernel(lg_ref, ri_ref, rw_ref, cnt_ref, *, tm):
    @pl.when(pl.program_id(0) == 0)
    def _():
        cnt_ref[...] = jnp.zeros(cnt_ref.shape, F32)

    lg = lg_ref[...]
    lane = lax.broadcasted_iota(jnp.int32, lg.shape, 1)
    lanef = lane.astype(F32)
    far = float(LANES)
    is_g = lane < N_GROUPS
    gl = jnp.where(is_g, lg, NEG_BIG)
    gmax = jnp.max(gl, axis=-1, keepdims=True)
    gidx = jnp.min(jnp.where(gl == gmax, lanef, far), axis=-1, keepdims=True)
    gw = 1.0 / jnp.sum(jnp.where(is_g, jnp.exp(gl - gmax), 0.0), axis=-1, keepdims=True)
    lo = N_GROUPS + gidx * EXPERTS_PER_GROUP
    in_g = (lanef >= lo) & (lanef < lo + EXPERTS_PER_GROUP)
    el = jnp.where(in_g, lg, NEG_BIG)
    v1 = jnp.max(el, axis=-1, keepdims=True)
    i1 = jnp.min(jnp.where(in_g & (el == v1), lanef, far), axis=-1, keepdims=True)
    in_g2 = in_g & (lanef != i1)
    el2 = jnp.where(in_g2, lg, NEG_BIG)
    v2 = jnp.max(el2, axis=-1, keepdims=True)
    i2 = jnp.min(jnp.where(in_g2 & (el2 == v2), lanef, far), axis=-1, keepdims=True)
    t = jnp.exp(v2 - v1)
    w1 = gw / (1.0 + t)
    w2 = gw * t / (1.0 + t)

    oh1 = lanef == i1
    oh2 = lanef == i2
    oh = jnp.where(oh1 | oh2, 1.0, 0.0)
    r = lax.broadcasted_iota(jnp.int32, (tm, tm), 0)
    c = lax.broadcasted_iota(jnp.int32, (tm, tm), 1)
    lower = jnp.where(c < r, 1.0, 0.0).astype(BF16)
    base = jnp.dot(lower, oh.astype(BF16), preferred_element_type=F32) + cnt_ref[0:1, :]
    r1 = jnp.sum(jnp.where(oh1, base, 0.0), axis=-1, keepdims=True)
    r2 = jnp.sum(jnp.where(oh2, base, 0.0), axis=-1, keepdims=True)
    cnt_ref[...] = cnt_ref[...] + jnp.sum(oh, axis=0, keepdims=True)

    e1 = i1 - float(N_GROUPS)
    e2 = i2 - float(N_GROUPS)
    ri = jnp.where(lane == 0, e1, jnp.where(lane == 1, e2, jnp.where(lane == 2, r1, jnp.where(lane == 3, r2, 0.0))))
    ri_ref[...] = ri.astype(jnp.int32)
    rw_ref[...] = jnp.where(lane == 0, w1, jnp.where(lane == 1, w2, 0.0))


def _route(lg, tm=512):
    n = lg.shape[0]
    row = lambda i: (i, 0)
    return pl.pallas_call(
        functools.partial(_route_kernel, tm=tm),
        out_shape=(jax.ShapeDtypeStruct((n, LANES), jnp.int32),
                   jax.ShapeDtypeStruct((n, LANES), F32),
                   jax.ShapeDtypeStruct((8, LANES), F32)),
        grid=(n // tm,),
        in_specs=[pl.BlockSpec((tm, LANES), row)],
        out_specs=(pl.BlockSpec((tm, LANES), row), pl.BlockSpec((tm, LANES), row),
                   pl.BlockSpec((8, LANES), lambda i: (0, 0))),
        compiler_params=_cparams(("arbitrary",)),
    )(lg)


MOE_ROWS = 256


def _expert_kernel(be_ref, nu_ref, tok_a_ref, tok_b_ref, u_ref, wg_ref, wu_ref, wd_ref, y_ref,
                   xb0, xb1, wg_scr, wu_scr, wd_scr, gsem):
    i = pl.program_id(0)
    n_used = nu_ref[0]
    blk = tok_a_ref.shape[-1]
    xbs = (xb0, xb1)

    def gather_issue(tok_ref, xb, sem):
        for r in range(blk):
            pltpu.make_async_copy(_row_tile(u_ref, tok_ref[0, 0, r]), _row_tile(xb, r), sem).start()

    def gather_wait(xb, sem):
        pltpu.make_async_copy(u_ref.at[pl.ds(0, blk * ROW_TILES)], xb, sem).wait()

    def compute(xb):
        x = _load_token_tiles(xb).astype(BF16)
        g = jnp.dot(x, wg_scr[...], preferred_element_type=F32)
        u = jnp.dot(x, wu_scr[...], preferred_element_type=F32)
        hid = (g * _sigmoid(g) * u).astype(BF16)
        _store_token_tiles(y_ref, jnp.dot(hid, wd_scr[...], preferred_element_type=F32))

    @pl.when(i == 0)
    def _():
        gather_issue(tok_a_ref, xb0, gsem.at[0])

    for p in range(2):
        @pl.when((i < n_used) & (i % 2 == p))
        def _():
            gather_wait(xbs[p], gsem.at[p])

            @pl.when((i == 0) | (be_ref[i] != be_ref[jnp.maximum(i - 1, 0)]))
            def _():
                wg_scr[...] = wg_ref[0].astype(BF16)
                wu_scr[...] = wu_ref[0].astype(BF16)
                wd_scr[...] = wd_ref[0].astype(BF16)

            @pl.when(i + 1 < n_used)
            def _():
                gather_issue(tok_b_ref, xbs[1 - p], gsem.at[1 - p])
                compute(xbs[p])

            @pl.when(i + 1 >= n_used)
            def _():
                compute(xbs[p])

    @pl.when(i >= n_used)
    def _():
        y_ref[...] = jnp.zeros(y_ref.shape, F32)


def _experts(blk_e, n_used, src_tok3, u2t, wg, wu, wd, layer):
    d, de = wg.shape[-2:]
    blk = MOE_ROWS
    rt = d // LANES
    n_blocks = src_tok3.shape[0]
    wspec = lambda r, c: pl.BlockSpec((None, 1, r, c), lambda i, be, nu: (layer, be[i], 0, 0))
    ispec = lambda f: pl.BlockSpec((1, 1, blk), lambda i, be, nu: (f(i, nu), 0, 0), memory_space=pltpu.SMEM)
    return pl.pallas_call(
        _expert_kernel,
        out_shape=jax.ShapeDtypeStruct((n_blocks * blk * rt, LANES), F32),
        grid_spec=pltpu.PrefetchScalarGridSpec(
            num_scalar_prefetch=2,
            grid=(n_blocks,),
            in_specs=[ispec(lambda i, nu: jnp.minimum(i, nu[0] - 1)),
                      ispec(lambda i, nu: jnp.minimum(i + 1, nu[0] - 1)),
                      pl.BlockSpec(memory_space=pl.ANY),
                      wspec(d, de), wspec(d, de), wspec(de, d)],
            out_specs=pl.BlockSpec((blk * rt, LANES), lambda i, be, nu: (i, 0)),
            scratch_shapes=[pltpu.VMEM((blk * rt, LANES), F32), pltpu.VMEM((blk * rt, LANES), F32),
                            pltpu.VMEM((d, de), BF16), pltpu.VMEM((d, de), BF16),
                            pltpu.VMEM((de, d), BF16),
                            pltpu.SemaphoreType.DMA((2,))]),
        compiler_params=_cparams(("arbitrary",)),
    )(blk_e, n_used, src_tok3, src_tok3, u2t, wg, wu, wd)


def _moe_plan(ri, cnt, n, tm):
    blk = MOE_ROWS
    n_rows = 2 * n + N_EXPERTS * blk
    n_blocks = n_rows // blk
    counts = cnt[0, N_GROUPS:N_GROUPS + N_EXPERTS].astype(jnp.int32)
    padded = ((counts + blk - 1) // blk) * blk
    pends = jnp.cumsum(padded)
    pstarts = pends - padded
    pos = jnp.take(pstarts, ri[:, 0:2], axis=0) + ri[:, 2:4]
    blk_start = jnp.arange(n_blocks, dtype=jnp.int32) * blk
    blk_e = jnp.minimum(jnp.sum(blk_start[:, None] >= pends[None, :], axis=1), N_EXPERTS - 1).astype(jnp.int32)
    n_used = (pends[-1:] // blk).astype(jnp.int32)
    tok2 = jnp.tile(jnp.arange(n, dtype=jnp.int32), 2)
    src_tok = jnp.zeros((n_rows,), jnp.int32).at[pos.T.reshape(-1)].set(tok2, unique_indices=True)
    pos3 = pos.reshape(n // tm, tm, 2).transpose(0, 2, 1)
    return blk_e, n_used, src_tok.reshape(n_blocks, 1, blk), pos3


def kernel(x, c, w_ada, b_ada, w_in, b_gates, w_alpha, b_alpha, lambda_q1, lambda_k1, lambda_q2, lambda_k2, diff_norm_g, gla_norm_g, w_branch_a, w_branch_b, w_out, ln1_g, ln1_b, w_router_g, b_router_g, w_router_e, b_router_e, w_gate_e, w_up_e, w_down_e, ln2_g, ln2_b):
    b, s, d = x.shape
    depth = w_ada.shape[0]
    n = b * s
    alpha = (2.0 * depth) ** 0.25

    ada = _ada(c, w_ada, b_ada)
    x2 = x.reshape(n, d)
    tm_rows = 512
    moe = None
    for l in range(depth):
        sh1, sc1, g1, sh2, sc2, g2 = [ada[l, :, i * d:(i + 1) * d].reshape(b, 1, d) for i in range(6)]
        wl = w_in[l]
        w_main = jnp.concatenate([wl[:, :6144], wl[:, 6144 + GLA_RANK:]], axis=1).astype(BF16)
        w_ab = jnp.pad(wl[:, 6144:6144 + GLA_RANK], ((0, 0), (0, LANES - GLA_RANK))).astype(BF16)
        wal = jnp.pad(w_alpha[l], ((0, LANES - GLA_RANK), (0, 0))).astype(BF16)
        bal = b_alpha[l].reshape(1, -1)
        if moe is None:
            p, la = _inproj(x2, sh1, sc1, w_main, w_ab, wal, bal, s)
        else:
            x2, p, la = _combine_inproj(*moe, sh1, sc1, w_main, w_ab, wal, bal, s, alpha, tm=tm_rows)
        p3 = p.reshape(b, s, W_MAIN)

        lam_init = 0.8 - 0.6 * math.exp(-0.3 * l)
        oa = _attn(p3, lambda_q1[l].reshape(1, -1), lambda_k1[l].reshape(1, -1),
                   lambda_q2[l].reshape(1, -1), lambda_k2[l].reshape(1, -1),
                   diff_norm_g[l].reshape(1, -1), lam_init)
        ob = _gla(p3, la.reshape(b, s, -1), gla_norm_g[l].reshape(1, -1))

        wr = jnp.pad(jnp.concatenate([w_router_g[l], w_router_e[l]], axis=1),
                     ((0, 0), (0, LANES - N_GROUPS - N_EXPERTS)))
        wrh = wr.astype(BF16)
        wrl = (wr - wrh.astype(F32)).astype(BF16)
        br = jnp.pad(jnp.concatenate([b_router_g[l], b_router_e[l]]),
                     (0, LANES - N_GROUPS - N_EXPERTS)).reshape(1, LANES)
        x1, u2, lg = _mix(oa.reshape(n, d), ob.reshape(n, d), p, x2, b_gates[l].reshape(1, -1),
                          g1, sh2, sc2, ln1_g[l].reshape(1, d), ln1_b[l].reshape(1, d),
                          w_branch_a[l].astype(BF16), w_branch_b[l].astype(BF16), w_out[l].astype(BF16),
                          wrh, wrl, br, s, alpha)

        ri, rw, cnt = _route(lg)
        blk_e, n_used, src_tok3, pos3 = _moe_plan(ri, cnt, n, tm_rows)
        y_sorted = _experts(blk_e, n_used, src_tok3, u2, w_gate_e, w_up_e, w_down_e, l)
        moe = (pos3, x1, y_sorted, rw, g2, ln2_g[l].reshape(1, d), ln2_b[l].reshape(1, d))
    x2 = _combine_gather(*moe, s, alpha, tm=tm_rows)
    return x2.reshape(b, s, d)
```

```python
import functools
import math

import jax
import jax.numpy as jnp
import numpy as np
from jax import lax
from jax.experimental import pallas as pl
from jax.experimental.pallas import tpu as pltpu

F32 = jnp.float32
BF16 = jnp.bfloat16

N_HEADS_DIFF = 8
HEAD_DIM_DIFF = 64
N_HEADS_GLA = 4
KEY_DIM_GLA = 128
VAL_DIM_GLA = 256
GLA_RANK = 16
GLA_TAU = 16.0
GLA_CHUNK = 64
N_GROUPS = 4
EXPERTS_PER_GROUP = 8
N_EXPERTS = N_GROUPS * EXPERTS_PER_GROUP
LN_EPS = 1e-5
LANES = 128
NEG_BIG = -1e30

OFF_QA, OFF_KA, OFF_VA = 0, 1024, 2048
OFF_QB, OFF_KB, OFF_VB, OFF_GB, OFF_GATES = 3072, 3584, 4096, 5120, 6144
W_MAIN = 8192

VMEM_LIMIT = 56 * 1024 * 1024


def _cparams(sem, **kw):
    return pltpu.CompilerParams(dimension_semantics=sem, vmem_limit_bytes=VMEM_LIMIT, **kw)


def _standardize(x):
    mu = jnp.mean(x, axis=-1, keepdims=True)
    xc = x - mu
    var = jnp.mean(xc * xc, axis=-1, keepdims=True)
    return xc * lax.rsqrt(var + LN_EPS)


def _sigmoid(x):
    return 1.0 / (1.0 + jnp.exp(-x))


ROW_TILES = 8


def _store_token_tiles(ref, x):
    t = x.shape[0]
    for s in range(ROW_TILES):
        ref[pl.ds(s, t, stride=ROW_TILES), :] = x[:, s * LANES:(s + 1) * LANES]


def _load_token_tiles(ref):
    t = ref.shape[0] // ROW_TILES
    return jnp.concatenate([ref[pl.ds(s, t, stride=ROW_TILES), :] for s in range(ROW_TILES)], axis=1)


def _ada_kernel(c_ref, w_ref, b_ref, o_ref):
    c = c_ref[...]
    cond = c * _sigmoid(c)
    o_ref[0] = jnp.dot(cond, w_ref[0], preferred_element_type=F32,
                       precision=lax.Precision.HIGHEST) + b_ref[0]


def _ada(c, w_ada, b_ada):
    depth, d, d6 = w_ada.shape
    b = c.shape[0]
    return pl.pallas_call(
        _ada_kernel,
        out_shape=jax.ShapeDtypeStruct((depth, b, d6), F32),
        grid=(depth, d6 // d),
        in_specs=[pl.BlockSpec((b, d), lambda l, j: (0, 0)),
                  pl.BlockSpec((1, d, d), lambda l, j: (l, 0, j)),
                  pl.BlockSpec((1, 1, d), lambda l, j: (l, 0, j))],
        out_specs=pl.BlockSpec((1, b, d), lambda l, j: (l, 0, j)),
        compiler_params=_cparams(("arbitrary", "arbitrary")),
    )(c, w_ada, b_ada.reshape(depth, 1, d6))


def _prep_tile(x, sh_ref, sc_ref, wab_ref, wal_ref, bal_ref, la_ref, u_scr):
    u = _standardize(x) * (1.0 + sc_ref[0]) + sh_ref[0]
    ub = u.astype(BF16)
    u_scr[...] = ub
    ab = jnp.dot(ub, wab_ref[...], preferred_element_type=F32)
    pre = jnp.dot(ab.astype(BF16), wal_ref[...], preferred_element_type=F32) + bal_ref[...]
    la_ref[...] = (jnp.minimum(pre, 0.0) - jnp.log(1.0 + jnp.exp(-jnp.abs(pre)))) * (1.0 / GLA_TAU)


def _inproj_kernel(x_ref, sh_ref, sc_ref, w_ref, wab_ref, wal_ref, bal_ref,
                   p_ref, la_ref, u_scr):
    @pl.when(pl.program_id(1) == 0)
    def _():
        _prep_tile(x_ref[...], sh_ref, sc_ref, wab_ref, wal_ref, bal_ref, la_ref, u_scr)

    p_ref[...] = jnp.dot(u_scr[...], w_ref[...], preferred_element_type=F32).astype(BF16)


def _inproj(x2, sh, sc, w_main, w_ab, w_alpha, b_alpha, seq, tm=1024, tn=1024):
    n, d = x2.shape
    tpb = seq // tm
    wq = w_alpha.shape[1]
    return pl.pallas_call(
        _inproj_kernel,
        out_shape=(jax.ShapeDtypeStruct((n, W_MAIN), BF16),
                   jax.ShapeDtypeStruct((n, wq), F32)),
        grid=(n // tm, W_MAIN // tn),
        in_specs=[pl.BlockSpec((tm, d), lambda i, j: (i, 0)),
                  pl.BlockSpec((1, 1, d), lambda i, j: (i // tpb, 0, 0)),
                  pl.BlockSpec((1, 1, d), lambda i, j: (i // tpb, 0, 0)),
                  pl.BlockSpec((d, tn), lambda i, j: (0, j)),
                  pl.BlockSpec((d, LANES), lambda i, j: (0, 0)),
                  pl.BlockSpec((LANES, wq), lambda i, j: (0, 0)),
                  pl.BlockSpec((1, wq), lambda i, j: (0, 0))],
        out_specs=(pl.BlockSpec((tm, tn), lambda i, j: (i, j)),
                   pl.BlockSpec((tm, wq), lambda i, j: (i, 0))),
        scratch_shapes=[pltpu.VMEM((tm, d), BF16)],
        compiler_params=_cparams(("arbitrary", "arbitrary")),
    )(x2, sh, sc, w_main, w_ab, w_alpha, b_alpha)


def _row_tile(ref, t):
    return ref.at[pl.ds(pl.multiple_of(t * ROW_TILES, ROW_TILES), ROW_TILES)]


def _combine_tile(x1_ref, ya, yb, rw_ref, g2_ref, lng_ref, lnb_ref, alpha):
    rw = rw_ref[...]
    y = rw[:, 0:1] * _load_token_tiles(ya) + rw[:, 1:2] * _load_token_tiles(yb)
    return _standardize(alpha * x1_ref[...] + g2_ref[0] * y) * lng_ref[...] + lnb_ref[...]


def _combine_inproj_kernel(pos_a_ref, pos_b_ref, x1_ref, rw_ref, g2_ref, lng_ref, lnb_ref, sh_ref, sc_ref,
                           y_ref, w_ref, wab_ref, wal_ref, bal_ref,
                           x2_ref, p_ref, la_ref, ya0, yb0, ya1, yb1, u_scr, sem, *, alpha):
    i, j = pl.program_id(0), pl.program_id(1)
    n_i, n_j = pl.num_programs(0), pl.num_programs(1)
    tm = x1_ref.shape[0]
    per_step = tm // W_MAIN_STEPS
    bufs = ((ya0, yb0), (ya1, yb1))

    def issue(pos_ref, slot, t):
        for s in range(2):
            pltpu.make_async_copy(_row_tile(y_ref, pos_ref[0, s, t]), _row_tile(bufs[slot][s], t),
                                  sem.at[slot]).start()

    def wait_slot(slot):
        for s in range(2):
            pltpu.make_async_copy(y_ref.at[pl.ds(0, tm * ROW_TILES)], bufs[slot][s], sem.at[slot]).wait()

    for p in range(2):
        @pl.when(i % 2 == p)
        def _():
            @pl.when(j == 0)
            def _():
                if p == 0:
                    @pl.when(i == 0)
                    def _():
                        def first(t, carry):
                            issue(pos_a_ref, 0, t)
                            return carry
                        lax.fori_loop(0, tm, first, 0)
                wait_slot(p)
                x2 = _combine_tile(x1_ref, bufs[p][0], bufs[p][1], rw_ref, g2_ref, lng_ref, lnb_ref, alpha)
                x2_ref[...] = x2
                _prep_tile(x2, sh_ref, sc_ref, wab_ref, wal_ref, bal_ref, la_ref, u_scr)

            for r in range(per_step):
                issue(pos_b_ref, 1 - p, j * per_step + r)
            p_ref[...] = jnp.dot(u_scr[...], w_ref[...], preferred_element_type=F32).astype(BF16)

            @pl.when((i == n_i - 1) & (j == n_j - 1))
            def _():
                wait_slot(1 - p)


W_MAIN_STEPS = 8


def _combine_inproj(pos3, x1, y_sorted, rw, g2, lng, lnb, sh, sc, w_main, w_ab, w_alpha, b_alpha,
                    seq, alpha, tm=512):
    n, d = x1.shape
    tn = W_MAIN // W_MAIN_STEPS
    tpb = seq // tm
    n_t = n // tm
    wq = w_alpha.shape[1]
    const = lambda i, j: (0, 0)
    per_b = pl.BlockSpec((1, 1, d), lambda i, j: (i // tpb, 0, 0))
    ybuf = pltpu.VMEM((tm * ROW_TILES, LANES), F32)
    return pl.pallas_call(
        functools.partial(_combine_inproj_kernel, alpha=alpha),
        out_shape=(jax.ShapeDtypeStruct((n, d), F32),
                   jax.ShapeDtypeStruct((n, W_MAIN), BF16),
                   jax.ShapeDtypeStruct((n, wq), F32)),
        grid=(n_t, W_MAIN_STEPS),
        in_specs=[pl.BlockSpec((1, 2, tm), lambda i, j: (0, 0, 0), memory_space=pltpu.SMEM),
                  pl.BlockSpec((1, 2, tm), lambda i, j: (jnp.minimum(i + 1, n_t - 1), 0, 0),
                               memory_space=pltpu.SMEM),
                  pl.BlockSpec((tm, d), lambda i, j: (i, 0)),
                  pl.BlockSpec((tm, LANES), lambda i, j: (i, 0)),
                  per_b, pl.BlockSpec((1, d), const), pl.BlockSpec((1, d), const), per_b, per_b,
                  pl.BlockSpec(memory_space=pl.ANY),
                  pl.BlockSpec((d, tn), lambda i, j: (0, j)),
                  pl.BlockSpec((d, LANES), const),
                  pl.BlockSpec((LANES, wq), const),
                  pl.BlockSpec((1, wq), const)],
        out_specs=(pl.BlockSpec((tm, d), lambda i, j: (i, 0)),
                   pl.BlockSpec((tm, tn), lambda i, j: (i, j)),
                   pl.BlockSpec((tm, wq), lambda i, j: (i, 0))),
        scratch_shapes=[ybuf, ybuf, ybuf, ybuf, pltpu.VMEM((tm, d), BF16), pltpu.SemaphoreType.DMA((2,))],
        compiler_params=_cparams(("arbitrary", "arbitrary")),
    )(pos3, pos3, x1, rw, g2, lng, lnb, sh, sc, y_sorted, w_main, w_ab, w_alpha, b_alpha)


def _combine_gather_kernel(pos_ref, x1_ref, rw_ref, g2_ref, lng_ref, lnb_ref, y_ref, o_ref, ya, yb, sem, *, alpha):
    tm = x1_ref.shape[0]

    def issue(t, carry):
        for s, buf in enumerate((ya, yb)):
            pltpu.make_async_copy(_row_tile(y_ref, pos_ref[0, s, t]), _row_tile(buf, t), sem).start()
        return carry

    lax.fori_loop(0, tm, issue, 0)
    for buf in (ya, yb):
        pltpu.make_async_copy(y_ref.at[pl.ds(0, tm * ROW_TILES)], buf, sem).wait()
    o_ref[...] = _combine_tile(x1_ref, ya, yb, rw_ref, g2_ref, lng_ref, lnb_ref, alpha)


def _combine_gather(pos3, x1, y_sorted, rw, g2, lng, lnb, seq, alpha, tm=512):
    n, d = x1.shape
    tpb = seq // tm
    const = lambda i: (0, 0)
    ybuf = pltpu.VMEM((tm * ROW_TILES, LANES), F32)
    return pl.pallas_call(
        functools.partial(_combine_gather_kernel, alpha=alpha),
        out_shape=jax.ShapeDtypeStruct((n, d), F32),
        grid=(n // tm,),
        in_specs=[pl.BlockSpec((1, 2, tm), lambda i: (i, 0, 0), memory_space=pltpu.SMEM),
                  pl.BlockSpec((tm, d), lambda i: (i, 0)),
                  pl.BlockSpec((tm, LANES), lambda i: (i, 0)),
                  pl.BlockSpec((1, 1, d), lambda i: (i // tpb, 0, 0)),
                  pl.BlockSpec((1, d), const), pl.BlockSpec((1, d), const),
                  pl.BlockSpec(memory_space=pl.ANY)],
        out_specs=pl.BlockSpec((tm, d), lambda i: (i, 0)),
        scratch_shapes=[ybuf, ybuf, pltpu.SemaphoreType.DMA(())],
        compiler_params=_cparams(("arbitrary",)),
    )(pos3, x1, rw, g2, lng, lnb, y_sorted)


ONES_ROWS = 16


QUERY_GROUP = 256
POS_SPLIT = 32


def _attn_kernel(slopes_ref, q_ref, k_ref, v_ref, lq1_ref, lk1_ref, lq2_ref, lk2_ref, g_ref,
                 o_ref, kaug_scr, vt_scr, dmask_scr, sa_scr, sb_scr, acc_scr, m_scr, *, tk, lam_init):
    h = pl.program_id(1)
    qi = pl.program_id(2)
    slope = slopes_ref[h]
    dh = HEAD_DIM_DIFF
    dv = LANES
    n_chunks = vt_scr.shape[0]
    halves = q_ref.shape[0] // tk

    @pl.when(qi == 0)
    def _():
        for j in range(n_chunks):
            vt_scr[j, 0:dv, :] = v_ref[j * tk:(j + 1) * tk, :].astype(F32).T.astype(BF16)
            vt_scr[j, dv:dv + ONES_ROWS, :] = jnp.ones((ONES_ROWS, tk), BF16)
        kaug_scr[:, 0:LANES] = k_ref[...]
        koff = lax.broadcasted_iota(jnp.int32, (tk, LANES), 0)
        flane = lax.broadcasted_iota(jnp.int32, (tk, LANES), 1)
        feat = jnp.where(flane == 0, koff // POS_SPLIT, jnp.where(flane == 1, koff % POS_SPLIT, 0))
        feat = feat.astype(F32).astype(BF16)
        for j in range(n_chunks):
            kaug_scr[j * tk:(j + 1) * tk, LANES:2 * LANES] = feat
        kpos = lax.broadcasted_iota(jnp.int32, (tk, 2 * tk), 0)
        qpos = lax.broadcasted_iota(jnp.int32, (tk, 2 * tk), 1)
        qoff = jnp.where(qpos >= tk, qpos - tk, qpos)
        dmask_scr[...] = jnp.where(qoff >= kpos, 0.0, NEG_BIG)

    q = q_ref[...] * jnp.asarray(dh ** -0.5, BF16)
    lane = lax.broadcasted_iota(jnp.int32, q.shape, 1)
    zero = jnp.zeros_like(q)
    qf = jnp.where(lane == 0, slope * POS_SPLIT, jnp.where(lane == 1, slope, 0.0)).astype(BF16)
    q1 = jnp.concatenate([jnp.where(lane < dh, q, zero), qf], axis=1)
    q2 = jnp.concatenate([jnp.where(lane >= dh, q, zero), qf], axis=1)
    qq = jnp.concatenate([part[hq * tk:(hq + 1) * tk] for hq in range(halves) for part in (q1, q2)], axis=0)

    m_scr[...] = jnp.full(m_scr.shape, NEG_BIG, F32)
    acc_scr[...] = jnp.zeros(acc_scr.shape, F32)

    per_half = 2 * tk // QUERY_GROUP
    groups = [(hq, slice((hq * per_half + g) * QUERY_GROUP, (hq * per_half + g + 1) * QUERY_GROUP),
               slice(g * QUERY_GROUP, (g + 1) * QUERY_GROUP))
              for hq in range(halves) for g in range(per_half)]

    def score(kaug, ls):
        return lax.dot_general(kaug, qq[ls], (((1,), (1,)), ((), ())), preferred_element_type=F32)

    def step(j_cur, src, j_next, dst, first_half, masked_half, next_first_half=0):
        if dst is not None:
            kaug = kaug_scr[pl.ds(pl.multiple_of(j_next * tk, tk), tk), :]
        vt = vt_scr[j_cur]
        m_all = m_scr[...]
        acc_all = acc_scr[...]
        done = []
        for hq, ls, ms in groups:
            if dst is not None and hq >= next_first_half:
                dst[:, ls] = score(kaug, ls)
            if hq < first_half:
                continue
            cj = (-slope) * ((halves * qi + hq - j_cur) * tk).astype(F32)
            s = src[:, ls]
            if hq == masked_half:
                s = s + dmask_scr[:, ms]
            m_prev = m_all[:, ls]
            m_new = jnp.maximum(m_prev, jnp.max(s, axis=0, keepdims=True) + cj)
            p = jnp.exp(s - (m_new - cj))
            alpha = jnp.exp(m_prev - m_new)
            done.append((ls, m_new, alpha * acc_all[:, ls] + jnp.dot(vt, p.astype(BF16),
                                                                     preferred_element_type=F32)))
        for ls, m_new, acc_new in done:
            m_scr[:, ls] = m_new
            acc_scr[:, ls] = acc_new

    kaug0 = kaug_scr[0:tk, :]
    for _, ls, _ in groups:
        sa_scr[:, ls] = score(kaug0, ls)

    def pair(jj, carry):
        j = 2 * jj
        step(j, sa_scr, j + 1, sb_scr, 0, None)
        step(j + 1, sb_scr, j + 2, sa_scr, 0, None)
        return carry

    lax.fori_loop(0, qi, pair, 0)
    step(2 * qi, sa_scr, 2 * qi + 1, sb_scr, 0, 0, next_first_half=1)
    step(2 * qi + 1, sb_scr, None, None, 1, 1)

    lam = (jnp.exp(jnp.sum(lq1_ref[...] * lk1_ref[...], axis=-1, keepdims=True))
           - jnp.exp(jnp.sum(lq2_ref[...] * lk2_ref[...], axis=-1, keepdims=True)) + lam_init)
    acc = acc_scr[...]
    ot = acc[0:dv] / acc[dv:dv + 1]
    for hq in range(halves):
        base = hq * 2 * tk
        o = (ot[:, base:base + tk] - lam * ot[:, base + tk:base + 2 * tk]).T
        o = o * lax.rsqrt(jnp.mean(o * o, axis=-1, keepdims=True) + LN_EPS) * g_ref[...] * (1.0 - lam_init)
        o_ref[hq * tk:(hq + 1) * tk, :] = o.astype(o_ref.dtype)


def _attn(p3, lq1, lk1, lq2, lk2, g, lam_init, tk=512):
    b, s, _ = p3.shape
    h = N_HEADS_DIFF
    tq = 2 * tk
    slopes = jnp.asarray(2.0 ** (-8.0 * np.arange(1, h + 1) / h), dtype=F32)
    vec = pl.BlockSpec((1, HEAD_DIM_DIFF), lambda bi, hi, qi, sl: (0, 0))
    return pl.pallas_call(
        functools.partial(_attn_kernel, tk=tk, lam_init=lam_init),
        out_shape=jax.ShapeDtypeStruct((b, s, h * LANES), BF16),
        grid_spec=pltpu.PrefetchScalarGridSpec(
            num_scalar_prefetch=1,
            grid=(b, h, s // tq),
            in_specs=[pl.BlockSpec((None, tq, LANES), lambda bi, hi, qi, sl: (bi, qi, OFF_QA // LANES + hi)),
                      pl.BlockSpec((None, s, LANES), lambda bi, hi, qi, sl: (bi, 0, OFF_KA // LANES + hi)),
                      pl.BlockSpec((None, s, LANES), lambda bi, hi, qi, sl: (bi, 0, OFF_VA // LANES + hi)),
                      vec, vec, vec, vec,
                      pl.BlockSpec((1, LANES), lambda bi, hi, qi, sl: (0, 0))],
            out_specs=pl.BlockSpec((None, tq, LANES), lambda bi, hi, qi, sl: (bi, qi, hi)),
            scratch_shapes=[pltpu.VMEM((s, 2 * LANES), BF16),
                            pltpu.VMEM((s // tk, LANES + ONES_ROWS, tk), BF16),
                            pltpu.VMEM((tk, 2 * tk), F32),
                            pltpu.VMEM((tk, 2 * tq), F32),
                            pltpu.VMEM((tk, 2 * tq), F32),
                            pltpu.VMEM((LANES + ONES_ROWS, 2 * tq), F32),
                            pltpu.VMEM((1, 2 * tq), F32)]),
        compiler_params=_cparams(("arbitrary", "arbitrary", "arbitrary")),
    )(slopes, p3, p3, p3, lq1, lk1, lq2, lk2, g)


GLA_HEADS_PER_STEP = 2


def _gla_kernel(q_ref, k_ref, v_ref, gb_ref, la_ref, g_ref, o_ref, state_scr, *, tt):
    c = GLA_CHUNK
    dk, dv = KEY_DIM_GLA, VAL_DIM_GLA
    hp = state_scr.shape[0]
    n_c = tt // c
    work = [(hh, ci) for hh in range(hp) for ci in range(n_c)]

    @pl.when(pl.program_id(2) == 0)
    def _():
        state_scr[...] = jnp.zeros(state_scr.shape, F32)

    rr = lax.broadcasted_iota(jnp.int32, (c, c), 0)
    cr = lax.broadcasted_iota(jnp.int32, (c, c), 1)
    causal = cr <= rr
    tri = jnp.where(causal, 1.0, 0.0).astype(BF16)
    qscale = dk ** -0.5
    rows = lambda ci: slice(ci * c, (ci + 1) * c)
    kcols = lambda hh: slice(hh * dk, (hh + 1) * dk)
    vcols = lambda hh: slice(hh * dv, (hh + 1) * dv)

    cum = {}
    for hh, ci in work:
        la = la_ref[rows(ci), kcols(hh)]
        la_hi = la.astype(BF16)
        la_lo = (la - la_hi.astype(F32)).astype(BF16)
        cum2 = jnp.dot(tri, jnp.concatenate([la_hi, la_lo], axis=1), preferred_element_type=F32)
        cum[hh, ci] = cum2[:, :dk] + cum2[:, dk:]

    qe, q2, k2, kd, dec, v = {}, {}, {}, {}, {}, {}
    for hh, ci in work:
        cum_c = cum[hh, ci]
        last = cum_c[c - 1:c]
        mid = cum_c[c // 2:c // 2 + 1]
        q_c = q_ref[rows(ci), kcols(hh)].astype(F32) * qscale
        k_c = k_ref[rows(ci), kcols(hh)].astype(F32)
        qe[hh, ci] = (q_c * jnp.exp(cum_c)).astype(BF16)
        q2[hh, ci] = (q_c * jnp.exp(cum_c - mid)).astype(BF16)
        k2[hh, ci] = (k_c * jnp.exp(mid - cum_c)).astype(BF16)
        kd[hh, ci] = (k_c * jnp.exp(last - cum_c)).astype(BF16)
        dec[hh, ci] = jnp.exp(last)
        v[hh, ci] = v_ref[rows(ci), vcols(hh)]

    att = {w: lax.dot_general(q2[w], k2[w], (((1,), (1,)), ((), ())), preferred_element_type=F32)
           for w in work}
    o_intra = {w: jnp.dot(jnp.where(causal, att[w], 0.0).astype(BF16), v[w], preferred_element_type=F32)
               for w in work}
    kv = {w: jnp.dot(v[w].astype(F32).T.astype(BF16), kd[w], preferred_element_type=F32) for w in work}

    o_inter = {}
    for hh in range(hp):
        st = state_scr[hh]
        for ci in range(n_c):
            o_inter[hh, ci] = lax.dot_general(qe[hh, ci], st.astype(BF16), (((1,), (1,)), ((), ())),
                                              preferred_element_type=F32)
            st = st * dec[hh, ci] + kv[hh, ci]
        state_scr[hh] = st

    for hh, ci in work:
        o = o_inter[hh, ci] + o_intra[hh, ci]
        o = o * lax.rsqrt(jnp.mean(o * o, axis=-1, keepdims=True) + LN_EPS) * g_ref[...]
        gate = gb_ref[rows(ci), vcols(hh)].astype(F32)
        o_ref[rows(ci), vcols(hh)] = (o * (gate * _sigmoid(gate))).astype(o_ref.dtype)


def _gla(p3, la3, g, tt=512):
    b, s, _ = p3.shape
    hp = GLA_HEADS_PER_STEP
    dk, dv = KEY_DIM_GLA * hp, VAL_DIM_GLA * hp
    return pl.pallas_call(
        functools.partial(_gla_kernel, tt=tt),
        out_shape=jax.ShapeDtypeStruct((b, s, N_HEADS_GLA * VAL_DIM_GLA), BF16),
        grid=(b, N_HEADS_GLA // hp, s // tt),
        in_specs=[pl.BlockSpec((None, tt, dk), lambda bi, hi, ti: (bi, ti, OFF_QB // dk + hi)),
                  pl.BlockSpec((None, tt, dk), lambda bi, hi, ti: (bi, ti, OFF_KB // dk + hi)),
                  pl.BlockSpec((None, tt, dv), lambda bi, hi, ti: (bi, ti, OFF_VB // dv + hi)),
                  pl.BlockSpec((None, tt, dv), lambda bi, hi, ti: (bi, ti, OFF_GB // dv + hi)),
                  pl.BlockSpec((None, tt, dk), lambda bi, hi, ti: (bi, ti, hi)),
                  pl.BlockSpec((1, VAL_DIM_GLA), lambda bi, hi, ti: (0, 0))],
        out_specs=pl.BlockSpec((None, tt, dv), lambda bi, hi, ti: (bi, ti, hi)),
        scratch_shapes=[pltpu.VMEM((hp, VAL_DIM_GLA, KEY_DIM_GLA), F32)],
        compiler_params=_cparams(("arbitrary", "arbitrary", "arbitrary")),
    )(p3, p3, p3, p3, la3, g)


def _mix_kernel(oa_ref, ob_ref, gt_ref, x_ref, bg_ref, g1_ref, sh2_ref, sc2_ref, lng_ref, lnb_ref,
                wba_ref, wbb_ref, wo_ref, wrh_ref, wrl_ref, br_ref,
                x1_ref, u2_ref, lg_ref, *, alpha):
    d = x_ref.shape[-1]
    a = jnp.dot(oa_ref[...], wba_ref[...], preferred_element_type=F32)
    bm = jnp.dot(ob_ref[...], wbb_ref[...], preferred_element_type=F32)
    gates = _sigmoid(gt_ref[...].astype(F32) + bg_ref[...])
    mixed = gates[:, :d] * a + gates[:, d:] * bm
    y = jnp.dot(mixed.astype(BF16), wo_ref[...], preferred_element_type=F32)
    x1 = _standardize(alpha * x_ref[...] + g1_ref[0] * y) * lng_ref[...] + lnb_ref[...]
    x1_ref[...] = x1
    u2 = _standardize(x1) * (1.0 + sc2_ref[0]) + sh2_ref[0]
    _store_token_tiles(u2_ref, u2)
    uh = u2.astype(BF16)
    ul = (u2 - uh.astype(F32)).astype(BF16)
    lg_ref[...] = (jnp.dot(uh, wrh_ref[...], preferred_element_type=F32)
                   + jnp.dot(uh, wrl_ref[...], preferred_element_type=F32)
                   + jnp.dot(ul, wrh_ref[...], preferred_element_type=F32) + br_ref[...])


def _mix(oa, ob, p, x2, bg, g1, sh2, sc2, lng, lnb, wba, wbb, wo, wrh, wrl, br, seq, alpha, tm=512):
    n, d = x2.shape
    tpb = seq // tm
    row = lambda i: (i, 0)
    const = lambda i: (0, 0)
    per_b = pl.BlockSpec((1, 1, d), lambda i: (i // tpb, 0, 0))
    return pl.pallas_call(
        functools.partial(_mix_kernel, alpha=alpha),
        out_shape=(jax.ShapeDtypeStruct((n, d), F32),
                   jax.ShapeDtypeStruct((n * (d // LANES), LANES), F32),
                   jax.ShapeDtypeStruct((n, LANES), F32)),
        grid=(n // tm,),
        in_specs=[pl.BlockSpec((tm, d), row), pl.BlockSpec((tm, d), row),
                  pl.BlockSpec((tm, 2 * d), lambda i: (i, OFF_GATES // (2 * d))),
                  pl.BlockSpec((tm, d), row),
                  pl.BlockSpec((1, 2 * d), const), per_b, per_b, per_b,
                  pl.BlockSpec((1, d), const), pl.BlockSpec((1, d), const),
                  pl.BlockSpec((d, d), const), pl.BlockSpec((d, d), const), pl.BlockSpec((d, d), const),
                  pl.BlockSpec((d, LANES), const), pl.BlockSpec((d, LANES), const),
                  pl.BlockSpec((1, LANES), const)],
        out_specs=(pl.BlockSpec((tm, d), row), pl.BlockSpec((tm * (d // LANES), LANES), row),
                   pl.BlockSpec((tm, LANES), row)),
        compiler_params=_cparams(("arbitrary",)),
    )(oa, ob, p, x2, bg, g1, sh2, sc2, lng, lnb, wba, wbb, wo, wrh, wrl, br)


def _route_kernel(lg_ref, ri_ref, rw_ref, cnt_ref, *, tm):
    @pl.when(pl.program_id(0) == 0)
    def _():
        cnt_ref[...] = jnp.zeros(cnt_ref.shape, F32)

    lg = lg_ref[...]
    lane = lax.broadcasted_iota(jnp.int32, lg.shape, 1)
    lanef = lane.astype(F32)
    far = float(LANES)
    is_g = lane < N_GROUPS
    gl = jnp.where(is_g, lg, NEG_BIG)
    gmax = jnp.max(gl, axis=-1, keepdims=True)
    gidx = jnp.min(jnp.where(gl == gmax, lanef, far), axis=-1, keepdims=True)
    gw = 1.0 / jnp.sum(jnp.where(is_g, jnp.exp(gl - gmax), 0.0), axis=-1, keepdims=True)
    lo = N_GROUPS + gidx * EXPERTS_PER_GROUP
    in_g = (lanef >= lo) & (lanef < lo + EXPERTS_PER_GROUP)
    el = jnp.where(in_g, lg, NEG_BIG)
    v1 = jnp.max(el, axis=-1, keepdims=True)
    i1 = jnp.min(jnp.where(in_g & (el == v1), lanef, far), axis=-1, keepdims=True)
    in_g2 = in_g & (lanef != i1)
    el2 = jnp.where(in_g2, lg, NEG_BIG)
    v2 = jnp.max(el2, axis=-1, keepdims=True)
    i2 = jnp.min(jnp.where(in_g2 & (el2 == v2), lanef, far), axis=-1, keepdims=True)
    t = jnp.exp(v2 - v1)
    w1 = gw / (1.0 + t)
    w2 = gw * t / (1.0 + t)

    oh1 = lanef == i1
    oh2 = lanef == i2
    oh = jnp.where(oh1 | oh2, 1.0, 0.0)
    r = lax.broadcasted_iota(jnp.int32, (tm, tm), 0)
    c = lax.broadcasted_iota(jnp.int32, (tm, tm), 1)
    lower = jnp.where(c < r, 1.0, 0.0).astype(BF16)
    base = jnp.dot(lower, oh.astype(BF16), preferred_element_type=F32) + cnt_ref[0:1, :]
    r1 = jnp.sum(jnp.where(oh1, base, 0.0), axis=-1, keepdims=True)
    r2 = jnp.sum(jnp.where(oh2, base, 0.0), axis=-1, keepdims=True)
    cnt_ref[...] = cnt_ref[...] + jnp.sum(oh, axis=0, keepdims=True)

    e1 = i1 - float(N_GROUPS)
    e2 = i2 - float(N_GROUPS)
    ri = jnp.where(lane == 0, e1, jnp.where(lane == 1, e2, jnp.where(lane == 2, r1, jnp.where(lane == 3, r2, 0.0))))
    ri_ref[...] = ri.astype(jnp.int32)
    rw_ref[...] = jnp.where(lane == 0, w1, jnp.where(lane == 1, w2, 0.0))


def _route(lg, tm=512):
    n = lg.shape[0]
    row = lambda i: (i, 0)
    return pl.pallas_call(
        functools.partial(_route_kernel, tm=tm),
        out_shape=(jax.ShapeDtypeStruct((n, LANES), jnp.int32),
                   jax.ShapeDtypeStruct((n, LANES), F32),
                   jax.ShapeDtypeStruct((8, LANES), F32)),
        grid=(n // tm,),
        in_specs=[pl.BlockSpec((tm, LANES), row)],
        out_specs=(pl.BlockSpec((tm, LANES), row), pl.BlockSpec((tm, LANES), row),
                   pl.BlockSpec((8, LANES), lambda i: (0, 0))),
        compiler_params=_cparams(("arbitrary",)),
    )(lg)


MOE_ROWS = 256


def _expert_kernel(be_ref, nu_ref, tok_a_ref, tok_b_ref, dst_ref, u_ref, wg_ref, wu_ref, wd_ref, y_ref,
                   xb0, xb1, yb0, yb1, wg_scr, wu_scr, wd_scr, gsem, ssem):
    i = pl.program_id(0)
    n_used = nu_ref[0]
    blk = tok_a_ref.shape[-1]
    xbs, ybs = (xb0, xb1), (yb0, yb1)

    def tile(ref, t):
        return ref.at[pl.ds(pl.multiple_of(t * ROW_TILES, ROW_TILES), ROW_TILES)]

    def gather_issue(tok_ref, xb, sem):
        for r in range(blk):
            pltpu.async_copy(tile(u_ref, tok_ref[0, 0, r]), tile(xb, r), sem, priority=r % 2)

    def gather_wait(xb, sem):
        pltpu.make_async_copy(u_ref.at[pl.ds(0, blk * ROW_TILES)], xb, sem).wait()

    def scatter_issue(yb, sem):
        for r in range(blk):
            pltpu.make_async_copy(tile(yb, r), tile(y_ref, dst_ref[0, 0, r]), sem).start()

    def scatter_wait(yb, sem):
        pltpu.make_async_copy(yb, y_ref.at[pl.ds(0, blk * ROW_TILES)], sem).wait()

    def load_weights():
        @pl.when((i == 0) | (be_ref[i] != be_ref[jnp.maximum(i - 1, 0)]))
        def _():
            wg_scr[...] = wg_ref[0].astype(BF16)
            wu_scr[...] = wu_ref[0].astype(BF16)
            wd_scr[...] = wd_ref[0].astype(BF16)

    def compute(xb, yb):
        x = _load_token_tiles(xb).astype(BF16)
        g = jnp.dot(x, wg_scr[...], preferred_element_type=F32)
        u = jnp.dot(x, wu_scr[...], preferred_element_type=F32)
        hid = (g * _sigmoid(g) * u).astype(BF16)
        _store_token_tiles(yb, jnp.dot(hid, wd_scr[...], preferred_element_type=F32))

    @pl.when(i == 0)
    def _():
        gather_issue(tok_a_ref, xb0, gsem.at[0])
        gather_wait(xb0, gsem.at[0])
        load_weights()
        gather_issue(tok_b_ref, xb1, gsem.at[1])
        compute(xb0, yb0)

    for p in range(2):
        cur, oth = p, 1 - p

        @pl.when((i >= 1) & (i < n_used) & (i % 2 == p))
        def _():
            gather_wait(xbs[cur], gsem.at[cur])

            @pl.when(i >= 2)
            def _():
                scatter_wait(ybs[cur], ssem.at[cur])

            load_weights()
            gather_issue(tok_b_ref, xbs[oth], gsem.at[oth])
            scatter_issue(ybs[oth], ssem.at[oth])
            compute(xbs[cur], ybs[cur])

        @pl.when((i == n_used) & (i % 2 == p))
        def _():
            gather_wait(xbs[cur], gsem.at[cur])

            @pl.when(i >= 2)
            def _():
                scatter_wait(ybs[cur], ssem.at[cur])

            scatter_issue(ybs[oth], ssem.at[oth])

        @pl.when((i == n_used + 1) & (i % 2 == p))
        def _():
            scatter_wait(ybs[cur], ssem.at[cur])


def _experts(blk_e, n_used, src_tok3, dst_row3, u2t, wg, wu, wd, layer, n_rows):
    d, de = wg.shape[-2:]
    blk = MOE_ROWS
    rt = d // LANES
    n_steps = blk_e.shape[0]
    wspec = lambda r, c: pl.BlockSpec((None, 1, r, c), lambda i, be, nu: (layer, be[i], 0, 0))
    ispec = lambda f: pl.BlockSpec((1, 1, blk), lambda i, be, nu: (f(i, nu), 0, 0), memory_space=pltpu.SMEM)
    return pl.pallas_call(
        _expert_kernel,
        out_shape=jax.ShapeDtypeStruct((n_rows * rt, LANES), F32),
        grid_spec=pltpu.PrefetchScalarGridSpec(
            num_scalar_prefetch=2,
            grid=(n_steps,),
            in_specs=[ispec(lambda i, nu: jnp.minimum(i, nu[0] - 1)),
                      ispec(lambda i, nu: jnp.minimum(i + 1, nu[0] - 1)),
                      ispec(lambda i, nu: jnp.clip(i - 1, 0, nu[0] - 1)),
                      pl.BlockSpec(memory_space=pl.ANY),
                      wspec(d, de), wspec(d, de), wspec(de, d)],
            out_specs=pl.BlockSpec(memory_space=pl.ANY),
            scratch_shapes=[pltpu.VMEM((blk * rt, LANES), F32), pltpu.VMEM((blk * rt, LANES), F32),
                            pltpu.VMEM((blk * rt, LANES), F32), pltpu.VMEM((blk * rt, LANES), F32),
                            pltpu.VMEM((d, de), BF16), pltpu.VMEM((d, de), BF16),
                            pltpu.VMEM((de, d), BF16),
                            pltpu.SemaphoreType.DMA((2,)), pltpu.SemaphoreType.DMA((2,))]),
        compiler_params=_cparams(("arbitrary",), has_side_effects=True),
    )(blk_e, n_used, src_tok3, src_tok3, dst_row3, u2t, wg, wu, wd)


def _combine_kernel(x1_ref, y1_ref, y2_ref, rw_ref, g2_ref, lng_ref, lnb_ref, o_ref, *, alpha):
    rw = rw_ref[...]
    y = rw[:, 0:1] * _load_token_tiles(y1_ref) + rw[:, 1:2] * _load_token_tiles(y2_ref)
    o_ref[...] = _standardize(alpha * x1_ref[...] + g2_ref[0] * y) * lng_ref[...] + lnb_ref[...]


def _combine(x1, y_tok, rw, g2, lng, lnb, seq, alpha, tm=512):
    n, d = x1.shape
    tpb = seq // tm
    const = lambda i: (0, 0)
    return pl.pallas_call(
        functools.partial(_combine_kernel, alpha=alpha),
        out_shape=jax.ShapeDtypeStruct((n, d), F32),
        grid=(n // tm,),
        in_specs=[pl.BlockSpec((tm, d), lambda i: (i, 0)),
                  pl.BlockSpec((tm * (d // LANES), LANES), lambda i: (i, 0)),
                  pl.BlockSpec((tm * (d // LANES), LANES), lambda i: (n // tm + i, 0)),
                  pl.BlockSpec((tm, LANES), lambda i: (i, 0)),
                  pl.BlockSpec((1, 1, d), lambda i: (i // tpb, 0, 0)),
                  pl.BlockSpec((1, d), const), pl.BlockSpec((1, d), const)],
        out_specs=pl.BlockSpec((tm, d), lambda i: (i, 0)),
        compiler_params=_cparams(("arbitrary",)),
    )(x1, y_tok, y_tok, rw, g2, lng, lnb)


ID_SPLIT = 256


def _invert_kernel(ri_ref, ps_ref, inv_ref, *, n):
    i = pl.program_id(0)
    tm = ri_ref.shape[0]
    n_hi = inv_ref.shape[0]

    @pl.when(i == 0)
    def _():
        inv_ref[...] = jnp.zeros(inv_ref.shape, F32)

    ri = ri_ref[...].astype(F32)
    lane = lax.broadcasted_iota(jnp.int32, (tm, LANES), 1)
    lanef = lane.astype(F32)
    tok = (lax.broadcasted_iota(jnp.int32, (tm, 1), 0) + i * tm).astype(F32)
    cols = []
    onehot_lo = []
    for j in range(2):
        e = ri[:, j:j + 1]
        row = jnp.sum(jnp.where(lanef == e, ps_ref[...], 0.0), axis=-1, keepdims=True) + ri[:, 2 + j:3 + j]
        hi = jnp.floor(row * (1.0 / LANES))
        lo = row - hi * LANES
        ident = tok + float(j * n + 1)
        id_hi = jnp.floor(ident * (1.0 / ID_SPLIT))
        cols += [hi, id_hi, ident - id_hi * ID_SPLIT]
        onehot_lo.append(jnp.where(lanef == lo, 1.0, 0.0).astype(BF16))
    packed = jnp.zeros((tm, LANES), F32)
    for k, col in enumerate(cols):
        packed = jnp.where(lane == k, col, packed)
    rows = packed.T
    hsel = lax.broadcasted_iota(jnp.int32, (n_hi, tm), 0).astype(F32)
    lhs_hi, lhs_lo = [], []
    for j in range(2):
        hit = hsel == rows[3 * j:3 * j + 1]
        lhs_hi.append(jnp.where(hit, rows[3 * j + 1:3 * j + 2], 0.0).astype(BF16))
        lhs_lo.append(jnp.where(hit, rows[3 * j + 2:3 * j + 3], 0.0).astype(BF16))
    rhs = jnp.concatenate(onehot_lo, axis=0)
    inv_ref[...] += (float(ID_SPLIT) * jnp.dot(jnp.concatenate(lhs_hi, axis=1), rhs, preferred_element_type=F32)
                     + jnp.dot(jnp.concatenate(lhs_lo, axis=1), rhs, preferred_element_type=F32))


def _invert(ri, pstarts, n_rows, tm=512):
    n = ri.shape[0]
    ps = jnp.pad(pstarts.astype(F32), (0, LANES - pstarts.shape[0])).reshape(1, LANES)
    return pl.pallas_call(
        functools.partial(_invert_kernel, n=n),
        out_shape=jax.ShapeDtypeStruct((n_rows // LANES, LANES), F32),
        grid=(n // tm,),
        in_specs=[pl.BlockSpec((tm, LANES), lambda i: (i, 0)),
                  pl.BlockSpec((1, LANES), lambda i: (0, 0))],
        out_specs=pl.BlockSpec((n_rows // LANES, LANES), lambda i: (0, 0)),
        compiler_params=_cparams(("arbitrary",)),
    )(ri, ps)


def _moe_plan(ri, cnt, n):
    blk = MOE_ROWS
    n_rows = 2 * n + N_EXPERTS * blk
    n_blocks = n_rows // blk
    n_steps = n_blocks + 2
    counts = cnt[0, N_GROUPS:N_GROUPS + N_EXPERTS].astype(jnp.int32)
    padded = ((counts + blk - 1) // blk) * blk
    pends = jnp.cumsum(padded)
    pstarts = pends - padded
    cstarts = jnp.cumsum(counts) - counts
    blk_start = jnp.arange(n_steps, dtype=jnp.int32) * blk
    blk_e = jnp.minimum(jnp.sum(blk_start[:, None] >= pends[None, :], axis=1), N_EXPERTS - 1).astype(jnp.int32)
    n_used = (pends[-1:] // blk).astype(jnp.int32)
    prow = jnp.arange(n_rows, dtype=jnp.int32)
    pe = jnp.repeat(blk_e[:n_blocks], blk)
    spare = 2 * n + prow - (cstarts[pe] + jnp.minimum(prow - pstarts[pe], counts[pe]))
    inv = _invert(ri, pstarts, n_rows).reshape(-1).astype(jnp.int32)
    dst_row = jnp.where(inv > 0, inv - 1, spare)
    src_tok = dst_row % n
    return blk_e, n_used, src_tok.reshape(n_blocks, 1, blk), dst_row.reshape(n_blocks, 1, blk)


def kernel(x, c, w_ada, b_ada, w_in, b_gates, w_alpha, b_alpha, lambda_q1, lambda_k1, lambda_q2, lambda_k2, diff_norm_g, gla_norm_g, w_branch_a, w_branch_b, w_out, ln1_g, ln1_b, w_router_g, b_router_g, w_router_e, b_router_e, w_gate_e, w_up_e, w_down_e, ln2_g, ln2_b):
    b, s, d = x.shape
    depth = w_ada.shape[0]
    n = b * s
    alpha = (2.0 * depth) ** 0.25

    ada = _ada(c, w_ada, b_ada)
    x2 = x.reshape(n, d)
    for l in range(depth):
        sh1, sc1, g1, sh2, sc2, g2 = [ada[l, :, i * d:(i + 1) * d].reshape(b, 1, d) for i in range(6)]
        wl = w_in[l]
        w_main = jnp.concatenate([wl[:, :6144], wl[:, 6144 + GLA_RANK:]], axis=1).astype(BF16)
        w_ab = jnp.pad(wl[:, 6144:6144 + GLA_RANK], ((0, 0), (0, LANES - GLA_RANK))).astype(BF16)
        wal = jnp.pad(w_alpha[l], ((0, LANES - GLA_RANK), (0, 0))).astype(BF16)
        p, la = _inproj(x2, sh1, sc1, w_main, w_ab, wal, b_alpha[l].reshape(1, -1), s)
        p3 = p.reshape(b, s, W_MAIN)

        lam_init = 0.8 - 0.6 * math.exp(-0.3 * l)
        oa = _attn(p3, lambda_q1[l].reshape(1, -1), lambda_k1[l].reshape(1, -1),
                   lambda_q2[l].reshape(1, -1), lambda_k2[l].reshape(1, -1),
                   diff_norm_g[l].reshape(1, -1), lam_init)
        ob = _gla(p3, la.reshape(b, s, -1), gla_norm_g[l].reshape(1, -1))

        wr = jnp.pad(jnp.concatenate([w_router_g[l], w_router_e[l]], axis=1),
                     ((0, 0), (0, LANES - N_GROUPS - N_EXPERTS)))
        wrh = wr.astype(BF16)
        wrl = (wr - wrh.astype(F32)).astype(BF16)
        br = jnp.pad(jnp.concatenate([b_router_g[l], b_router_e[l]]),
                     (0, LANES - N_GROUPS - N_EXPERTS)).reshape(1, LANES)
        x1, u2, lg = _mix(oa.reshape(n, d), ob.reshape(n, d), p, x2, b_gates[l].reshape(1, -1),
                          g1, sh2, sc2, ln1_g[l].reshape(1, d), ln1_b[l].reshape(1, d),
                          w_branch_a[l].astype(BF16), w_branch_b[l].astype(BF16), w_out[l].astype(BF16),
                          wrh, wrl, br, s, alpha)

        ri, rw, cnt = _route(lg)
        blk_e, n_used, src_tok3, dst_row3 = _moe_plan(ri, cnt, n)
        y_tok = _experts(blk_e, n_used, src_tok3, dst_row3, u2, w_gate_e, w_up_e, w_down_e, l,
                         2 * n + N_EXPERTS * MOE_ROWS)
        x2 = _combine(x1, y_tok, rw, g2, ln2_g[l].reshape(1, d), ln2_b[l].reshape(1, d), s, alpha)
    return x2.reshape(b, s, d)
```

```python
import functools
import math

import jax
import jax.numpy as jnp
import numpy as np
from jax import lax
from jax.experimental import pallas as pl
from jax.experimental.pallas import tpu as pltpu

F32 = jnp.float32
BF16 = jnp.bfloat16

N_HEADS_DIFF = 8
HEAD_DIM_DIFF = 64
N_HEADS_GLA = 4
KEY_DIM_GLA = 128
VAL_DIM_GLA = 256
GLA_RANK = 16
GLA_TAU = 16.0
GLA_CHUNK = 64
N_GROUPS = 4
EXPERTS_PER_GROUP = 8
N_EXPERTS = N_GROUPS * EXPERTS_PER_GROUP
LN_EPS = 1e-5
LANES = 128
NEG_BIG = -1e30

OFF_QA, OFF_KA, OFF_VA = 0, 1024, 2048
OFF_QB, OFF_KB, OFF_VB, OFF_GB, OFF_GATES = 3072, 3584, 4096, 5120, 6144
W_MAIN = 8192

VMEM_LIMIT = 56 * 1024 * 1024


def _cparams(sem, **kw):
    return pltpu.CompilerParams(dimension_semantics=sem, vmem_limit_bytes=VMEM_LIMIT, **kw)


def _standardize(x):
    mu = jnp.mean(x, axis=-1, keepdims=True)
    xc = x - mu
    var = jnp.mean(xc * xc, axis=-1, keepdims=True)
    return xc * lax.rsqrt(var + LN_EPS)


def _sigmoid(x):
    return 1.0 / (1.0 + jnp.exp(-x))


ROW_TILES = 8


def _store_token_tiles(ref, x):
    t = x.shape[0]
    for s in range(ROW_TILES):
        ref[pl.ds(s, t, stride=ROW_TILES), :] = x[:, s * LANES:(s + 1) * LANES]


def _load_token_tiles(ref):
    t = ref.shape[0] // ROW_TILES
    return jnp.concatenate([ref[pl.ds(s, t, stride=ROW_TILES), :] for s in range(ROW_TILES)], axis=1)


def _ada_kernel(c_ref, w_ref, b_ref, o_ref):
    c = c_ref[...]
    cond = c * _sigmoid(c)
    o_ref[0] = jnp.dot(cond, w_ref[0], preferred_element_type=F32,
                       precision=lax.Precision.HIGHEST) + b_ref[0]


def _ada(c, w_ada, b_ada):
    depth, d, d6 = w_ada.shape
    b = c.shape[0]
    return pl.pallas_call(
        _ada_kernel,
        out_shape=jax.ShapeDtypeStruct((depth, b, d6), F32),
        grid=(depth, d6 // d),
        in_specs=[pl.BlockSpec((b, d), lambda l, j: (0, 0)),
                  pl.BlockSpec((1, d, d), lambda l, j: (l, 0, j)),
                  pl.BlockSpec((1, 1, d), lambda l, j: (l, 0, j))],
        out_specs=pl.BlockSpec((1, b, d), lambda l, j: (l, 0, j)),
        compiler_params=_cparams(("arbitrary", "arbitrary")),
    )(c, w_ada, b_ada.reshape(depth, 1, d6))


def _prep_tile(x, sh_ref, sc_ref, wab_ref, wal_ref, bal_ref, la_ref, u_scr):
    u = _standardize(x) * (1.0 + sc_ref[0]) + sh_ref[0]
    ub = u.astype(BF16)
    u_scr[...] = ub
    ab = jnp.dot(ub, wab_ref[...], preferred_element_type=F32)
    pre = jnp.dot(ab.astype(BF16), wal_ref[...], preferred_element_type=F32) + bal_ref[...]
    la_ref[...] = (jnp.minimum(pre, 0.0) - jnp.log(1.0 + jnp.exp(-jnp.abs(pre)))) * (1.0 / GLA_TAU)


def _inproj_kernel(x_ref, sh_ref, sc_ref, wa_ref, wg_ref, wab_ref, wal_ref, bal_ref,
                   p_ref, la_ref, u_scr, *, n_a):
    j = pl.program_id(1)

    @pl.when(j == 0)
    def _():
        _prep_tile(x_ref[...], sh_ref, sc_ref, wab_ref, wal_ref, bal_ref, la_ref, u_scr)

    @pl.when(j < n_a)
    def _():
        p_ref[...] = jnp.dot(u_scr[...], wa_ref[...], preferred_element_type=F32).astype(BF16)

    @pl.when(j >= n_a)
    def _():
        p_ref[...] = jnp.dot(u_scr[...], wg_ref[...], preferred_element_type=F32).astype(BF16)


def _inproj(x2, sh, sc, w_att, w_gates, w_ab, w_alpha, b_alpha, seq, tm=1024, tn=1024):
    n, d = x2.shape
    tpb = seq // tm
    wq = w_alpha.shape[1]
    n_a = w_att.shape[1] // tn
    return pl.pallas_call(
        functools.partial(_inproj_kernel, n_a=n_a),
        out_shape=(jax.ShapeDtypeStruct((n, W_MAIN), BF16),
                   jax.ShapeDtypeStruct((n, wq), F32)),
        grid=(n // tm, W_MAIN // tn),
        in_specs=[pl.BlockSpec((tm, d), lambda i, j: (i, 0)),
                  pl.BlockSpec((1, 1, d), lambda i, j: (i // tpb, 0, 0)),
                  pl.BlockSpec((1, 1, d), lambda i, j: (i // tpb, 0, 0)),
                  pl.BlockSpec((d, tn), lambda i, j: (0, jnp.minimum(j, n_a - 1))),
                  pl.BlockSpec((d, tn), lambda i, j: (0, jnp.maximum(j - n_a, 0))),
                  pl.BlockSpec((d, LANES), lambda i, j: (0, 0)),
                  pl.BlockSpec((LANES, wq), lambda i, j: (0, 0)),
                  pl.BlockSpec((1, wq), lambda i, j: (0, 0))],
        out_specs=(pl.BlockSpec((tm, tn), lambda i, j: (i, j)),
                   pl.BlockSpec((tm, wq), lambda i, j: (i, 0))),
        scratch_shapes=[pltpu.VMEM((tm, d), BF16)],
        compiler_params=_cparams(("arbitrary", "arbitrary")),
    )(x2, sh, sc, w_att, w_gates, w_ab, w_alpha, b_alpha)


ONES_ROWS = 16


QUERY_GROUP = 256
POS_SPLIT = 32


def _attn_kernel(slopes_ref, q_ref, k_ref, v_ref, lq1_ref, lk1_ref, lq2_ref, lk2_ref, g_ref,
                 o_ref, kaug_scr, vt_scr, dmask_scr, sa_scr, sb_scr, acc_scr, m_scr, *, tk, lam_init):
    h = pl.program_id(1)
    qi = pl.program_id(2)
    slope = slopes_ref[h]
    dh = HEAD_DIM_DIFF
    dv = LANES
    n_chunks = vt_scr.shape[0]
    halves = q_ref.shape[0] // tk

    @pl.when(qi == 0)
    def _():
        for j in range(n_chunks):
            vt_scr[j, 0:dv, :] = v_ref[j * tk:(j + 1) * tk, :].astype(F32).T.astype(BF16)
            vt_scr[j, dv:dv + ONES_ROWS, :] = jnp.ones((ONES_ROWS, tk), BF16)
        kaug_scr[:, 0:LANES] = k_ref[...]
        koff = lax.broadcasted_iota(jnp.int32, (tk, LANES), 0)
        flane = lax.broadcasted_iota(jnp.int32, (tk, LANES), 1)
        feat = jnp.where(flane == 0, koff // POS_SPLIT, jnp.where(flane == 1, koff % POS_SPLIT, 0))
        feat = feat.astype(F32).astype(BF16)
        for j in range(n_chunks):
            kaug_scr[j * tk:(j + 1) * tk, LANES:2 * LANES] = feat
        kpos = lax.broadcasted_iota(jnp.int32, (tk, 2 * tk), 0)
        qpos = lax.broadcasted_iota(jnp.int32, (tk, 2 * tk), 1)
        qoff = jnp.where(qpos >= tk, qpos - tk, qpos)
        dmask_scr[...] = jnp.where(qoff >= kpos, 0.0, NEG_BIG)

    q = q_ref[...] * jnp.asarray(dh ** -0.5, BF16)
    lane = lax.broadcasted_iota(jnp.int32, q.shape, 1)
    zero = jnp.zeros_like(q)
    qf = jnp.where(lane == 0, slope * POS_SPLIT, jnp.where(lane == 1, slope, 0.0)).astype(BF16)
    q1 = jnp.concatenate([jnp.where(lane < dh, q, zero), qf], axis=1)
    q2 = jnp.concatenate([jnp.where(lane >= dh, q, zero), qf], axis=1)
    qq = jnp.concatenate([part[hq * tk:(hq + 1) * tk] for hq in range(halves) for part in (q1, q2)], axis=0)

    m_scr[...] = jnp.full(m_scr.shape, NEG_BIG, F32)
    acc_scr[...] = jnp.zeros(acc_scr.shape, F32)

    per_half = 2 * tk // QUERY_GROUP
    groups = [(hq, slice((hq * per_half + g) * QUERY_GROUP, (hq * per_half + g + 1) * QUERY_GROUP),
               slice(g * QUERY_GROUP, (g + 1) * QUERY_GROUP))
              for hq in range(halves) for g in range(per_half)]

    def score(kaug, ls):
        return lax.dot_general(kaug, qq[ls], (((1,), (1,)), ((), ())), preferred_element_type=F32)

    def step(j_cur, src, j_next, dst, first_half, masked_half, next_first_half=0):
        if dst is not None:
            kaug = kaug_scr[pl.ds(pl.multiple_of(j_next * tk, tk), tk), :]
        vt = vt_scr[j_cur]
        m_all = m_scr[...]
        acc_all = acc_scr[...]
        done = []
        for hq, ls, ms in groups:
            if dst is not None and hq >= next_first_half:
                dst[:, ls] = score(kaug, ls)
            if hq < first_half:
                continue
            cj = (-slope) * ((halves * qi + hq - j_cur) * tk).astype(F32)
            s = src[:, ls]
            if hq == masked_half:
                s = s + dmask_scr[:, ms]
            m_prev = m_all[:, ls]
            m_new = jnp.maximum(m_prev, jnp.max(s, axis=0, keepdims=True) + cj)
            p = jnp.exp(s - (m_new - cj))
            alpha = jnp.exp(m_prev - m_new)
            done.append((ls, m_new, alpha * acc_all[:, ls] + jnp.dot(vt, p.astype(BF16),
                                                                     preferred_element_type=F32)))
        for ls, m_new, acc_new in done:
            m_scr[:, ls] = m_new
            acc_scr[:, ls] = acc_new

    kaug0 = kaug_scr[0:tk, :]
    for _, ls, _ in groups:
        sa_scr[:, ls] = score(kaug0, ls)

    def pair(jj, carry):
        j = 2 * jj
        step(j, sa_scr, j + 1, sb_scr, 0, None)
        step(j + 1, sb_scr, j + 2, sa_scr, 0, None)
        return carry

    lax.fori_loop(0, qi, pair, 0)
    step(2 * qi, sa_scr, 2 * qi + 1, sb_scr, 0, 0, next_first_half=1)
    step(2 * qi + 1, sb_scr, None, None, 1, 1)

    lam = (jnp.exp(jnp.sum(lq1_ref[...] * lk1_ref[...], axis=-1, keepdims=True))
           - jnp.exp(jnp.sum(lq2_ref[...] * lk2_ref[...], axis=-1, keepdims=True)) + lam_init)
    acc = acc_scr[...]
    ot = acc[0:dv] / acc[dv:dv + 1]
    for hq in range(halves):
        base = hq * 2 * tk
        o = (ot[:, base:base + tk] - lam * ot[:, base + tk:base + 2 * tk]).T
        o = o * lax.rsqrt(jnp.mean(o * o, axis=-1, keepdims=True) + LN_EPS) * g_ref[...] * (1.0 - lam_init)
        o_ref[hq * tk:(hq + 1) * tk, :] = o.astype(o_ref.dtype)


def _attn(p3, lq1, lk1, lq2, lk2, g, lam_init, tk=512):
    b, s, _ = p3.shape
    h = N_HEADS_DIFF
    tq = 2 * tk
    slopes = jnp.asarray(2.0 ** (-8.0 * np.arange(1, h + 1) / h), dtype=F32)
    vec = pl.BlockSpec((1, HEAD_DIM_DIFF), lambda bi, hi, qi, sl: (0, 0))
    return pl.pallas_call(
        functools.partial(_attn_kernel, tk=tk, lam_init=lam_init),
        out_shape=jax.ShapeDtypeStruct((b, s, h * LANES), BF16),
        grid_spec=pltpu.PrefetchScalarGridSpec(
            num_scalar_prefetch=1,
            grid=(b, h, s // tq),
            in_specs=[pl.BlockSpec((None, tq, LANES), lambda bi, hi, qi, sl: (bi, qi, OFF_QA // LANES + hi)),
                      pl.BlockSpec((None, s, LANES), lambda bi, hi, qi, sl: (bi, 0, OFF_KA // LANES + hi)),
                      pl.BlockSpec((None, s, LANES), lambda bi, hi, qi, sl: (bi, 0, OFF_VA // LANES + hi)),
                      vec, vec, vec, vec,
                      pl.BlockSpec((1, LANES), lambda bi, hi, qi, sl: (0, 0))],
            out_specs=pl.BlockSpec((None, tq, LANES), lambda bi, hi, qi, sl: (bi, qi, hi)),
            scratch_shapes=[pltpu.VMEM((s, 2 * LANES), BF16),
                            pltpu.VMEM((s // tk, LANES + ONES_ROWS, tk), BF16),
                            pltpu.VMEM((tk, 2 * tk), F32),
                            pltpu.VMEM((tk, 2 * tq), F32),
                            pltpu.VMEM((tk, 2 * tq), F32),
                            pltpu.VMEM((LANES + ONES_ROWS, 2 * tq), F32),
                            pltpu.VMEM((1, 2 * tq), F32)]),
        compiler_params=_cparams(("arbitrary", "arbitrary", "arbitrary")),
    )(slopes, p3, p3, p3, lq1, lk1, lq2, lk2, g)


GLA_HEADS_PER_STEP = 2


def _gla_kernel(q_ref, k_ref, v_ref, gb_ref, la_ref, g_ref, o_ref, state_scr, *, tt):
    c = GLA_CHUNK
    dk, dv = KEY_DIM_GLA, VAL_DIM_GLA
    hp = state_scr.shape[0]
    n_c = tt // c
    work = [(hh, ci) for hh in range(hp) for ci in range(n_c)]

    @pl.when(pl.program_id(2) == 0)
    def _():
        state_scr[...] = jnp.zeros(state_scr.shape, F32)

    rr = lax.broadcasted_iota(jnp.int32, (c, c), 0)
    cr = lax.broadcasted_iota(jnp.int32, (c, c), 1)
    causal = cr <= rr
    tri = jnp.where(causal, 1.0, 0.0).astype(BF16)
    qscale = dk ** -0.5
    rows = lambda ci: slice(ci * c, (ci + 1) * c)
    kcols = lambda hh: slice(hh * dk, (hh + 1) * dk)
    vcols = lambda hh: slice(hh * dv, (hh + 1) * dv)

    cum = {}
    for hh, ci in work:
        la = la_ref[rows(ci), kcols(hh)]
        la_hi = la.astype(BF16)
        la_lo = (la - la_hi.astype(F32)).astype(BF16)
        cum2 = jnp.dot(tri, jnp.concatenate([la_hi, la_lo], axis=1), preferred_element_type=F32)
        cum[hh, ci] = cum2[:, :dk] + cum2[:, dk:]

    qe, q2, k2, kd, dec, v = {}, {}, {}, {}, {}, {}
    for hh, ci in work:
        cum_c = cum[hh, ci]
        last = cum_c[c - 1:c]
        mid = cum_c[c // 2:c // 2 + 1]
        q_c = q_ref[rows(ci), kcols(hh)].astype(F32) * qscale
        k_c = k_ref[rows(ci), kcols(hh)].astype(F32)
        qe[hh, ci] = (q_c * jnp.exp(cum_c)).astype(BF16)
        q2[hh, ci] = (q_c * jnp.exp(cum_c - mid)).astype(BF16)
        k2[hh, ci] = (k_c * jnp.exp(mid - cum_c)).astype(BF16)
        kd[hh, ci] = (k_c * jnp.exp(last - cum_c)).astype(BF16)
        dec[hh, ci] = jnp.exp(last)
        v[hh, ci] = v_ref[rows(ci), vcols(hh)]

    att = {w: lax.dot_general(q2[w], k2[w], (((1,), (1,)), ((), ())), preferred_element_type=F32)
           for w in work}
    o_intra = {w: jnp.dot(jnp.where(causal, att[w], 0.0).astype(BF16), v[w], preferred_element_type=F32)
               for w in work}
    kv = {w: jnp.dot(v[w].astype(F32).T.astype(BF16), kd[w], preferred_element_type=F32) for w in work}

    o_inter = {}
    for hh in range(hp):
        st = state_scr[hh]
        for ci in range(n_c):
            o_inter[hh, ci] = lax.dot_general(qe[hh, ci], st.astype(BF16), (((1,), (1,)), ((), ())),
                                              preferred_element_type=F32)
            st = st * dec[hh, ci] + kv[hh, ci]
        state_scr[hh] = st

    for hh, ci in work:
        o = o_inter[hh, ci] + o_intra[hh, ci]
        o = o * lax.rsqrt(jnp.mean(o * o, axis=-1, keepdims=True) + LN_EPS) * g_ref[...]
        gate = gb_ref[rows(ci), vcols(hh)].astype(F32)
        o_ref[rows(ci), vcols(hh)] = (o * (gate * _sigmoid(gate))).astype(o_ref.dtype)


def _gla(p3, la3, g, tt=512):
    b, s, _ = p3.shape
    hp = GLA_HEADS_PER_STEP
    dk, dv = KEY_DIM_GLA * hp, VAL_DIM_GLA * hp
    return pl.pallas_call(
        functools.partial(_gla_kernel, tt=tt),
        out_shape=jax.ShapeDtypeStruct((b, s, N_HEADS_GLA * VAL_DIM_GLA), BF16),
        grid=(b, N_HEADS_GLA // hp, s // tt),
        in_specs=[pl.BlockSpec((None, tt, dk), lambda bi, hi, ti: (bi, ti, OFF_QB // dk + hi)),
                  pl.BlockSpec((None, tt, dk), lambda bi, hi, ti: (bi, ti, OFF_KB // dk + hi)),
                  pl.BlockSpec((None, tt, dv), lambda bi, hi, ti: (bi, ti, OFF_VB // dv + hi)),
                  pl.BlockSpec((None, tt, dv), lambda bi, hi, ti: (bi, ti, OFF_GB // dv + hi)),
                  pl.BlockSpec((None, tt, dk), lambda bi, hi, ti: (bi, ti, hi)),
                  pl.BlockSpec((1, VAL_DIM_GLA), lambda bi, hi, ti: (0, 0))],
        out_specs=pl.BlockSpec((None, tt, dv), lambda bi, hi, ti: (bi, ti, hi)),
        scratch_shapes=[pltpu.VMEM((hp, VAL_DIM_GLA, KEY_DIM_GLA), F32)],
        compiler_params=_cparams(("arbitrary", "arbitrary", "arbitrary")),
    )(p3, p3, p3, p3, la3, g)


def _mix_kernel(oa_ref, ob_ref, gt_ref, x_ref, bg_ref, g1_ref, sh2_ref, sc2_ref, lng_ref, lnb_ref,
                wba_ref, wbb_ref, wo_ref, wr_ref, br_ref,
                x1_ref, u2_ref, lg_ref, *, alpha):
    d = x_ref.shape[-1]
    a = jnp.dot(oa_ref[...], wba_ref[...], preferred_element_type=F32)
    bm = jnp.dot(ob_ref[...], wbb_ref[...], preferred_element_type=F32)
    gates = _sigmoid(gt_ref[...].astype(F32) + bg_ref[...])
    mixed = gates[:, :d] * a + gates[:, d:] * bm
    y = jnp.dot(mixed.astype(BF16), wo_ref[...], preferred_element_type=F32)
    x1 = _standardize(alpha * x_ref[...] + g1_ref[0] * y) * lng_ref[...] + lnb_ref[...]
    x1_ref[...] = x1
    u2 = _standardize(x1) * (1.0 + sc2_ref[0]) + sh2_ref[0]
    _store_token_tiles(u2_ref, u2)
    uh = u2.astype(BF16)
    ul = (u2 - uh.astype(F32)).astype(BF16)
    hh = jnp.dot(uh, wr_ref[...], preferred_element_type=F32)
    lg_ref[...] = (hh[:, :LANES] + hh[:, LANES:]
                   + jnp.dot(ul, wr_ref[:, 0:LANES], preferred_element_type=F32) + br_ref[...])


def _mix(oa, ob, p, x2, bg, g1, sh2, sc2, lng, lnb, wba, wbb, wo, wr2, br, seq, alpha, tm=512):
    n, d = x2.shape
    tpb = seq // tm
    row = lambda i: (i, 0)
    const = lambda i: (0, 0)
    per_b = pl.BlockSpec((1, 1, d), lambda i: (i // tpb, 0, 0))
    return pl.pallas_call(
        functools.partial(_mix_kernel, alpha=alpha),
        out_shape=(jax.ShapeDtypeStruct((n, d), F32),
                   jax.ShapeDtypeStruct((n * (d // LANES), LANES), F32),
                   jax.ShapeDtypeStruct((n, LANES), F32)),
        grid=(n // tm,),
        in_specs=[pl.BlockSpec((tm, d), row), pl.BlockSpec((tm, d), row),
                  pl.BlockSpec((tm, 2 * d), lambda i: (i, OFF_GATES // (2 * d))),
                  pl.BlockSpec((tm, d), row),
                  pl.BlockSpec((1, 2 * d), const), per_b, per_b, per_b,
                  pl.BlockSpec((1, d), const), pl.BlockSpec((1, d), const),
                  pl.BlockSpec((d, d), const), pl.BlockSpec((d, d), const), pl.BlockSpec((d, d), const),
                  pl.BlockSpec((d, 2 * LANES), const),
                  pl.BlockSpec((1, LANES), const)],
        out_specs=(pl.BlockSpec((tm, d), row), pl.BlockSpec((tm * (d // LANES), LANES), row),
                   pl.BlockSpec((tm, LANES), row)),
        compiler_params=_cparams(("arbitrary",)),
    )(oa, ob, p, x2, bg, g1, sh2, sc2, lng, lnb, wba, wbb, wo, wr2, br)


def _route_kernel(lg_ref, ri_ref, rw_ref, cnt_ref, *, tm):
    @pl.when(pl.program_id(0) == 0)
    def _():
        cnt_ref[...] = jnp.zeros(cnt_ref.shape, F32)

    lg = lg_ref[...]
    lane = lax.broadcasted_iota(jnp.int32, lg.shape, 1)
    lanef = lane.astype(F32)
    far = float(LANES)
    is_g = lane < N_GROUPS
    gl = jnp.where(is_g, lg, NEG_BIG)
    gmax = jnp.max(gl, axis=-1, keepdims=True)
    gidx = jnp.min(jnp.where(gl == gmax, lanef, far), axis=-1, keepdims=True)
    gw = 1.0 / jnp.sum(jnp.where(is_g, jnp.exp(gl - gmax), 0.0), axis=-1, keepdims=True)
    lo = N_GROUPS + gidx * EXPERTS_PER_GROUP
    in_g = (lanef >= lo) & (lanef < lo + EXPERTS_PER_GROUP)
    el = jnp.where(in_g, lg, NEG_BIG)
    v1 = jnp.max(el, axis=-1, keepdims=True)
    i1 = jnp.min(jnp.where(in_g & (el == v1), lanef, far), axis=-1, keepdims=True)
    in_g2 = in_g & (lanef != i1)
    el2 = jnp.where(in_g2, lg, NEG_BIG)
    v2 = jnp.max(el2, axis=-1, keepdims=True)
    i2 = jnp.min(jnp.where(in_g2 & (el2 == v2), lanef, far), axis=-1, keepdims=True)
    t = jnp.exp(v2 - v1)
    w1 = gw / (1.0 + t)
    w2 = gw * t / (1.0 + t)

    oh1 = lanef == i1
    oh2 = lanef == i2
    oh = jnp.where(oh1 | oh2, 1.0, 0.0)
    r = lax.broadcasted_iota(jnp.int32, (tm, tm), 0)
    c = lax.broadcasted_iota(jnp.int32, (tm, tm), 1)
    lower = jnp.where(c < r, 1.0, 0.0).astype(BF16)
    base = jnp.dot(lower, oh.astype(BF16), preferred_element_type=F32) + cnt_ref[0:1, :]
    r1 = jnp.sum(jnp.where(oh1, base, 0.0), axis=-1, keepdims=True)
    r2 = jnp.sum(jnp.where(oh2, base, 0.0), axis=-1, keepdims=True)
    cnt_ref[...] = cnt_ref[...] + jnp.sum(oh, axis=0, keepdims=True)

    e1 = i1 - float(N_GROUPS)
    e2 = i2 - float(N_GROUPS)
    ri = jnp.where(lane == 0, e1, jnp.where(lane == 1, e2, jnp.where(lane == 2, r1, jnp.where(lane == 3, r2, 0.0))))
    ri_ref[...] = ri.astype(jnp.int32)
    rw_ref[...] = jnp.where(lane == 0, w1, jnp.where(lane == 1, w2, 0.0))


def _route(lg, tm=512):
    n = lg.shape[0]
    row = lambda i: (i, 0)
    return pl.pallas_call(
        functools.partial(_route_kernel, tm=tm),
        out_shape=(jax.ShapeDtypeStruct((n, LANES), jnp.int32),
                   jax.ShapeDtypeStruct((n, LANES), F32),
                   jax.ShapeDtypeStruct((8, LANES), F32)),
        grid=(n // tm,),
        in_specs=[pl.BlockSpec((tm, LANES), row)],
        out_specs=(pl.BlockSpec((tm, LANES), row), pl.BlockSpec((tm, LANES), row),
                   pl.BlockSpec((8, LANES), lambda i: (0, 0))),
        compiler_params=_cparams(("arbitrary",)),
    )(lg)


MOE_ROWS = 256


def _expert_kernel(be_ref, nu_ref, tok_a_ref, tok_b_ref, dst_ref, u_ref, wg_ref, wu_ref, wd_ref, y_ref,
                   xb0, xb1, yb0, yb1, wg_scr, wu_scr, wd_scr, gsem, ssem):
    i = pl.program_id(0)
    n_used = nu_ref[0]
    blk = tok_a_ref.shape[-1]
    xbs, ybs = (xb0, xb1), (yb0, yb1)

    def tile(ref, t):
        return ref.at[pl.ds(pl.multiple_of(t * ROW_TILES, ROW_TILES), ROW_TILES)]

    def gather_issue(tok_ref, xb, sem):
        for r in range(blk):
            pltpu.make_async_copy(tile(u_ref, tok_ref[0, 0, r]), tile(xb, r), sem).start()

    def gather_wait(xb, sem):
        pltpu.make_async_copy(u_ref.at[pl.ds(0, blk * ROW_TILES)], xb, sem).wait()

    def scatter_issue(yb, sem):
        for r in range(blk):
            pltpu.make_async_copy(tile(yb, r), tile(y_ref, dst_ref[0, 0, r]), sem).start()

    def scatter_wait(yb, sem):
        pltpu.make_async_copy(yb, y_ref.at[pl.ds(0, blk * ROW_TILES)], sem).wait()

    def load_weights():
        @pl.when((i == 0) | (be_ref[i] != be_ref[jnp.maximum(i - 1, 0)]))
        def _():
            wg_scr[...] = wg_ref[0].astype(BF16)
            wu_scr[...] = wu_ref[0].astype(BF16)
            wd_scr[...] = wd_ref[0].astype(BF16)

    def compute(xb, yb):
        x = _load_token_tiles(xb).astype(BF16)
        g = jnp.dot(x, wg_scr[...], preferred_element_type=F32)
        u = jnp.dot(x, wu_scr[...], preferred_element_type=F32)
        hid = (g * _sigmoid(g) * u).astype(BF16)
        _store_token_tiles(yb, jnp.dot(hid, wd_scr[...], preferred_element_type=F32))

    @pl.when(i == 0)
    def _():
        gather_issue(tok_a_ref, xb0, gsem.at[0])
        gather_wait(xb0, gsem.at[0])
        load_weights()
        gather_issue(tok_b_ref, xb1, gsem.at[1])
        compute(xb0, yb0)

    for p in range(2):
        cur, oth = p, 1 - p

        @pl.when((i >= 1) & (i < n_used) & (i % 2 == p))
        def _():
            gather_wait(xbs[cur], gsem.at[cur])

            @pl.when(i >= 2)
            def _():
                scatter_wait(ybs[cur], ssem.at[cur])

            load_weights()
            gather_issue(tok_b_ref, xbs[oth], gsem.at[oth])
            scatter_issue(ybs[oth], ssem.at[oth])
            compute(xbs[cur], ybs[cur])

        @pl.when((i == n_used) & (i % 2 == p))
        def _():
            gather_wait(xbs[cur], gsem.at[cur])

            @pl.when(i >= 2)
            def _():
                scatter_wait(ybs[cur], ssem.at[cur])

            scatter_issue(ybs[oth], ssem.at[oth])

        @pl.when((i == n_used + 1) & (i % 2 == p))
        def _():
            scatter_wait(ybs[cur], ssem.at[cur])


def _experts(blk_e, n_used, src_tok3, dst_row3, u2t, wg, wu, wd, layer, n_rows):
    d, de = wg.shape[-2:]
    blk = MOE_ROWS
    rt = d // LANES
    n_steps = blk_e.shape[0]
    wspec = lambda r, c: pl.BlockSpec((None, 1, r, c), lambda i, be, nu: (layer, be[i], 0, 0))
    ispec = lambda f: pl.BlockSpec((1, 1, blk), lambda i, be, nu: (f(i, nu), 0, 0), memory_space=pltpu.SMEM)
    return pl.pallas_call(
        _expert_kernel,
        out_shape=jax.ShapeDtypeStruct((n_rows * rt, LANES), F32),
        grid_spec=pltpu.PrefetchScalarGridSpec(
            num_scalar_prefetch=2,
            grid=(n_steps,),
            in_specs=[ispec(lambda i, nu: jnp.minimum(i, nu[0] - 1)),
                      ispec(lambda i, nu: jnp.minimum(i + 1, nu[0] - 1)),
                      ispec(lambda i, nu: jnp.clip(i - 1, 0, nu[0] - 1)),
                      pl.BlockSpec(memory_space=pl.ANY),
                      wspec(d, de), wspec(d, de), wspec(de, d)],
            out_specs=pl.BlockSpec(memory_space=pl.ANY),
            scratch_shapes=[pltpu.VMEM((blk * rt, LANES), F32), pltpu.VMEM((blk * rt, LANES), F32),
                            pltpu.VMEM((blk * rt, LANES), F32), pltpu.VMEM((blk * rt, LANES), F32),
                            pltpu.VMEM((d, de), BF16), pltpu.VMEM((d, de), BF16),
                            pltpu.VMEM((de, d), BF16),
                            pltpu.SemaphoreType.DMA((2,)), pltpu.SemaphoreType.DMA((2,))]),
        compiler_params=_cparams(("arbitrary",), has_side_effects=True),
    )(blk_e, n_used, src_tok3, src_tok3, dst_row3, u2t, wg, wu, wd)


def _combine_kernel(x1_ref, y1_ref, y2_ref, rw_ref, g2_ref, lng_ref, lnb_ref, o_ref, *, alpha):
    rw = rw_ref[...]
    y = rw[:, 0:1] * _load_token_tiles(y1_ref) + rw[:, 1:2] * _load_token_tiles(y2_ref)
    o_ref[...] = _standardize(alpha * x1_ref[...] + g2_ref[0] * y) * lng_ref[...] + lnb_ref[...]


def _combine(x1, y_tok, rw, g2, lng, lnb, seq, alpha, tm=512):
    n, d = x1.shape
    tpb = seq // tm
    const = lambda i: (0, 0)
    return pl.pallas_call(
        functools.partial(_combine_kernel, alpha=alpha),
        out_shape=jax.ShapeDtypeStruct((n, d), F32),
        grid=(n // tm,),
        in_specs=[pl.BlockSpec((tm, d), lambda i: (i, 0)),
                  pl.BlockSpec((tm * (d // LANES), LANES), lambda i: (i, 0)),
                  pl.BlockSpec((tm * (d // LANES), LANES), lambda i: (n // tm + i, 0)),
                  pl.BlockSpec((tm, LANES), lambda i: (i, 0)),
                  pl.BlockSpec((1, 1, d), lambda i: (i // tpb, 0, 0)),
                  pl.BlockSpec((1, d), const), pl.BlockSpec((1, d), const)],
        out_specs=pl.BlockSpec((tm, d), lambda i: (i, 0)),
        compiler_params=_cparams(("arbitrary",)),
    )(x1, y_tok, y_tok, rw, g2, lng, lnb)


ID_SPLIT = 256


def _invert_kernel(ri_ref, ps_ref, inv_ref, *, n):
    i = pl.program_id(0)
    tm = ri_ref.shape[0]
    n_hi = inv_ref.shape[0]

    @pl.when(i == 0)
    def _():
        inv_ref[...] = jnp.zeros(inv_ref.shape, F32)

    ri = ri_ref[...].astype(F32)
    lane = lax.broadcasted_iota(jnp.int32, (tm, LANES), 1)
    lanef = lane.astype(F32)
    tok = (lax.broadcasted_iota(jnp.int32, (tm, 1), 0) + i * tm).astype(F32)
    cols = []
    onehot_lo = []
    for j in range(2):
        e = ri[:, j:j + 1]
        row = jnp.sum(jnp.where(lanef == e, ps_ref[...], 0.0), axis=-1, keepdims=True) + ri[:, 2 + j:3 + j]
        hi = jnp.floor(row * (1.0 / LANES))
        lo = row - hi * LANES
        ident = tok + float(j * n + 1)
        id_hi = jnp.floor(ident * (1.0 / ID_SPLIT))
        cols += [hi, id_hi, ident - id_hi * ID_SPLIT]
        onehot_lo.append(jnp.where(lanef == lo, 1.0, 0.0).astype(BF16))
    packed = jnp.zeros((tm, LANES), F32)
    for k, col in enumerate(cols):
        packed = jnp.where(lane == k, col, packed)
    rows = packed.T
    hsel = lax.broadcasted_iota(jnp.int32, (n_hi, tm), 0).astype(F32)
    lhs_hi, lhs_lo = [], []
    for j in range(2):
        hit = hsel == rows[3 * j:3 * j + 1]
        lhs_hi.append(jnp.where(hit, rows[3 * j + 1:3 * j + 2], 0.0).astype(BF16))
        lhs_lo.append(jnp.where(hit, rows[3 * j + 2:3 * j + 3], 0.0).astype(BF16))
    rhs = jnp.concatenate(onehot_lo, axis=0)
    inv_ref[...] += (float(ID_SPLIT) * jnp.dot(jnp.concatenate(lhs_hi, axis=1), rhs, preferred_element_type=F32)
                     + jnp.dot(jnp.concatenate(lhs_lo, axis=1), rhs, preferred_element_type=F32))


def _invert(ri, pstarts, n_rows, tm=512):
    n = ri.shape[0]
    ps = jnp.pad(pstarts.astype(F32), (0, LANES - pstarts.shape[0])).reshape(1, LANES)
    return pl.pallas_call(
        functools.partial(_invert_kernel, n=n),
        out_shape=jax.ShapeDtypeStruct((n_rows // LANES, LANES), F32),
        grid=(n // tm,),
        in_specs=[pl.BlockSpec((tm, LANES), lambda i: (i, 0)),
                  pl.BlockSpec((1, LANES), lambda i: (0, 0))],
        out_specs=pl.BlockSpec((n_rows // LANES, LANES), lambda i: (0, 0)),
        compiler_params=_cparams(("arbitrary",)),
    )(ri, ps)


def _moe_plan(ri, cnt, n):
    blk = MOE_ROWS
    n_rows = 2 * n + N_EXPERTS * blk
    n_blocks = n_rows // blk
    n_steps = n_blocks + 2
    counts = cnt[0, N_GROUPS:N_GROUPS + N_EXPERTS].astype(jnp.int32)
    padded = ((counts + blk - 1) // blk) * blk
    pends = jnp.cumsum(padded)
    pstarts = pends - padded
    cstarts = jnp.cumsum(counts) - counts
    blk_start = jnp.arange(n_steps, dtype=jnp.int32) * blk
    blk_e = jnp.minimum(jnp.sum(blk_start[:, None] >= pends[None, :], axis=1), N_EXPERTS - 1).astype(jnp.int32)
    n_used = (pends[-1:] // blk).astype(jnp.int32)
    prow = jnp.arange(n_rows, dtype=jnp.int32)
    pe = jnp.repeat(blk_e[:n_blocks], blk)
    spare = 2 * n + prow - (cstarts[pe] + jnp.minimum(prow - pstarts[pe], counts[pe]))
    inv = _invert(ri, pstarts, n_rows).reshape(-1).astype(jnp.int32)
    dst_row = jnp.where(inv > 0, inv - 1, spare)
    src_tok = dst_row % n
    return blk_e, n_used, src_tok.reshape(n_blocks, 1, blk), dst_row.reshape(n_blocks, 1, blk)


def kernel(x, c, w_ada, b_ada, w_in, b_gates, w_alpha, b_alpha, lambda_q1, lambda_k1, lambda_q2, lambda_k2, diff_norm_g, gla_norm_g, w_branch_a, w_branch_b, w_out, ln1_g, ln1_b, w_router_g, b_router_g, w_router_e, b_router_e, w_gate_e, w_up_e, w_down_e, ln2_g, ln2_b):
    b, s, d = x.shape
    depth = w_ada.shape[0]
    n = b * s
    alpha = (2.0 * depth) ** 0.25

    ada = _ada(c, w_ada, b_ada)
    x2 = x.reshape(n, d)
    for l in range(depth):
        sh1, sc1, g1, sh2, sc2, g2 = [ada[l, :, i * d:(i + 1) * d].reshape(b, 1, d) for i in range(6)]
        wl = w_in[l]
        w_att = wl[:, :OFF_GATES].astype(BF16)
        w_gates = wl[:, OFF_GATES + GLA_RANK:].astype(BF16)
        w_ab = jnp.pad(wl[:, OFF_GATES:OFF_GATES + GLA_RANK], ((0, 0), (0, LANES - GLA_RANK))).astype(BF16)
        wal = jnp.pad(w_alpha[l], ((0, LANES - GLA_RANK), (0, 0))).astype(BF16)
        p, la = _inproj(x2, sh1, sc1, w_att, w_gates, w_ab, wal, b_alpha[l].reshape(1, -1), s)
        p3 = p.reshape(b, s, W_MAIN)

        lam_init = 0.8 - 0.6 * math.exp(-0.3 * l)
        oa = _attn(p3, lambda_q1[l].reshape(1, -1), lambda_k1[l].reshape(1, -1),
                   lambda_q2[l].reshape(1, -1), lambda_k2[l].reshape(1, -1),
                   diff_norm_g[l].reshape(1, -1), lam_init)
        ob = _gla(p3, la.reshape(b, s, -1), gla_norm_g[l].reshape(1, -1))

        wr = jnp.pad(jnp.concatenate([w_router_g[l], w_router_e[l]], axis=1),
                     ((0, 0), (0, LANES - N_GROUPS - N_EXPERTS)))
        wrh = wr.astype(BF16)
        wr2 = jnp.concatenate([wrh, (wr - wrh.astype(F32)).astype(BF16)], axis=1)
        br = jnp.pad(jnp.concatenate([b_router_g[l], b_router_e[l]]),
                     (0, LANES - N_GROUPS - N_EXPERTS)).reshape(1, LANES)
        x1, u2, lg = _mix(oa.reshape(n, d), ob.reshape(n, d), p, x2, b_gates[l].reshape(1, -1),
                          g1, sh2, sc2, ln1_g[l].reshape(1, d), ln1_b[l].reshape(1, d),
                          w_branch_a[l].astype(BF16), w_branch_b[l].astype(BF16), w_out[l].astype(BF16),
                          wr2, br, s, alpha)

        ri, rw, cnt = _route(lg)
        blk_e, n_used, src_tok3, dst_row3 = _moe_plan(ri, cnt, n)
        y_tok = _experts(blk_e, n_used, src_tok3, dst_row3, u2, w_gate_e, w_up_e, w_down_e, l,
                         2 * n + N_EXPERTS * MOE_ROWS)
        x2 = _combine(x1, y_tok, rw, g2, ln2_g[l].reshape(1, d), ln2_b[l].reshape(1, d), s, alpha)
    return x2.reshape(b, s, d)
```

```python
import functools
import math

import jax
import jax.numpy as jnp
import numpy as np
from jax import lax
from jax.experimental import pallas as pl
from jax.experimental.pallas import tpu as pltpu

F32 = jnp.float32
BF16 = jnp.bfloat16

N_HEADS_DIFF = 8
HEAD_DIM_DIFF = 64
N_HEADS_GLA = 4
KEY_DIM_GLA = 128
VAL_DIM_GLA = 256
GLA_RANK = 16
GLA_TAU = 16.0
GLA_CHUNK = 64
N_GROUPS = 4
EXPERTS_PER_GROUP = 8
N_EXPERTS = N_GROUPS * EXPERTS_PER_GROUP
LN_EPS = 1e-5
LANES = 128
NEG_BIG = -1e30

OFF_QA, OFF_KA, OFF_VA = 0, 1024, 2048
OFF_QB, OFF_KB, OFF_VB, OFF_GB, OFF_GATES = 3072, 3584, 4096, 5120, 6144
W_MAIN = 8192

VMEM_LIMIT = 56 * 1024 * 1024


def _cparams(sem, **kw):
    return pltpu.CompilerParams(dimension_semantics=sem, vmem_limit_bytes=VMEM_LIMIT, **kw)


def _standardize(x):
    mu = jnp.mean(x, axis=-1, keepdims=True)
    xc = x - mu
    var = jnp.mean(xc * xc, axis=-1, keepdims=True)
    return xc * lax.rsqrt(var + LN_EPS)


def _sigmoid(x):
    return 1.0 / (1.0 + jnp.exp(-x))


ROW_TILES = 8


def _store_token_tiles(ref, x):
    t = x.shape[0]
    for s in range(ROW_TILES):
        ref[pl.ds(s, t, stride=ROW_TILES), :] = x[:, s * LANES:(s + 1) * LANES]


def _load_token_tiles(ref):
    t = ref.shape[0] // ROW_TILES
    return jnp.concatenate([ref[pl.ds(s, t, stride=ROW_TILES), :] for s in range(ROW_TILES)], axis=1)


def _ada_kernel(c_ref, w_ref, b_ref, o_ref):
    c = c_ref[...]
    cond = c * _sigmoid(c)
    o_ref[0] = jnp.dot(cond, w_ref[0], preferred_element_type=F32,
                       precision=lax.Precision.HIGHEST) + b_ref[0]


def _ada(c, w_ada, b_ada):
    depth, d, d6 = w_ada.shape
    b = c.shape[0]
    return pl.pallas_call(
        _ada_kernel,
        out_shape=jax.ShapeDtypeStruct((depth, b, d6), F32),
        grid=(depth, d6 // d),
        in_specs=[pl.BlockSpec((b, d), lambda l, j: (0, 0)),
                  pl.BlockSpec((1, d, d), lambda l, j: (l, 0, j)),
                  pl.BlockSpec((1, 1, d), lambda l, j: (l, 0, j))],
        out_specs=pl.BlockSpec((1, b, d), lambda l, j: (l, 0, j)),
        compiler_params=_cparams(("arbitrary", "arbitrary")),
    )(c, w_ada, b_ada.reshape(depth, 1, d6))


def _prep_tile(x, sh_ref, sc_ref, wab_ref, wal_ref, bal_ref, la_ref, u_scr):
    u = _standardize(x) * (1.0 + sc_ref[0]) + sh_ref[0]
    ub = u.astype(BF16)
    u_scr[...] = ub
    ab = jnp.dot(ub, wab_ref[...], preferred_element_type=F32)
    pre = jnp.dot(ab.astype(BF16), wal_ref[...], preferred_element_type=F32) + bal_ref[...]
    la_ref[...] = (jnp.minimum(pre, 0.0) - jnp.log(1.0 + jnp.exp(-jnp.abs(pre)))) * (1.0 / GLA_TAU)


def _inproj_kernel(x_ref, sh_ref, sc_ref, wa_ref, wg_ref, wab_ref, wal_ref, bal_ref,
                   p_ref, la_ref, u_scr, *, n_a):
    j = pl.program_id(1)

    @pl.when(j == 0)
    def _():
        _prep_tile(x_ref[...], sh_ref, sc_ref, wab_ref, wal_ref, bal_ref, la_ref, u_scr)

    @pl.when(j < n_a)
    def _():
        p_ref[...] = jnp.dot(u_scr[...], wa_ref[...], preferred_element_type=F32).astype(BF16)

    @pl.when(j >= n_a)
    def _():
        p_ref[...] = jnp.dot(u_scr[...], wg_ref[...], preferred_element_type=F32).astype(BF16)


def _inproj(x2, sh, sc, w_att, w_gates, w_ab, w_alpha, b_alpha, seq, tm=1024, tn=1024):
    n, d = x2.shape
    tpb = seq // tm
    wq = w_alpha.shape[1]
    n_a = w_att.shape[1] // tn
    return pl.pallas_call(
        functools.partial(_inproj_kernel, n_a=n_a),
        out_shape=(jax.ShapeDtypeStruct((n, W_MAIN), BF16),
                   jax.ShapeDtypeStruct((n, wq), F32)),
        grid=(n // tm, W_MAIN // tn),
        in_specs=[pl.BlockSpec((tm, d), lambda i, j: (i, 0)),
                  pl.BlockSpec((1, 1, d), lambda i, j: (i // tpb, 0, 0)),
                  pl.BlockSpec((1, 1, d), lambda i, j: (i // tpb, 0, 0)),
                  pl.BlockSpec((d, tn), lambda i, j: (0, jnp.minimum(j, n_a - 1))),
                  pl.BlockSpec((d, tn), lambda i, j: (0, jnp.maximum(j - n_a, 0))),
                  pl.BlockSpec((d, LANES), lambda i, j: (0, 0)),
                  pl.BlockSpec((LANES, wq), lambda i, j: (0, 0)),
                  pl.BlockSpec((1, wq), lambda i, j: (0, 0))],
        out_specs=(pl.BlockSpec((tm, tn), lambda i, j: (i, j)),
                   pl.BlockSpec((tm, wq), lambda i, j: (i, 0))),
        scratch_shapes=[pltpu.VMEM((tm, d), BF16)],
        compiler_params=_cparams(("arbitrary", "arbitrary")),
    )(x2, sh, sc, w_att, w_gates, w_ab, w_alpha, b_alpha)


ONES_ROWS = 16


QUERY_GROUP = 256
POS_SPLIT = 32


def _attn_kernel(slopes_ref, q_ref, k_ref, v_ref, lq1_ref, lk1_ref, lq2_ref, lk2_ref, g_ref,
                 o_ref, kaug_scr, vt_scr, dmask_scr, sa_scr, sb_scr, acc_scr, m_scr, *, tk, lam_init):
    h = pl.program_id(1)
    qi = pl.program_id(2)
    slope = slopes_ref[h]
    dh = HEAD_DIM_DIFF
    dv = LANES
    n_chunks = vt_scr.shape[0]
    halves = q_ref.shape[0] // tk

    @pl.when(qi == 0)
    def _():
        for j in range(n_chunks):
            vt_scr[j, 0:dv, :] = v_ref[j * tk:(j + 1) * tk, :].astype(F32).T.astype(BF16)
            vt_scr[j, dv:dv + ONES_ROWS, :] = jnp.ones((ONES_ROWS, tk), BF16)
        kaug_scr[:, 0:LANES] = k_ref[...]
        koff = lax.broadcasted_iota(jnp.int32, (tk, LANES), 0)
        flane = lax.broadcasted_iota(jnp.int32, (tk, LANES), 1)
        feat = jnp.where(flane == 0, koff // POS_SPLIT, jnp.where(flane == 1, koff % POS_SPLIT, 0))
        feat = feat.astype(F32).astype(BF16)
        for j in range(n_chunks):
            kaug_scr[j * tk:(j + 1) * tk, LANES:2 * LANES] = feat
        kpos = lax.broadcasted_iota(jnp.int32, (tk, 2 * tk), 0)
        qpos = lax.broadcasted_iota(jnp.int32, (tk, 2 * tk), 1)
        qoff = jnp.where(qpos >= tk, qpos - tk, qpos)
        dmask_scr[...] = jnp.where(qoff >= kpos, 0.0, NEG_BIG)

    q = q_ref[...] * jnp.asarray(dh ** -0.5, BF16)
    lane = lax.broadcasted_iota(jnp.int32, q.shape, 1)
    zero = jnp.zeros_like(q)
    qf = jnp.where(lane == 0, slope * POS_SPLIT, jnp.where(lane == 1, slope, 0.0)).astype(BF16)
    q1 = jnp.concatenate([jnp.where(lane < dh, q, zero), qf], axis=1)
    q2 = jnp.concatenate([jnp.where(lane >= dh, q, zero), qf], axis=1)
    qq = jnp.concatenate([part[hq * tk:(hq + 1) * tk] for hq in range(halves) for part in (q1, q2)], axis=0)

    m_scr[...] = jnp.full(m_scr.shape, NEG_BIG, F32)
    acc_scr[...] = jnp.zeros(acc_scr.shape, F32)

    per_half = 2 * tk // QUERY_GROUP
    groups = [(hq, slice((hq * per_half + g) * QUERY_GROUP, (hq * per_half + g + 1) * QUERY_GROUP),
               slice(g * QUERY_GROUP, (g + 1) * QUERY_GROUP))
              for hq in range(halves) for g in range(per_half)]

    def score(kaug, ls):
        return lax.dot_general(kaug, qq[ls], (((1,), (1,)), ((), ())), preferred_element_type=F32)

    def step(j_cur, src, j_next, dst, first_half, masked_half, next_first_half=0):
        if dst is not None:
            kaug = kaug_scr[pl.ds(pl.multiple_of(j_next * tk, tk), tk), :]
        vt = vt_scr[j_cur]
        m_all = m_scr[...]
        acc_all = acc_scr[...]
        done = []
        for hq, ls, ms in groups:
            if dst is not None and hq >= next_first_half:
                dst[:, ls] = score(kaug, ls)
            if hq < first_half:
                continue
            cj = (-slope) * ((halves * qi + hq - j_cur) * tk).astype(F32)
            s = src[:, ls]
            if hq == masked_half:
                s = s + dmask_scr[:, ms]
            m_prev = m_all[:, ls]
            m_new = jnp.maximum(m_prev, jnp.max(s, axis=0, keepdims=True) + cj)
            p = jnp.exp(s - (m_new - cj))
            alpha = jnp.exp(m_prev - m_new)
            done.append((ls, m_new, alpha * acc_all[:, ls] + jnp.dot(vt, p.astype(BF16),
                                                                     preferred_element_type=F32)))
        for ls, m_new, acc_new in done:
            m_scr[:, ls] = m_new
            acc_scr[:, ls] = acc_new

    kaug0 = kaug_scr[0:tk, :]
    for _, ls, _ in groups:
        sa_scr[:, ls] = score(kaug0, ls)

    def pair(jj, carry):
        j = 2 * jj
        step(j, sa_scr, j + 1, sb_scr, 0, None)
        step(j + 1, sb_scr, j + 2, sa_scr, 0, None)
        return carry

    lax.fori_loop(0, qi, pair, 0)
    step(2 * qi, sa_scr, 2 * qi + 1, sb_scr, 0, 0, next_first_half=1)
    step(2 * qi + 1, sb_scr, None, None, 1, 1)

    lam = (jnp.exp(jnp.sum(lq1_ref[...] * lk1_ref[...], axis=-1, keepdims=True))
           - jnp.exp(jnp.sum(lq2_ref[...] * lk2_ref[...], axis=-1, keepdims=True)) + lam_init)
    acc = acc_scr[...]
    ot = acc[0:dv] / acc[dv:dv + 1]
    for hq in range(halves):
        base = hq * 2 * tk
        o = (ot[:, base:base + tk] - lam * ot[:, base + tk:base + 2 * tk]).T
        o = o * lax.rsqrt(jnp.mean(o * o, axis=-1, keepdims=True) + LN_EPS) * g_ref[...] * (1.0 - lam_init)
        o_ref[hq * tk:(hq + 1) * tk, :] = o.astype(o_ref.dtype)


def _attn(p3, lq1, lk1, lq2, lk2, g, lam_init, tk=512):
    b, s, _ = p3.shape
    h = N_HEADS_DIFF
    tq = 2 * tk
    slopes = jnp.asarray(2.0 ** (-8.0 * np.arange(1, h + 1) / h), dtype=F32)
    vec = pl.BlockSpec((1, HEAD_DIM_DIFF), lambda bi, hi, qi, sl: (0, 0))
    return pl.pallas_call(
        functools.partial(_attn_kernel, tk=tk, lam_init=lam_init),
        out_shape=jax.ShapeDtypeStruct((b, s, h * LANES), BF16),
        grid_spec=pltpu.PrefetchScalarGridSpec(
            num_scalar_prefetch=1,
            grid=(b, h, s // tq),
            in_specs=[pl.BlockSpec((None, tq, LANES), lambda bi, hi, qi, sl: (bi, qi, OFF_QA // LANES + hi)),
                      pl.BlockSpec((None, s, LANES), lambda bi, hi, qi, sl: (bi, 0, OFF_KA // LANES + hi)),
                      pl.BlockSpec((None, s, LANES), lambda bi, hi, qi, sl: (bi, 0, OFF_VA // LANES + hi)),
                      vec, vec, vec, vec,
                      pl.BlockSpec((1, LANES), lambda bi, hi, qi, sl: (0, 0))],
            out_specs=pl.BlockSpec((None, tq, LANES), lambda bi, hi, qi, sl: (bi, qi, hi)),
            scratch_shapes=[pltpu.VMEM((s, 2 * LANES), BF16),
                            pltpu.VMEM((s // tk, LANES + ONES_ROWS, tk), BF16),
                            pltpu.VMEM((tk, 2 * tk), F32),
                            pltpu.VMEM((tk, 2 * tq), F32),
                            pltpu.VMEM((tk, 2 * tq), F32),
                            pltpu.VMEM((LANES + ONES_ROWS, 2 * tq), F32),
                            pltpu.VMEM((1, 2 * tq), F32)]),
        compiler_params=_cparams(("arbitrary", "arbitrary", "arbitrary")),
    )(slopes, p3, p3, p3, lq1, lk1, lq2, lk2, g)


GLA_HEADS_PER_STEP = 4


def _gla_kernel(q_ref, k_ref, v_ref, gb_ref, la_ref, g_ref, o_ref, state_scr, *, tt):
    c = GLA_CHUNK
    dk, dv = KEY_DIM_GLA, VAL_DIM_GLA
    hp = state_scr.shape[0]
    n_c = tt // c
    work = [(hh, ci) for hh in range(hp) for ci in range(n_c)]

    @pl.when(pl.program_id(2) == 0)
    def _():
        state_scr[...] = jnp.zeros(state_scr.shape, F32)

    rr = lax.broadcasted_iota(jnp.int32, (c, c), 0)
    cr = lax.broadcasted_iota(jnp.int32, (c, c), 1)
    causal = cr <= rr
    tri = jnp.where(causal, 1.0, 0.0).astype(BF16)
    qscale = dk ** -0.5
    rows = lambda ci: slice(ci * c, (ci + 1) * c)
    kcols = lambda hh: slice(hh * dk, (hh + 1) * dk)
    vcols = lambda hh: slice(hh * dv, (hh + 1) * dv)

    cum = {}
    for hh, ci in work:
        la = la_ref[rows(ci), kcols(hh)]
        la_hi = la.astype(BF16)
        la_lo = (la - la_hi.astype(F32)).astype(BF16)
        cum2 = jnp.dot(tri, jnp.concatenate([la_hi, la_lo], axis=1), preferred_element_type=F32)
        cum[hh, ci] = cum2[:, :dk] + cum2[:, dk:]

    qe, q2, k2, kd, dec, v = {}, {}, {}, {}, {}, {}
    for hh, ci in work:
        cum_c = cum[hh, ci]
        last = cum_c[c - 1:c]
        mid = cum_c[c // 2:c // 2 + 1]
        q_c = q_ref[rows(ci), kcols(hh)].astype(F32) * qscale
        k_c = k_ref[rows(ci), kcols(hh)].astype(F32)
        qe[hh, ci] = (q_c * jnp.exp(cum_c)).astype(BF16)
        q2[hh, ci] = (q_c * jnp.exp(cum_c - mid)).astype(BF16)
        k2[hh, ci] = (k_c * jnp.exp(mid - cum_c)).astype(BF16)
        kd[hh, ci] = (k_c * jnp.exp(last - cum_c)).astype(BF16)
        dec[hh, ci] = jnp.exp(last)
        v[hh, ci] = v_ref[rows(ci), vcols(hh)]

    att = {w: lax.dot_general(q2[w], k2[w], (((1,), (1,)), ((), ())), preferred_element_type=F32)
           for w in work}
    o_intra = {w: jnp.dot(jnp.where(causal, att[w], 0.0).astype(BF16), v[w], preferred_element_type=F32)
               for w in work}
    kv = {w: jnp.dot(v[w].astype(F32).T.astype(BF16), kd[w], preferred_element_type=F32) for w in work}

    o_inter = {}
    for hh in range(hp):
        st = state_scr[hh]
        for ci in range(n_c):
            o_inter[hh, ci] = lax.dot_general(qe[hh, ci], st.astype(BF16), (((1,), (1,)), ((), ())),
                                              preferred_element_type=F32)
            st = st * dec[hh, ci] + kv[hh, ci]
        state_scr[hh] = st

    for hh, ci in work:
        o = o_inter[hh, ci] + o_intra[hh, ci]
        o = o * lax.rsqrt(jnp.mean(o * o, axis=-1, keepdims=True) + LN_EPS) * g_ref[...]
        gate = gb_ref[rows(ci), vcols(hh)].astype(F32)
        o_ref[rows(ci), vcols(hh)] = (o * (gate * _sigmoid(gate))).astype(o_ref.dtype)


def _gla(p3, la3, g, tt=512):
    b, s, _ = p3.shape
    hp = GLA_HEADS_PER_STEP
    dk, dv = KEY_DIM_GLA * hp, VAL_DIM_GLA * hp
    return pl.pallas_call(
        functools.partial(_gla_kernel, tt=tt),
        out_shape=jax.ShapeDtypeStruct((b, s, N_HEADS_GLA * VAL_DIM_GLA), BF16),
        grid=(b, N_HEADS_GLA // hp, s // tt),
        in_specs=[pl.BlockSpec((None, tt, dk), lambda bi, hi, ti: (bi, ti, OFF_QB // dk + hi)),
                  pl.BlockSpec((None, tt, dk), lambda bi, hi, ti: (bi, ti, OFF_KB // dk + hi)),
                  pl.BlockSpec((None, tt, dv), lambda bi, hi, ti: (bi, ti, OFF_VB // dv + hi)),
                  pl.BlockSpec((None, tt, dv), lambda bi, hi, ti: (bi, ti, OFF_GB // dv + hi)),
                  pl.BlockSpec((None, tt, dk), lambda bi, hi, ti: (bi, ti, hi)),
                  pl.BlockSpec((1, VAL_DIM_GLA), lambda bi, hi, ti: (0, 0))],
        out_specs=pl.BlockSpec((None, tt, dv), lambda bi, hi, ti: (bi, ti, hi)),
        scratch_shapes=[pltpu.VMEM((hp, VAL_DIM_GLA, KEY_DIM_GLA), F32)],
        compiler_params=_cparams(("arbitrary", "arbitrary", "arbitrary")),
    )(p3, p3, p3, p3, la3, g)


def _mix_kernel(oa_ref, ob_ref, gt_ref, x_ref, bg_ref, g1_ref, sh2_ref, sc2_ref, lng_ref, lnb_ref,
                wba_ref, wbb_ref, wo_ref, wr_ref, br_ref,
                x1_ref, u2_ref, lg_ref, *, alpha):
    d = x_ref.shape[-1]
    a = jnp.dot(oa_ref[...], wba_ref[...], preferred_element_type=F32)
    bm = jnp.dot(ob_ref[...], wbb_ref[...], preferred_element_type=F32)
    gates = _sigmoid(gt_ref[...].astype(F32) + bg_ref[...])
    mixed = gates[:, :d] * a + gates[:, d:] * bm
    y = jnp.dot(mixed.astype(BF16), wo_ref[...], preferred_element_type=F32)
    x1 = _standardize(alpha * x_ref[...] + g1_ref[0] * y) * lng_ref[...] + lnb_ref[...]
    x1_ref[...] = x1
    u2 = _standardize(x1) * (1.0 + sc2_ref[0]) + sh2_ref[0]
    _store_token_tiles(u2_ref, u2)
    uh = u2.astype(BF16)
    ul = (u2 - uh.astype(F32)).astype(BF16)
    hh = jnp.dot(uh, wr_ref[...], preferred_element_type=F32)
    lg_ref[...] = (hh[:, :LANES] + hh[:, LANES:]
                   + jnp.dot(ul, wr_ref[:, 0:LANES], preferred_element_type=F32) + br_ref[...])


def _mix(oa, ob, p, x2, bg, g1, sh2, sc2, lng, lnb, wba, wbb, wo, wr2, br, seq, alpha, tm=512):
    n, d = x2.shape
    tpb = seq // tm
    row = lambda i: (i, 0)
    const = lambda i: (0, 0)
    per_b = pl.BlockSpec((1, 1, d), lambda i: (i // tpb, 0, 0))
    return pl.pallas_call(
        functools.partial(_mix_kernel, alpha=alpha),
        out_shape=(jax.ShapeDtypeStruct((n, d), F32),
                   jax.ShapeDtypeStruct((n * (d // LANES), LANES), F32),
                   jax.ShapeDtypeStruct((n, LANES), F32)),
        grid=(n // tm,),
        in_specs=[pl.BlockSpec((tm, d), row), pl.BlockSpec((tm, d), row),
                  pl.BlockSpec((tm, 2 * d), lambda i: (i, OFF_GATES // (2 * d))),
                  pl.BlockSpec((tm, d), row),
                  pl.BlockSpec((1, 2 * d), const), per_b, per_b, per_b,
                  pl.BlockSpec((1, d), const), pl.BlockSpec((1, d), const),
                  pl.BlockSpec((d, d), const), pl.BlockSpec((d, d), const), pl.BlockSpec((d, d), const),
                  pl.BlockSpec((d, 2 * LANES), const),
                  pl.BlockSpec((1, LANES), const)],
        out_specs=(pl.BlockSpec((tm, d), row), pl.BlockSpec((tm * (d // LANES), LANES), row),
                   pl.BlockSpec((tm, LANES), row)),
        compiler_params=_cparams(("arbitrary",)),
    )(oa, ob, p, x2, bg, g1, sh2, sc2, lng, lnb, wba, wbb, wo, wr2, br)


def _route_kernel(lg_ref, ri_ref, rw_ref, cnt_ref, *, tm):
    @pl.when(pl.program_id(0) == 0)
    def _():
        cnt_ref[...] = jnp.zeros(cnt_ref.shape, F32)

    lg = lg_ref[...]
    lane = lax.broadcasted_iota(jnp.int32, lg.shape, 1)
    lanef = lane.astype(F32)
    far = float(LANES)
    is_g = lane < N_GROUPS
    gl = jnp.where(is_g, lg, NEG_BIG)
    gmax = jnp.max(gl, axis=-1, keepdims=True)
    gidx = jnp.min(jnp.where(gl == gmax, lanef, far), axis=-1, keepdims=True)
    gw = 1.0 / jnp.sum(jnp.where(is_g, jnp.exp(gl - gmax), 0.0), axis=-1, keepdims=True)
    lo = N_GROUPS + gidx * EXPERTS_PER_GROUP
    in_g = (lanef >= lo) & (lanef < lo + EXPERTS_PER_GROUP)
    el = jnp.where(in_g, lg, NEG_BIG)
    v1 = jnp.max(el, axis=-1, keepdims=True)
    i1 = jnp.min(jnp.where(in_g & (el == v1), lanef, far), axis=-1, keepdims=True)
    in_g2 = in_g & (lanef != i1)
    el2 = jnp.where(in_g2, lg, NEG_BIG)
    v2 = jnp.max(el2, axis=-1, keepdims=True)
    i2 = jnp.min(jnp.where(in_g2 & (el2 == v2), lanef, far), axis=-1, keepdims=True)
    t = jnp.exp(v2 - v1)
    w1 = gw / (1.0 + t)
    w2 = gw * t / (1.0 + t)

    oh1 = lanef == i1
    oh2 = lanef == i2
    oh = jnp.where(oh1 | oh2, 1.0, 0.0)
    r = lax.broadcasted_iota(jnp.int32, (tm, tm), 0)
    c = lax.broadcasted_iota(jnp.int32, (tm, tm), 1)
    lower = jnp.where(c < r, 1.0, 0.0).astype(BF16)
    base = jnp.dot(lower, oh.astype(BF16), preferred_element_type=F32) + cnt_ref[0:1, :]
    r1 = jnp.sum(jnp.where(oh1, base, 0.0), axis=-1, keepdims=True)
    r2 = jnp.sum(jnp.where(oh2, base, 0.0), axis=-1, keepdims=True)
    cnt_ref[...] = cnt_ref[...] + jnp.sum(oh, axis=0, keepdims=True)

    e1 = i1 - float(N_GROUPS)
    e2 = i2 - float(N_GROUPS)
    ri = jnp.where(lane == 0, e1, jnp.where(lane == 1, e2, jnp.where(lane == 2, r1, jnp.where(lane == 3, r2, 0.0))))
    ri_ref[...] = ri.astype(jnp.int32)
    rw_ref[...] = jnp.where(lane == 0, w1, jnp.where(lane == 1, w2, 0.0))


def _route(lg, tm=512):
    n = lg.shape[0]
    row = lambda i: (i, 0)
    return pl.pallas_call(
        functools.partial(_route_kernel, tm=tm),
        out_shape=(jax.ShapeDtypeStruct((n, LANES), jnp.int32),
                   jax.ShapeDtypeStruct((n, LANES), F32),
                   jax.ShapeDtypeStruct((8, LANES), F32)),
        grid=(n // tm,),
        in_specs=[pl.BlockSpec((tm, LANES), row)],
        out_specs=(pl.BlockSpec((tm, LANES), row), pl.BlockSpec((tm, LANES), row),
                   pl.BlockSpec((8, LANES), lambda i: (0, 0))),
        compiler_params=_cparams(("arbitrary",)),
    )(lg)


MOE_ROWS = 256


def _expert_kernel(be_ref, nu_ref, tok_a_ref, tok_b_ref, dst_ref, u_ref, wg_ref, wu_ref, wd_ref, y_ref,
                   xb0, xb1, yb0, yb1, wg_scr, wu_scr, wd_scr, gsem, ssem):
    i = pl.program_id(0)
    n_used = nu_ref[0]
    blk = tok_a_ref.shape[-1]
    xbs, ybs = (xb0, xb1), (yb0, yb1)

    def tile(ref, t):
        return ref.at[pl.ds(pl.multiple_of(t * ROW_TILES, ROW_TILES), ROW_TILES)]

    def gather_issue(tok_ref, xb, sem):
        for r in range(blk):
            pltpu.make_async_copy(tile(u_ref, tok_ref[0, 0, r]), tile(xb, r), sem).start()

    def gather_wait(xb, sem):
        pltpu.make_async_copy(u_ref.at[pl.ds(0, blk * ROW_TILES)], xb, sem).wait()

    def scatter_issue(yb, sem):
        for r in range(blk):
            pltpu.make_async_copy(tile(yb, r), tile(y_ref, dst_ref[0, 0, r]), sem).start()

    def scatter_wait(yb, sem):
        pltpu.make_async_copy(yb, y_ref.at[pl.ds(0, blk * ROW_TILES)], sem).wait()

    def load_weights():
        @pl.when((i == 0) | (be_ref[i] != be_ref[jnp.maximum(i - 1, 0)]))
        def _():
            wg_scr[...] = wg_ref[0].astype(BF16)
            wu_scr[...] = wu_ref[0].astype(BF16)
            wd_scr[...] = wd_ref[0].astype(BF16)

    def compute(xb, yb):
        x = _load_token_tiles(xb).astype(BF16)
        g = jnp.dot(x, wg_scr[...], preferred_element_type=F32)
        u = jnp.dot(x, wu_scr[...], preferred_element_type=F32)
        hid = (g * _sigmoid(g) * u).astype(BF16)
        _store_token_tiles(yb, jnp.dot(hid, wd_scr[...], preferred_element_type=F32))

    @pl.when(i == 0)
    def _():
        gather_issue(tok_a_ref, xb0, gsem.at[0])
        gather_wait(xb0, gsem.at[0])
        load_weights()
        gather_issue(tok_b_ref, xb1, gsem.at[1])
        compute(xb0, yb0)

    for p in range(2):
        cur, oth = p, 1 - p

        @pl.when((i >= 1) & (i < n_used) & (i % 2 == p))
        def _():
            gather_wait(xbs[cur], gsem.at[cur])

            @pl.when(i >= 2)
            def _():
                scatter_wait(ybs[cur], ssem.at[cur])

            load_weights()
            gather_issue(tok_b_ref, xbs[oth], gsem.at[oth])
            scatter_issue(ybs[oth], ssem.at[oth])
            compute(xbs[cur], ybs[cur])

        @pl.when((i == n_used) & (i % 2 == p))
        def _():
            gather_wait(xbs[cur], gsem.at[cur])

            @pl.when(i >= 2)
            def _():
                scatter_wait(ybs[cur], ssem.at[cur])

            scatter_issue(ybs[oth], ssem.at[oth])

        @pl.when((i == n_used + 1) & (i % 2 == p))
        def _():
            scatter_wait(ybs[cur], ssem.at[cur])


def _experts(blk_e, n_used, src_tok3, dst_row3, u2t, wg, wu, wd, layer, n_rows):
    d, de = wg.shape[-2:]
    blk = MOE_ROWS
    rt = d // LANES
    n_steps = blk_e.shape[0]
    wspec = lambda r, c: pl.BlockSpec((None, 1, r, c), lambda i, be, nu: (layer, be[i], 0, 0))
    ispec = lambda f: pl.BlockSpec((1, 1, blk), lambda i, be, nu: (f(i, nu), 0, 0), memory_space=pltpu.SMEM)
    return pl.pallas_call(
        _expert_kernel,
        out_shape=jax.ShapeDtypeStruct((n_rows * rt, LANES), F32),
        grid_spec=pltpu.PrefetchScalarGridSpec(
            num_scalar_prefetch=2,
            grid=(n_steps,),
            in_specs=[ispec(lambda i, nu: jnp.minimum(i, nu[0] - 1)),
                      ispec(lambda i, nu: jnp.minimum(i + 1, nu[0] - 1)),
                      ispec(lambda i, nu: jnp.clip(i - 1, 0, nu[0] - 1)),
                      pl.BlockSpec(memory_space=pl.ANY),
                      wspec(d, de), wspec(d, de), wspec(de, d)],
            out_specs=pl.BlockSpec(memory_space=pl.ANY),
            scratch_shapes=[pltpu.VMEM((blk * rt, LANES), F32), pltpu.VMEM((blk * rt, LANES), F32),
                            pltpu.VMEM((blk * rt, LANES), F32), pltpu.VMEM((blk * rt, LANES), F32),
                            pltpu.VMEM((d, de), BF16), pltpu.VMEM((d, de), BF16),
                            pltpu.VMEM((de, d), BF16),
                            pltpu.SemaphoreType.DMA((2,)), pltpu.SemaphoreType.DMA((2,))]),
        compiler_params=_cparams(("arbitrary",), has_side_effects=True),
    )(blk_e, n_used, src_tok3, src_tok3, dst_row3, u2t, wg, wu, wd)


def _combine_kernel(x1_ref, y1_ref, y2_ref, rw_ref, g2_ref, lng_ref, lnb_ref, o_ref, *, alpha):
    rw = rw_ref[...]
    y = rw[:, 0:1] * _load_token_tiles(y1_ref) + rw[:, 1:2] * _load_token_tiles(y2_ref)
    o_ref[...] = _standardize(alpha * x1_ref[...] + g2_ref[0] * y) * lng_ref[...] + lnb_ref[...]


def _combine(x1, y_tok, rw, g2, lng, lnb, seq, alpha, tm=512):
    n, d = x1.shape
    tpb = seq // tm
    const = lambda i: (0, 0)
    return pl.pallas_call(
        functools.partial(_combine_kernel, alpha=alpha),
        out_shape=jax.ShapeDtypeStruct((n, d), F32),
        grid=(n // tm,),
        in_specs=[pl.BlockSpec((tm, d), lambda i: (i, 0)),
                  pl.BlockSpec((tm * (d // LANES), LANES), lambda i: (i, 0)),
                  pl.BlockSpec((tm * (d // LANES), LANES), lambda i: (n // tm + i, 0)),
                  pl.BlockSpec((tm, LANES), lambda i: (i, 0)),
                  pl.BlockSpec((1, 1, d), lambda i: (i // tpb, 0, 0)),
                  pl.BlockSpec((1, d), const), pl.BlockSpec((1, d), const)],
        out_specs=pl.BlockSpec((tm, d), lambda i: (i, 0)),
        compiler_params=_cparams(("arbitrary",)),
    )(x1, y_tok, y_tok, rw, g2, lng, lnb)


ID_SPLIT = 256


def _invert_kernel(ri_ref, ps_ref, inv_ref, *, n):
    i = pl.program_id(0)
    tm = ri_ref.shape[0]
    n_hi = inv_ref.shape[0]

    @pl.when(i == 0)
    def _():
        inv_ref[...] = jnp.zeros(inv_ref.shape, F32)

    ri = ri_ref[...].astype(F32)
    lane = lax.broadcasted_iota(jnp.int32, (tm, LANES), 1)
    lanef = lane.astype(F32)
    tok = (lax.broadcasted_iota(jnp.int32, (tm, 1), 0) + i * tm).astype(F32)
    cols = []
    onehot_lo = []
    for j in range(2):
        e = ri[:, j:j + 1]
        row = jnp.sum(jnp.where(lanef == e, ps_ref[...], 0.0), axis=-1, keepdims=True) + ri[:, 2 + j:3 + j]
        hi = jnp.floor(row * (1.0 / LANES))
        lo = row - hi * LANES
        ident = tok + float(j * n + 1)
        id_hi = jnp.floor(ident * (1.0 / ID_SPLIT))
        cols += [hi, id_hi, ident - id_hi * ID_SPLIT]
        onehot_lo.append(jnp.where(lanef == lo, 1.0, 0.0).astype(BF16))
    packed = jnp.zeros((tm, LANES), F32)
    for k, col in enumerate(cols):
        packed = jnp.where(lane == k, col, packed)
    rows = packed.T
    hsel = lax.broadcasted_iota(jnp.int32, (n_hi, tm), 0).astype(F32)
    lhs_hi, lhs_lo = [], []
    for j in range(2):
        hit = hsel == rows[3 * j:3 * j + 1]
        lhs_hi.append(jnp.where(hit, rows[3 * j + 1:3 * j + 2], 0.0).astype(BF16))
        lhs_lo.append(jnp.where(hit, rows[3 * j + 2:3 * j + 3], 0.0).astype(BF16))
    rhs = jnp.concatenate(onehot_lo, axis=0)
    inv_ref[...] += (float(ID_SPLIT) * jnp.dot(jnp.concatenate(lhs_hi, axis=1), rhs, preferred_element_type=F32)
                     + jnp.dot(jnp.concatenate(lhs_lo, axis=1), rhs, preferred_element_type=F32))


def _invert(ri, pstarts, n_rows, tm=1024):
    n = ri.shape[0]
    ps = jnp.pad(pstarts.astype(F32), (0, LANES - pstarts.shape[0])).reshape(1, LANES)
    return pl.pallas_call(
        functools.partial(_invert_kernel, n=n),
        out_shape=jax.ShapeDtypeStruct((n_rows // LANES, LANES), F32),
        grid=(n // tm,),
        in_specs=[pl.BlockSpec((tm, LANES), lambda i: (i, 0)),
                  pl.BlockSpec((1, LANES), lambda i: (0, 0))],
        out_specs=pl.BlockSpec((n_rows // LANES, LANES), lambda i: (0, 0)),
        compiler_params=_cparams(("arbitrary",)),
    )(ri, ps)


def _moe_plan(ri, cnt, n):
    blk = MOE_ROWS
    n_rows = 2 * n + N_EXPERTS * blk
    n_blocks = n_rows // blk
    n_steps = n_blocks + 2
    counts = cnt[0, N_GROUPS:N_GROUPS + N_EXPERTS].astype(jnp.int32)
    padded = ((counts + blk - 1) // blk) * blk
    pends = jnp.cumsum(padded)
    pstarts = pends - padded
    cstarts = jnp.cumsum(counts) - counts
    blk_start = jnp.arange(n_steps, dtype=jnp.int32) * blk
    blk_e = jnp.minimum(jnp.sum(blk_start[:, None] >= pends[None, :], axis=1), N_EXPERTS - 1).astype(jnp.int32)
    n_used = (pends[-1:] // blk).astype(jnp.int32)
    prow = jnp.arange(n_rows, dtype=jnp.int32)
    pe = jnp.repeat(blk_e[:n_blocks], blk)
    spare = 2 * n + prow - (cstarts[pe] + jnp.minimum(prow - pstarts[pe], counts[pe]))
    inv = _invert(ri, pstarts, n_rows).reshape(-1).astype(jnp.int32)
    dst_row = jnp.where(inv > 0, inv - 1, spare)
    src_tok = dst_row % n
    return blk_e, n_used, src_tok.reshape(n_blocks, 1, blk), dst_row.reshape(n_blocks, 1, blk)


def kernel(x, c, w_ada, b_ada, w_in, b_gates, w_alpha, b_alpha, lambda_q1, lambda_k1, lambda_q2, lambda_k2, diff_norm_g, gla_norm_g, w_branch_a, w_branch_b, w_out, ln1_g, ln1_b, w_router_g, b_router_g, w_router_e, b_router_e, w_gate_e, w_up_e, w_down_e, ln2_g, ln2_b):
    b, s, d = x.shape
    depth = w_ada.shape[0]
    n = b * s
    alpha = (2.0 * depth) ** 0.25

    ada = _ada(c, w_ada, b_ada)
    x2 = x.reshape(n, d)
    for l in range(depth):
        sh1, sc1, g1, sh2, sc2, g2 = [ada[l, :, i * d:(i + 1) * d].reshape(b, 1, d) for i in range(6)]
        wl = w_in[l]
        w_att = wl[:, :OFF_GATES].astype(BF16)
        w_gates = wl[:, OFF_GATES + GLA_RANK:].astype(BF16)
        w_ab = jnp.pad(wl[:, OFF_GATES:OFF_GATES + GLA_RANK], ((0, 0), (0, LANES - GLA_RANK))).astype(BF16)
        wal = jnp.pad(w_alpha[l], ((0, LANES - GLA_RANK), (0, 0))).astype(BF16)
        p, la = _inproj(x2, sh1, sc1, w_att, w_gates, w_ab, wal, b_alpha[l].reshape(1, -1), s)
        p3 = p.reshape(b, s, W_MAIN)

        lam_init = 0.8 - 0.6 * math.exp(-0.3 * l)
        oa = _attn(p3, lambda_q1[l].reshape(1, -1), lambda_k1[l].reshape(1, -1),
                   lambda_q2[l].reshape(1, -1), lambda_k2[l].reshape(1, -1),
                   diff_norm_g[l].reshape(1, -1), lam_init)
        ob = _gla(p3, la.reshape(b, s, -1), gla_norm_g[l].reshape(1, -1))

        wr = jnp.pad(jnp.concatenate([w_router_g[l], w_router_e[l]], axis=1),
                     ((0, 0), (0, LANES - N_GROUPS - N_EXPERTS)))
        wrh = wr.astype(BF16)
        wr2 = jnp.concatenate([wrh, (wr - wrh.astype(F32)).astype(BF16)], axis=1)
        br = jnp.pad(jnp.concatenate([b_router_g[l], b_router_e[l]]),
                     (0, LANES - N_GROUPS - N_EXPERTS)).reshape(1, LANES)
        x1, u2, lg = _mix(oa.reshape(n, d), ob.reshape(n, d), p, x2, b_gates[l].reshape(1, -1),
                          g1, sh2, sc2, ln1_g[l].reshape(1, d), ln1_b[l].reshape(1, d),
                          w_branch_a[l].astype(BF16), w_branch_b[l].astype(BF16), w_out[l].astype(BF16),
                          wr2, br, s, alpha)

        ri, rw, cnt = _route(lg)
        blk_e, n_used, src_tok3, dst_row3 = _moe_plan(ri, cnt, n)
        y_tok = _experts(blk_e, n_used, src_tok3, dst_row3, u2, w_gate_e, w_up_e, w_down_e, l,
                         2 * n + N_EXPERTS * MOE_ROWS)
        x2 = _combine(x1, y_tok, rw, g2, ln2_g[l].reshape(1, d), ln2_b[l].reshape(1, d), s, alpha)
    return x2.reshape(b, s, d)
```

```python
import functools
import math

import jax
import jax.numpy as jnp
import numpy as np
from jax import lax
from jax.experimental import pallas as pl
from jax.experimental.pallas import tpu as pltpu

F32 = jnp.float32
BF16 = jnp.bfloat16

N_HEADS_DIFF = 8
HEAD_DIM_DIFF = 64
N_HEADS_GLA = 4
KEY_DIM_GLA = 128
VAL_DIM_GLA = 256
GLA_RANK = 16
GLA_TAU = 16.0
GLA_CHUNK = 64
N_GROUPS = 4
EXPERTS_PER_GROUP = 8
N_EXPERTS = N_GROUPS * EXPERTS_PER_GROUP
LN_EPS = 1e-5
LANES = 128
NEG_BIG = -1e30

OFF_QA, OFF_KA, OFF_VA = 0, 1024, 2048
OFF_QB, OFF_KB, OFF_VB, OFF_GB, OFF_GATES = 3072, 3584, 4096, 5120, 6144
W_MAIN = 8192

VMEM_LIMIT = 56 * 1024 * 1024


def _cparams(sem, **kw):
    return pltpu.CompilerParams(dimension_semantics=sem, vmem_limit_bytes=VMEM_LIMIT, **kw)


def _standardize(x):
    mu = jnp.mean(x, axis=-1, keepdims=True)
    xc = x - mu
    var = jnp.mean(xc * xc, axis=-1, keepdims=True)
    return xc * lax.rsqrt(var + LN_EPS)


def _sigmoid(x):
    return 1.0 / (1.0 + jnp.exp(-x))


ROW_TILES = 8


def _store_token_tiles(ref, x):
    t = x.shape[0]
    for s in range(ROW_TILES):
        ref[pl.ds(s, t, stride=ROW_TILES), :] = x[:, s * LANES:(s + 1) * LANES]


def _load_token_tiles(ref):
    t = ref.shape[0] // ROW_TILES
    return jnp.concatenate([ref[pl.ds(s, t, stride=ROW_TILES), :] for s in range(ROW_TILES)], axis=1)


def _ada_kernel(c_ref, w_ref, b_ref, o_ref):
    c = c_ref[...]
    cond = c * _sigmoid(c)
    o_ref[0] = jnp.dot(cond, w_ref[0], preferred_element_type=F32,
                       precision=lax.Precision.HIGHEST) + b_ref[0]


def _ada(c, w_ada, b_ada):
    depth, d, d6 = w_ada.shape
    b = c.shape[0]
    return pl.pallas_call(
        _ada_kernel,
        out_shape=jax.ShapeDtypeStruct((depth, b, d6), F32),
        grid=(depth, d6 // d),
        in_specs=[pl.BlockSpec((b, d), lambda l, j: (0, 0)),
                  pl.BlockSpec((1, d, d), lambda l, j: (l, 0, j)),
                  pl.BlockSpec((1, 1, d), lambda l, j: (l, 0, j))],
        out_specs=pl.BlockSpec((1, b, d), lambda l, j: (l, 0, j)),
        compiler_params=_cparams(("arbitrary", "arbitrary")),
    )(c, w_ada, b_ada.reshape(depth, 1, d6))


def _prep_tile(x, sh_ref, sc_ref, wab_ref, wal_ref, bal_ref, la_ref, u_scr):
    u = _standardize(x) * (1.0 + sc_ref[0]) + sh_ref[0]
    ub = u.astype(BF16)
    u_scr[...] = ub
    ab = jnp.dot(ub, wab_ref[...], preferred_element_type=F32)
    pre = jnp.dot(ab.astype(BF16), wal_ref[...], preferred_element_type=F32) + bal_ref[...]
    la_ref[...] = (jnp.minimum(pre, 0.0) - jnp.log(1.0 + jnp.exp(-jnp.abs(pre)))) * (1.0 / GLA_TAU)


def _inproj_kernel(x_ref, sh_ref, sc_ref, wa_ref, wg_ref, wab_ref, wal_ref, bal_ref,
                   p_ref, la_ref, u_scr, *, n_a):
    j = pl.program_id(1)

    @pl.when(j == 0)
    def _():
        _prep_tile(x_ref[...], sh_ref, sc_ref, wab_ref, wal_ref, bal_ref, la_ref, u_scr)

    @pl.when(j < n_a)
    def _():
        p_ref[...] = jnp.dot(u_scr[...], wa_ref[...], preferred_element_type=F32).astype(BF16)

    @pl.when(j >= n_a)
    def _():
        p_ref[...] = jnp.dot(u_scr[...], wg_ref[...], preferred_element_type=F32).astype(BF16)


def _inproj(x2, sh, sc, w_att, w_gates, w_ab, w_alpha, b_alpha, seq, tm=1024, tn=1024):
    n, d = x2.shape
    tpb = seq // tm
    wq = w_alpha.shape[1]
    n_a = w_att.shape[1] // tn
    return pl.pallas_call(
        functools.partial(_inproj_kernel, n_a=n_a),
        out_shape=(jax.ShapeDtypeStruct((n, W_MAIN), BF16),
                   jax.ShapeDtypeStruct((n, wq), F32)),
        grid=(n // tm, W_MAIN // tn),
        in_specs=[pl.BlockSpec((tm, d), lambda i, j: (i, 0)),
                  pl.BlockSpec((1, 1, d), lambda i, j: (i // tpb, 0, 0)),
                  pl.BlockSpec((1, 1, d), lambda i, j: (i // tpb, 0, 0)),
                  pl.BlockSpec((d, tn), lambda i, j: (0, jnp.minimum(j, n_a - 1))),
                  pl.BlockSpec((d, tn), lambda i, j: (0, jnp.maximum(j - n_a, 0))),
                  pl.BlockSpec((d, LANES), lambda i, j: (0, 0)),
                  pl.BlockSpec((LANES, wq), lambda i, j: (0, 0)),
                  pl.BlockSpec((1, wq), lambda i, j: (0, 0))],
        out_specs=(pl.BlockSpec((tm, tn), lambda i, j: (i, j)),
                   pl.BlockSpec((tm, wq), lambda i, j: (i, 0))),
        scratch_shapes=[pltpu.VMEM((tm, d), BF16)],
        compiler_params=_cparams(("arbitrary", "arbitrary")),
    )(x2, sh, sc, w_att, w_gates, w_ab, w_alpha, b_alpha)


ONES_ROWS = 16


QUERY_GROUP = 256
POS_SPLIT = 32


def _attn_kernel(slopes_ref, q_ref, k_ref, v_ref, lq1_ref, lk1_ref, lq2_ref, lk2_ref, g_ref,
                 o_ref, kaug_scr, vt_scr, dmask_scr, sa_scr, sb_scr, acc_scr, m_scr, *, tk, lam_init):
    h = pl.program_id(1)
    qi = pl.program_id(2)
    slope = slopes_ref[h]
    dh = HEAD_DIM_DIFF
    dv = LANES
    n_chunks = vt_scr.shape[0]
    halves = q_ref.shape[0] // tk

    @pl.when(qi == 0)
    def _():
        for j in range(n_chunks):
            vt_scr[j, 0:dv, :] = v_ref[j * tk:(j + 1) * tk, :].astype(F32).T.astype(BF16)
            vt_scr[j, dv:dv + ONES_ROWS, :] = jnp.ones((ONES_ROWS, tk), BF16)
        kaug_scr[:, 0:LANES] = k_ref[...]
        koff = lax.broadcasted_iota(jnp.int32, (tk, LANES), 0)
        flane = lax.broadcasted_iota(jnp.int32, (tk, LANES), 1)
        feat = jnp.where(flane == 0, koff // POS_SPLIT, jnp.where(flane == 1, koff % POS_SPLIT, 0))
        feat = feat.astype(F32).astype(BF16)
        for j in range(n_chunks):
            kaug_scr[j * tk:(j + 1) * tk, LANES:2 * LANES] = feat
        kpos = lax.broadcasted_iota(jnp.int32, (tk, 2 * tk), 0)
        qpos = lax.broadcasted_iota(jnp.int32, (tk, 2 * tk), 1)
        qoff = jnp.where(qpos >= tk, qpos - tk, qpos)
        dmask_scr[...] = jnp.where(qoff >= kpos, 0.0, NEG_BIG)

    q = q_ref[...] * jnp.asarray(dh ** -0.5, BF16)
    lane = lax.broadcasted_iota(jnp.int32, q.shape, 1)
    zero = jnp.zeros_like(q)
    qf = jnp.where(lane == 0, slope * POS_SPLIT, jnp.where(lane == 1, slope, 0.0)).astype(BF16)
    q1 = jnp.concatenate([jnp.where(lane < dh, q, zero), qf], axis=1)
    q2 = jnp.concatenate([jnp.where(lane >= dh, q, zero), qf], axis=1)
    qq = jnp.concatenate([part[hq * tk:(hq + 1) * tk] for hq in range(halves) for part in (q1, q2)], axis=0)

    m_scr[...] = jnp.full(m_scr.shape, NEG_BIG, F32)
    acc_scr[...] = jnp.zeros(acc_scr.shape, F32)

    per_half = 2 * tk // QUERY_GROUP
    groups = [(hq, slice((hq * per_half + g) * QUERY_GROUP, (hq * per_half + g + 1) * QUERY_GROUP),
               slice(g * QUERY_GROUP, (g + 1) * QUERY_GROUP))
              for hq in range(halves) for g in range(per_half)]

    def score(kaug, ls):
        return lax.dot_general(kaug, qq[ls], (((1,), (1,)), ((), ())), preferred_element_type=F32)

    def step(j_cur, src, j_next, dst, first_half, masked_half, next_first_half=0):
        if dst is not None:
            kaug = kaug_scr[pl.ds(pl.multiple_of(j_next * tk, tk), tk), :]
        vt = vt_scr[j_cur]
        m_all = m_scr[...]
        acc_all = acc_scr[...]
        done = []
        for hq, ls, ms in groups:
            if dst is not None and hq >= next_first_half:
                dst[:, ls] = score(kaug, ls)
            if hq < first_half:
                continue
            cj = (-slope) * ((halves * qi + hq - j_cur) * tk).astype(F32)
            s = src[:, ls]
            if hq == masked_half:
                s = s + dmask_scr[:, ms]
            m_prev = m_all[:, ls]
            m_new = jnp.maximum(m_prev, jnp.max(s, axis=0, keepdims=True) + cj)
            p = jnp.exp(s - (m_new - cj))
            alpha = jnp.exp(m_prev - m_new)
            done.append((ls, m_new, alpha * acc_all[:, ls] + jnp.dot(vt, p.astype(BF16),
                                                                     preferred_element_type=F32)))
        for ls, m_new, acc_new in done:
            m_scr[:, ls] = m_new
            acc_scr[:, ls] = acc_new

    kaug0 = kaug_scr[0:tk, :]
    for _, ls, _ in groups:
        sa_scr[:, ls] = score(kaug0, ls)

    def pair(jj, carry):
        j = 2 * jj
        step(j, sa_scr, j + 1, sb_scr, 0, None)
        step(j + 1, sb_scr, j + 2, sa_scr, 0, None)
        return carry

    lax.fori_loop(0, qi, pair, 0)
    step(2 * qi, sa_scr, 2 * qi + 1, sb_scr, 0, 0, next_first_half=1)
    step(2 * qi + 1, sb_scr, None, None, 1, 1)

    lam = (jnp.exp(jnp.sum(lq1_ref[...] * lk1_ref[...], axis=-1, keepdims=True))
           - jnp.exp(jnp.sum(lq2_ref[...] * lk2_ref[...], axis=-1, keepdims=True)) + lam_init)
    acc = acc_scr[...]
    ot = acc[0:dv] / acc[dv:dv + 1]
    for hq in range(halves):
        base = hq * 2 * tk
        o = (ot[:, base:base + tk] - lam * ot[:, base + tk:base + 2 * tk]).T
        o = o * lax.rsqrt(jnp.mean(o * o, axis=-1, keepdims=True) + LN_EPS) * g_ref[...] * (1.0 - lam_init)
        o_ref[hq * tk:(hq + 1) * tk, :] = o.astype(o_ref.dtype)


def _attn(p3, lq1, lk1, lq2, lk2, g, lam_init, tk=512):
    b, s, _ = p3.shape
    h = N_HEADS_DIFF
    tq = 2 * tk
    slopes = jnp.asarray(2.0 ** (-8.0 * np.arange(1, h + 1) / h), dtype=F32)
    vec = pl.BlockSpec((1, HEAD_DIM_DIFF), lambda bi, hi, qi, sl: (0, 0))
    return pl.pallas_call(
        functools.partial(_attn_kernel, tk=tk, lam_init=lam_init),
        out_shape=jax.ShapeDtypeStruct((b, s, h * LANES), BF16),
        grid_spec=pltpu.PrefetchScalarGridSpec(
            num_scalar_prefetch=1,
            grid=(b, h, s // tq),
            in_specs=[pl.BlockSpec((None, tq, LANES), lambda bi, hi, qi, sl: (bi, qi, OFF_QA // LANES + hi)),
                      pl.BlockSpec((None, s, LANES), lambda bi, hi, qi, sl: (bi, 0, OFF_KA // LANES + hi)),
                      pl.BlockSpec((None, s, LANES), lambda bi, hi, qi, sl: (bi, 0, OFF_VA // LANES + hi)),
                      vec, vec, vec, vec,
                      pl.BlockSpec((1, LANES), lambda bi, hi, qi, sl: (0, 0))],
            out_specs=pl.BlockSpec((None, tq, LANES), lambda bi, hi, qi, sl: (bi, qi, hi)),
            scratch_shapes=[pltpu.VMEM((s, 2 * LANES), BF16),
                            pltpu.VMEM((s // tk, LANES + ONES_ROWS, tk), BF16),
                            pltpu.VMEM((tk, 2 * tk), F32),
                            pltpu.VMEM((tk, 2 * tq), F32),
                            pltpu.VMEM((tk, 2 * tq), F32),
                            pltpu.VMEM((LANES + ONES_ROWS, 2 * tq), F32),
                            pltpu.VMEM((1, 2 * tq), F32)]),
        compiler_params=_cparams(("arbitrary", "arbitrary", "arbitrary")),
    )(slopes, p3, p3, p3, lq1, lk1, lq2, lk2, g)


GLA_HEADS_PER_STEP = 4


def _gla_kernel(q_ref, k_ref, v_ref, gb_ref, la_ref, g_ref, o_ref, state_scr, *, tt):
    c = GLA_CHUNK
    dk, dv = KEY_DIM_GLA, VAL_DIM_GLA
    hp = state_scr.shape[0]
    n_c = tt // c
    work = [(hh, ci) for hh in range(hp) for ci in range(n_c)]

    @pl.when(pl.program_id(2) == 0)
    def _():
        state_scr[...] = jnp.zeros(state_scr.shape, F32)

    rr = lax.broadcasted_iota(jnp.int32, (c, c), 0)
    cr = lax.broadcasted_iota(jnp.int32, (c, c), 1)
    causal = cr <= rr
    tri = jnp.where(causal, 1.0, 0.0).astype(BF16)
    qscale = dk ** -0.5
    rows = lambda ci: slice(ci * c, (ci + 1) * c)
    kcols = lambda hh: slice(hh * dk, (hh + 1) * dk)
    vcols = lambda hh: slice(hh * dv, (hh + 1) * dv)

    cum = {}
    for hh, ci in work:
        la = la_ref[rows(ci), kcols(hh)]
        la_hi = la.astype(BF16)
        la_lo = (la - la_hi.astype(F32)).astype(BF16)
        cum2 = jnp.dot(tri, jnp.concatenate([la_hi, la_lo], axis=1), preferred_element_type=F32)
        cum[hh, ci] = cum2[:, :dk] + cum2[:, dk:]

    qe, q2, k2, kd, dec, v = {}, {}, {}, {}, {}, {}
    for hh, ci in work:
        cum_c = cum[hh, ci]
        last = cum_c[c - 1:c]
        mid = cum_c[c // 2:c // 2 + 1]
        q_c = q_ref[rows(ci), kcols(hh)].astype(F32) * qscale
        k_c = k_ref[rows(ci), kcols(hh)].astype(F32)
        qe[hh, ci] = (q_c * jnp.exp(cum_c)).astype(BF16)
        q2[hh, ci] = (q_c * jnp.exp(cum_c - mid)).astype(BF16)
        k2[hh, ci] = (k_c * jnp.exp(mid - cum_c)).astype(BF16)
        kd[hh, ci] = (k_c * jnp.exp(last - cum_c)).astype(BF16)
        dec[hh, ci] = jnp.exp(last)
        v[hh, ci] = v_ref[rows(ci), vcols(hh)]

    att = {w: lax.dot_general(q2[w], k2[w], (((1,), (1,)), ((), ())), preferred_element_type=F32)
           for w in work}
    o_intra = {w: jnp.dot(jnp.where(causal, att[w], 0.0).astype(BF16), v[w], preferred_element_type=F32)
               for w in work}
    kv = {w: jnp.dot(v[w].astype(F32).T.astype(BF16), kd[w], preferred_element_type=F32) for w in work}

    o_inter = {}
    for hh in range(hp):
        st = state_scr[hh]
        for ci in range(n_c):
            o_inter[hh, ci] = lax.dot_general(qe[hh, ci], st.astype(BF16), (((1,), (1,)), ((), ())),
                                              preferred_element_type=F32)
            st = st * dec[hh, ci] + kv[hh, ci]
        state_scr[hh] = st

    for hh, ci in work:
        o = o_inter[hh, ci] + o_intra[hh, ci]
        o = o * lax.rsqrt(jnp.mean(o * o, axis=-1, keepdims=True) + LN_EPS) * g_ref[...]
        gate = gb_ref[rows(ci), vcols(hh)].astype(F32)
        o_ref[rows(ci), vcols(hh)] = (o * (gate * _sigmoid(gate))).astype(o_ref.dtype)


def _gla(p3, la3, g, tt=512):
    b, s, _ = p3.shape
    hp = GLA_HEADS_PER_STEP
    dk, dv = KEY_DIM_GLA * hp, VAL_DIM_GLA * hp
    return pl.pallas_call(
        functools.partial(_gla_kernel, tt=tt),
        out_shape=jax.ShapeDtypeStruct((b, s, N_HEADS_GLA * VAL_DIM_GLA), BF16),
        grid=(b, N_HEADS_GLA // hp, s // tt),
        in_specs=[pl.BlockSpec((None, tt, dk), lambda bi, hi, ti: (bi, ti, OFF_QB // dk + hi)),
                  pl.BlockSpec((None, tt, dk), lambda bi, hi, ti: (bi, ti, OFF_KB // dk + hi)),
                  pl.BlockSpec((None, tt, dv), lambda bi, hi, ti: (bi, ti, OFF_VB // dv + hi)),
                  pl.BlockSpec((None, tt, dv), lambda bi, hi, ti: (bi, ti, OFF_GB // dv + hi)),
                  pl.BlockSpec((None, tt, dk), lambda bi, hi, ti: (bi, ti, hi)),
                  pl.BlockSpec((1, VAL_DIM_GLA), lambda bi, hi, ti: (0, 0))],
        out_specs=pl.BlockSpec((None, tt, dv), lambda bi, hi, ti: (bi, ti, hi)),
        scratch_shapes=[pltpu.VMEM((hp, VAL_DIM_GLA, KEY_DIM_GLA), F32)],
        compiler_params=_cparams(("arbitrary", "arbitrary", "arbitrary")),
    )(p3, p3, p3, p3, la3, g)


def _mix_kernel(oa_ref, ob_ref, gt_ref, x_ref, bg_ref, g1_ref, sh2_ref, sc2_ref, lng_ref, lnb_ref,
                wba_ref, wbb_ref, wo_ref, wr_ref, br_ref,
                x1_ref, u2_ref, lg_ref, *, alpha):
    d = x_ref.shape[-1]
    a = jnp.dot(oa_ref[...], wba_ref[...], preferred_element_type=F32)
    bm = jnp.dot(ob_ref[...], wbb_ref[...], preferred_element_type=F32)
    gates = _sigmoid(gt_ref[...].astype(F32) + bg_ref[...])
    mixed = gates[:, :d] * a + gates[:, d:] * bm
    y = jnp.dot(mixed.astype(BF16), wo_ref[...], preferred_element_type=F32)
    x1 = _standardize(alpha * x_ref[...] + g1_ref[0] * y) * lng_ref[...] + lnb_ref[...]
    x1_ref[...] = x1
    u2 = _standardize(x1) * (1.0 + sc2_ref[0]) + sh2_ref[0]
    _store_token_tiles(u2_ref, u2)
    uh = u2.astype(BF16)
    ul = (u2 - uh.astype(F32)).astype(BF16)
    hh = jnp.dot(uh, wr_ref[...], preferred_element_type=F32)
    lg_ref[...] = (hh[:, :LANES] + hh[:, LANES:]
                   + jnp.dot(ul, wr_ref[:, 0:LANES], preferred_element_type=F32) + br_ref[...])


def _mix(oa, ob, p, x2, bg, g1, sh2, sc2, lng, lnb, wba, wbb, wo, wr2, br, seq, alpha, tm=512):
    n, d = x2.shape
    tpb = seq // tm
    row = lambda i: (i, 0)
    const = lambda i: (0, 0)
    per_b = pl.BlockSpec((1, 1, d), lambda i: (i // tpb, 0, 0))
    return pl.pallas_call(
        functools.partial(_mix_kernel, alpha=alpha),
        out_shape=(jax.ShapeDtypeStruct((n, d), F32),
                   jax.ShapeDtypeStruct((n * (d // LANES), LANES), F32),
                   jax.ShapeDtypeStruct((n, LANES), F32)),
        grid=(n // tm,),
        in_specs=[pl.BlockSpec((tm, d), row), pl.BlockSpec((tm, d), row),
                  pl.BlockSpec((tm, 2 * d), lambda i: (i, OFF_GATES // (2 * d))),
                  pl.BlockSpec((tm, d), row),
                  pl.BlockSpec((1, 2 * d), const), per_b, per_b, per_b,
                  pl.BlockSpec((1, d), const), pl.BlockSpec((1, d), const),
                  pl.BlockSpec((d, d), const), pl.BlockSpec((d, d), const), pl.BlockSpec((d, d), const),
                  pl.BlockSpec((d, 2 * LANES), const),
                  pl.BlockSpec((1, LANES), const)],
        out_specs=(pl.BlockSpec((tm, d), row), pl.BlockSpec((tm * (d // LANES), LANES), row),
                   pl.BlockSpec((tm, LANES), row)),
        compiler_params=_cparams(("arbitrary",)),
    )(oa, ob, p, x2, bg, g1, sh2, sc2, lng, lnb, wba, wbb, wo, wr2, br)


def _route_kernel(lg_ref, ri_ref, rw_ref, cnt_ref, *, tm):
    @pl.when(pl.program_id(0) == 0)
    def _():
        cnt_ref[...] = jnp.zeros(cnt_ref.shape, F32)

    lg = lg_ref[...]
    lane = lax.broadcasted_iota(jnp.int32, lg.shape, 1)
    lanef = lane.astype(F32)
    far = float(LANES)
    is_g = lane < N_GROUPS
    gl = jnp.where(is_g, lg, NEG_BIG)
    gmax = jnp.max(gl, axis=-1, keepdims=True)
    gidx = jnp.min(jnp.where(gl == gmax, lanef, far), axis=-1, keepdims=True)
    gw = 1.0 / jnp.sum(jnp.where(is_g, jnp.exp(gl - gmax), 0.0), axis=-1, keepdims=True)
    lo = N_GROUPS + gidx * EXPERTS_PER_GROUP
    in_g = (lanef >= lo) & (lanef < lo + EXPERTS_PER_GROUP)
    el = jnp.where(in_g, lg, NEG_BIG)
    v1 = jnp.max(el, axis=-1, keepdims=True)
    i1 = jnp.min(jnp.where(in_g & (el == v1), lanef, far), axis=-1, keepdims=True)
    in_g2 = in_g & (lanef != i1)
    el2 = jnp.where(in_g2, lg, NEG_BIG)
    v2 = jnp.max(el2, axis=-1, keepdims=True)
    i2 = jnp.min(jnp.where(in_g2 & (el2 == v2), lanef, far), axis=-1, keepdims=True)
    t = jnp.exp(v2 - v1)
    w1 = gw / (1.0 + t)
    w2 = gw * t / (1.0 + t)

    oh1 = lanef == i1
    oh2 = lanef == i2
    oh = jnp.where(oh1 | oh2, 1.0, 0.0)
    r = lax.broadcasted_iota(jnp.int32, (tm, tm), 0)
    c = lax.broadcasted_iota(jnp.int32, (tm, tm), 1)
    lower = jnp.where(c < r, 1.0, 0.0).astype(BF16)
    base = jnp.dot(lower, oh.astype(BF16), preferred_element_type=F32) + cnt_ref[0:1, :]
    r1 = jnp.sum(jnp.where(oh1, base, 0.0), axis=-1, keepdims=True)
    r2 = jnp.sum(jnp.where(oh2, base, 0.0), axis=-1, keepdims=True)
    cnt_ref[...] = cnt_ref[...] + jnp.sum(oh, axis=0, keepdims=True)

    e1 = i1 - float(N_GROUPS)
    e2 = i2 - float(N_GROUPS)
    ri = jnp.where(lane == 0, e1, jnp.where(lane == 1, e2, jnp.where(lane == 2, r1, jnp.where(lane == 3, r2, 0.0))))
    ri_ref[...] = ri.astype(jnp.int32)
    rw_ref[...] = jnp.where(lane == 0, w1, jnp.where(lane == 1, w2, 0.0))


def _route(lg, tm=512):
    n = lg.shape[0]
    row = lambda i: (i, 0)
    return pl.pallas_call(
        functools.partial(_route_kernel, tm=tm),
        out_shape=(jax.ShapeDtypeStruct((n, LANES), jnp.int32),
                   jax.ShapeDtypeStruct((n, LANES), F32),
                   jax.ShapeDtypeStruct((8, LANES), F32)),
        grid=(n // tm,),
        in_specs=[pl.BlockSpec((tm, LANES), row)],
        out_specs=(pl.BlockSpec((tm, LANES), row), pl.BlockSpec((tm, LANES), row),
                   pl.BlockSpec((8, LANES), lambda i: (0, 0))),
        compiler_params=_cparams(("arbitrary",)),
    )(lg)


MOE_ROWS = 256


def _expert_kernel(be_ref, nu_ref, tok_a_ref, tok_b_ref, dst_ref, u_ref, wg_ref, wu_ref, wd_ref, y_ref,
                   xb0, xb1, yb0, yb1, wg_scr, wu_scr, wd_scr, gsem, ssem):
    i = pl.program_id(0)
    n_used = nu_ref[0]
    blk = tok_a_ref.shape[-1]
    xbs, ybs = (xb0, xb1), (yb0, yb1)

    def tile(ref, t):
        return ref.at[pl.ds(pl.multiple_of(t * ROW_TILES, ROW_TILES), ROW_TILES)]

    def gather_issue(tok_ref, xb, sem):
        for r in range(blk):
            pltpu.make_async_copy(tile(u_ref, tok_ref[0, 0, r]), tile(xb, r), sem).start()

    def gather_wait(xb, sem):
        pltpu.make_async_copy(u_ref.at[pl.ds(0, blk * ROW_TILES)], xb, sem).wait()

    def scatter_issue(yb, sem):
        for r in range(blk):
            pltpu.make_async_copy(tile(yb, r), tile(y_ref, dst_ref[0, 0, r]), sem).start()

    def scatter_wait(yb, sem):
        pltpu.make_async_copy(yb, y_ref.at[pl.ds(0, blk * ROW_TILES)], sem).wait()

    def load_weights():
        @pl.when((i == 0) | (be_ref[i] != be_ref[jnp.maximum(i - 1, 0)]))
        def _():
            wg_scr[...] = wg_ref[0].astype(BF16)
            wu_scr[...] = wu_ref[0].astype(BF16)
            wd_scr[...] = wd_ref[0].astype(BF16)

    def compute(xb, yb):
        x = _load_token_tiles(xb).astype(BF16)
        g = jnp.dot(x, wg_scr[...], preferred_element_type=F32)
        u = jnp.dot(x, wu_scr[...], preferred_element_type=F32)
        hid = (g * _sigmoid(g) * u).astype(BF16)
        _store_token_tiles(yb, jnp.dot(hid, wd_scr[...], preferred_element_type=F32))

    @pl.when(i == 0)
    def _():
        gather_issue(tok_a_ref, xb0, gsem.at[0])
        gather_wait(xb0, gsem.at[0])
        load_weights()
        gather_issue(tok_b_ref, xb1, gsem.at[1])
        compute(xb0, yb0)

    for p in range(2):
        cur, oth = p, 1 - p

        @pl.when((i >= 1) & (i < n_used) & (i % 2 == p))
        def _():
            gather_wait(xbs[cur], gsem.at[cur])

            @pl.when(i >= 2)
            def _():
                scatter_wait(ybs[cur], ssem.at[cur])

            load_weights()
            gather_issue(tok_b_ref, xbs[oth], gsem.at[oth])
            scatter_issue(ybs[oth], ssem.at[oth])
            compute(xbs[cur], ybs[cur])

        @pl.when((i == n_used) & (i % 2 == p))
        def _():
            gather_wait(xbs[cur], gsem.at[cur])

            @pl.when(i >= 2)
            def _():
                scatter_wait(ybs[cur], ssem.at[cur])

            scatter_issue(ybs[oth], ssem.at[oth])

        @pl.when((i == n_used + 1) & (i % 2 == p))
        def _():
            scatter_wait(ybs[cur], ssem.at[cur])


def _experts(blk_e, n_used, src_tok3, dst_row3, u2t, wg, wu, wd, layer, n_rows):
    d, de = wg.shape[-2:]
    blk = MOE_ROWS
    rt = d // LANES
    n_steps = blk_e.shape[0]
    wspec = lambda r, c: pl.BlockSpec((None, 1, r, c), lambda i, be, nu: (layer, be[i], 0, 0))
    ispec = lambda f: pl.BlockSpec((1, 1, blk), lambda i, be, nu: (f(i, nu), 0, 0), memory_space=pltpu.SMEM)
    return pl.pallas_call(
        _expert_kernel,
        out_shape=jax.ShapeDtypeStruct((n_rows * rt, LANES), F32),
        grid_spec=pltpu.PrefetchScalarGridSpec(
            num_scalar_prefetch=2,
            grid=(n_steps,),
            in_specs=[ispec(lambda i, nu: jnp.minimum(i, nu[0] - 1)),
                      ispec(lambda i, nu: jnp.minimum(i + 1, nu[0] - 1)),
                      ispec(lambda i, nu: jnp.clip(i - 1, 0, nu[0] - 1)),
                      pl.BlockSpec(memory_space=pl.ANY),
                      wspec(d, de), wspec(d, de), wspec(de, d)],
            out_specs=pl.BlockSpec(memory_space=pl.ANY),
            scratch_shapes=[pltpu.VMEM((blk * rt, LANES), F32), pltpu.VMEM((blk * rt, LANES), F32),
                            pltpu.VMEM((blk * rt, LANES), F32), pltpu.VMEM((blk * rt, LANES), F32),
                            pltpu.VMEM((d, de), BF16), pltpu.VMEM((d, de), BF16),
                            pltpu.VMEM((de, d), BF16),
                            pltpu.SemaphoreType.DMA((2,)), pltpu.SemaphoreType.DMA((2,))]),
        compiler_params=_cparams(("arbitrary",), has_side_effects=True),
    )(blk_e, n_used, src_tok3, src_tok3, dst_row3, u2t, wg, wu, wd)


def _combine_kernel(x1_ref, y1_ref, y2_ref, rw_ref, g2_ref, lng_ref, lnb_ref, o_ref, *, alpha):
    rw = rw_ref[...]
    y = rw[:, 0:1] * _load_token_tiles(y1_ref) + rw[:, 1:2] * _load_token_tiles(y2_ref)
    o_ref[...] = _standardize(alpha * x1_ref[...] + g2_ref[0] * y) * lng_ref[...] + lnb_ref[...]


def _combine(x1, y_tok, rw, g2, lng, lnb, seq, alpha, tm=512):
    n, d = x1.shape
    tpb = seq // tm
    const = lambda i: (0, 0)
    return pl.pallas_call(
        functools.partial(_combine_kernel, alpha=alpha),
        out_shape=jax.ShapeDtypeStruct((n, d), F32),
        grid=(n // tm,),
        in_specs=[pl.BlockSpec((tm, d), lambda i: (i, 0)),
                  pl.BlockSpec((tm * (d // LANES), LANES), lambda i: (i, 0)),
                  pl.BlockSpec((tm * (d // LANES), LANES), lambda i: (n // tm + i, 0)),
                  pl.BlockSpec((tm, LANES), lambda i: (i, 0)),
                  pl.BlockSpec((1, 1, d), lambda i: (i // tpb, 0, 0)),
                  pl.BlockSpec((1, d), const), pl.BlockSpec((1, d), const)],
        out_specs=pl.BlockSpec((tm, d), lambda i: (i, 0)),
        compiler_params=_cparams(("arbitrary",)),
    )(x1, y_tok, y_tok, rw, g2, lng, lnb)


ID_SPLIT = 256


def _invert_kernel(ri_ref, ps_ref, inv_ref, *, n):
    i = pl.program_id(0)
    tm = ri_ref.shape[0]
    n_hi = inv_ref.shape[0]

    @pl.when(i == 0)
    def _():
        inv_ref[...] = jnp.zeros(inv_ref.shape, F32)

    ri = ri_ref[...].astype(F32)
    lane = lax.broadcasted_iota(jnp.int32, (tm, LANES), 1)
    lanef = lane.astype(F32)
    tok = (lax.broadcasted_iota(jnp.int32, (tm, 1), 0) + i * tm).astype(F32)
    cols = []
    onehot_lo = []
    for j in range(2):
        e = ri[:, j:j + 1]
        row = jnp.sum(jnp.where(lanef == e, ps_ref[...], 0.0), axis=-1, keepdims=True) + ri[:, 2 + j:3 + j]
        hi = jnp.floor(row * (1.0 / LANES))
        lo = row - hi * LANES
        ident = tok + float(j * n + 1)
        id_hi = jnp.floor(ident * (1.0 / ID_SPLIT))
        cols += [hi, id_hi, ident - id_hi * ID_SPLIT]
        onehot_lo.append(jnp.where(lanef == lo, 1.0, 0.0).astype(BF16))
    packed = jnp.zeros((tm, LANES), F32)
    for k, col in enumerate(cols):
        packed = jnp.where(lane == k, col, packed)
    rows = packed.T
    hsel = lax.broadcasted_iota(jnp.int32, (n_hi, tm), 0).astype(F32)
    lhs_hi, lhs_lo = [], []
    for j in range(2):
        hit = hsel == rows[3 * j:3 * j + 1]
        lhs_hi.append(jnp.where(hit, rows[3 * j + 1:3 * j + 2], 0.0).astype(BF16))
        lhs_lo.append(jnp.where(hit, rows[3 * j + 2:3 * j + 3], 0.0).astype(BF16))
    rhs = jnp.concatenate(onehot_lo, axis=0)
    inv_ref[...] += (float(ID_SPLIT) * jnp.dot(jnp.concatenate(lhs_hi, axis=1), rhs, preferred_element_type=F32)
                     + jnp.dot(jnp.concatenate(lhs_lo, axis=1), rhs, preferred_element_type=F32))


def _invert(ri, pstarts, n_rows, tm=1024):
    n = ri.shape[0]
    assert n % tm == 0 and n_rows % LANES == 0
    ps = jnp.pad(pstarts.astype(F32), (0, LANES - pstarts.shape[0])).reshape(1, LANES)
    return pl.pallas_call(
        functools.partial(_invert_kernel, n=n),
        out_shape=jax.ShapeDtypeStruct((n_rows // LANES, LANES), F32),
        grid=(n // tm,),
        in_specs=[pl.BlockSpec((tm, LANES), lambda i: (i, 0)),
                  pl.BlockSpec((1, LANES), lambda i: (0, 0))],
        out_specs=pl.BlockSpec((n_rows // LANES, LANES), lambda i: (0, 0)),
        compiler_params=_cparams(("arbitrary",)),
    )(ri, ps)


def _moe_plan(ri, cnt, n):
    blk = MOE_ROWS
    n_rows = 2 * n + N_EXPERTS * blk
    n_blocks = n_rows // blk
    n_steps = n_blocks + 2
    counts = cnt[0, N_GROUPS:N_GROUPS + N_EXPERTS].astype(jnp.int32)
    padded = ((counts + blk - 1) // blk) * blk
    pends = jnp.cumsum(padded)
    pstarts = pends - padded
    blk_start = jnp.arange(n_steps, dtype=jnp.int32) * blk
    blk_e = jnp.minimum(jnp.sum(blk_start[:, None] >= pends[None, :], axis=1), N_EXPERTS - 1).astype(jnp.int32)
    n_used = (pends[-1:] // blk).astype(jnp.int32)
    assigned_upto = jnp.cumsum(counts)[blk_e[:n_blocks]]
    spare = (2 * n + jnp.arange(n_rows, dtype=jnp.int32).reshape(n_blocks, blk) - assigned_upto[:, None]).reshape(-1)
    inv = _invert(ri, pstarts, n_rows).reshape(-1).astype(jnp.int32)
    dst_row = jnp.where(inv > 0, inv - 1, spare)
    src_tok = dst_row % n
    return blk_e, n_used, src_tok.reshape(n_blocks, 1, blk), dst_row.reshape(n_blocks, 1, blk)


def kernel(x, c, w_ada, b_ada, w_in, b_gates, w_alpha, b_alpha, lambda_q1, lambda_k1, lambda_q2, lambda_k2, diff_norm_g, gla_norm_g, w_branch_a, w_branch_b, w_out, ln1_g, ln1_b, w_router_g, b_router_g, w_router_e, b_router_e, w_gate_e, w_up_e, w_down_e, ln2_g, ln2_b):
    b, s, d = x.shape
    depth = w_ada.shape[0]
    n = b * s
    alpha = (2.0 * depth) ** 0.25

    ada = _ada(c, w_ada, b_ada)
    x2 = x.reshape(n, d)
    for l in range(depth):
        sh1, sc1, g1, sh2, sc2, g2 = [ada[l, :, i * d:(i + 1) * d].reshape(b, 1, d) for i in range(6)]
        wl = w_in[l]
        w_att = wl[:, :OFF_GATES].astype(BF16)
        w_gates = wl[:, OFF_GATES + GLA_RANK:].astype(BF16)
        w_ab = jnp.pad(wl[:, OFF_GATES:OFF_GATES + GLA_RANK], ((0, 0), (0, LANES - GLA_RANK))).astype(BF16)
        wal = jnp.pad(w_alpha[l], ((0, LANES - GLA_RANK), (0, 0))).astype(BF16)
        p, la = _inproj(x2, sh1, sc1, w_att, w_gates, w_ab, wal, b_alpha[l].reshape(1, -1), s)
        p3 = p.reshape(b, s, W_MAIN)

        lam_init = 0.8 - 0.6 * math.exp(-0.3 * l)
        oa = _attn(p3, lambda_q1[l].reshape(1, -1), lambda_k1[l].reshape(1, -1),
                   lambda_q2[l].reshape(1, -1), lambda_k2[l].reshape(1, -1),
                   diff_norm_g[l].reshape(1, -1), lam_init)
        ob = _gla(p3, la.reshape(b, s, -1), gla_norm_g[l].reshape(1, -1))

        wr = jnp.pad(jnp.concatenate([w_router_g[l], w_router_e[l]], axis=1),
                     ((0, 0), (0, LANES - N_GROUPS - N_EXPERTS)))
        wrh = wr.astype(BF16)
        wr2 = jnp.concatenate([wrh, (wr - wrh.astype(F32)).astype(BF16)], axis=1)
        br = jnp.pad(jnp.concatenate([b_router_g[l], b_router_e[l]]),
                     (0, LANES - N_GROUPS - N_EXPERTS)).reshape(1, LANES)
        x1, u2, lg = _mix(oa.reshape(n, d), ob.reshape(n, d), p, x2, b_gates[l].reshape(1, -1),
                          g1, sh2, sc2, ln1_g[l].reshape(1, d), ln1_b[l].reshape(1, d),
                          w_branch_a[l].astype(BF16), w_branch_b[l].astype(BF16), w_out[l].astype(BF16),
                          wr2, br, s, alpha)

        ri, rw, cnt = _route(lg)
        blk_e, n_used, src_tok3, dst_row3 = _moe_plan(ri, cnt, n)
        y_tok = _experts(blk_e, n_used, src_tok3, dst_row3, u2, w_gate_e, w_up_e, w_down_e, l,
                         2 * n + N_EXPERTS * MOE_ROWS)
        x2 = _combine(x1, y_tok, rw, g2, ln2_g[l].reshape(1, d), ln2_b[l].reshape(1, d), s, alpha)
    return x2.reshape(b, s, d)
```

```python
import functools
import math

import jax
import jax.numpy as jnp
import numpy as np
from jax import lax
from jax.experimental import pallas as pl
from jax.experimental.pallas import tpu as pltpu

F32 = jnp.float32
BF16 = jnp.bfloat16

N_HEADS_DIFF = 8
HEAD_DIM_DIFF = 64
N_HEADS_GLA = 4
KEY_DIM_GLA = 128
VAL_DIM_GLA = 256
GLA_RANK = 16
GLA_TAU = 16.0
GLA_CHUNK = 64
N_GROUPS = 4
EXPERTS_PER_GROUP = 8
N_EXPERTS = N_GROUPS * EXPERTS_PER_GROUP
LN_EPS = 1e-5
LANES = 128
NEG_BIG = -1e30

OFF_QA, OFF_KA, OFF_VA = 0, 1024, 2048
OFF_QB, OFF_KB, OFF_VB, OFF_GB, OFF_GATES = 3072, 3584, 4096, 5120, 6144
W_MAIN = 8192

VMEM_LIMIT = 56 * 1024 * 1024


def _cparams(sem, **kw):
    return pltpu.CompilerParams(dimension_semantics=sem, vmem_limit_bytes=VMEM_LIMIT, **kw)


def _standardize(x):
    mu = jnp.mean(x, axis=-1, keepdims=True)
    xc = x - mu
    var = jnp.mean(xc * xc, axis=-1, keepdims=True)
    return xc * lax.rsqrt(var + LN_EPS)


def _sigmoid(x):
    return 1.0 / (1.0 + jnp.exp(-x))


ROW_TILES = 8


def _store_token_tiles(ref, x):
    t = x.shape[0]
    for s in range(ROW_TILES):
        ref[pl.ds(s, t, stride=ROW_TILES), :] = x[:, s * LANES:(s + 1) * LANES]


def _load_token_tiles(ref):
    t = ref.shape[0] // ROW_TILES
    return jnp.concatenate([ref[pl.ds(s, t, stride=ROW_TILES), :] for s in range(ROW_TILES)], axis=1)


def _ada_kernel(c_ref, w_ref, b_ref, o_ref):
    c = c_ref[...]
    cond = c * _sigmoid(c)
    o_ref[0] = jnp.dot(cond, w_ref[0], preferred_element_type=F32,
                       precision=lax.Precision.HIGHEST) + b_ref[0]


def _ada(c, w_ada, b_ada):
    depth, d, d6 = w_ada.shape
    b = c.shape[0]
    return pl.pallas_call(
        _ada_kernel,
        out_shape=jax.ShapeDtypeStruct((depth, b, d6), F32),
        grid=(depth, d6 // d),
        in_specs=[pl.BlockSpec((b, d), lambda l, j: (0, 0)),
                  pl.BlockSpec((1, d, d), lambda l, j: (l, 0, j)),
                  pl.BlockSpec((1, 1, d), lambda l, j: (l, 0, j))],
        out_specs=pl.BlockSpec((1, b, d), lambda l, j: (l, 0, j)),
        compiler_params=_cparams(("arbitrary", "arbitrary")),
    )(c, w_ada, b_ada.reshape(depth, 1, d6))


def _prep_tile(x, sh_ref, sc_ref, wab_ref, wal_ref, bal_ref, la_ref, u_scr):
    u = _standardize(x) * (1.0 + sc_ref[0]) + sh_ref[0]
    ub = u.astype(BF16)
    u_scr[...] = ub
    ab = jnp.dot(ub, wab_ref[...], preferred_element_type=F32)
    pre = jnp.dot(ab.astype(BF16), wal_ref[...], preferred_element_type=F32) + bal_ref[...]
    la_ref[...] = (jnp.minimum(pre, 0.0) - jnp.log(1.0 + jnp.exp(-jnp.abs(pre)))) * (1.0 / GLA_TAU)


def _inproj_kernel(x_ref, sh_ref, sc_ref, wa_ref, wg_ref, wab_ref, wal_ref, bal_ref,
                   p_ref, la_ref, u_scr, *, n_a):
    j = pl.program_id(1)

    @pl.when(j == 0)
    def _():
        _prep_tile(x_ref[...], sh_ref, sc_ref, wab_ref, wal_ref, bal_ref, la_ref, u_scr)

    @pl.when(j < n_a)
    def _():
        p_ref[...] = jnp.dot(u_scr[...], wa_ref[...], preferred_element_type=F32).astype(BF16)

    @pl.when(j >= n_a)
    def _():
        p_ref[...] = jnp.dot(u_scr[...], wg_ref[...], preferred_element_type=F32).astype(BF16)


def _inproj(x2, sh, sc, w_att, w_gates, w_ab, w_alpha, b_alpha, seq, tm=1024, tn=1024):
    n, d = x2.shape
    tpb = seq // tm
    wq = w_alpha.shape[1]
    n_a = w_att.shape[1] // tn
    return pl.pallas_call(
        functools.partial(_inproj_kernel, n_a=n_a),
        out_shape=(jax.ShapeDtypeStruct((n, W_MAIN), BF16),
                   jax.ShapeDtypeStruct((n, wq), F32)),
        grid=(n // tm, W_MAIN // tn),
        in_specs=[pl.BlockSpec((tm, d), lambda i, j: (i, 0)),
                  pl.BlockSpec((1, 1, d), lambda i, j: (i // tpb, 0, 0)),
                  pl.BlockSpec((1, 1, d), lambda i, j: (i // tpb, 0, 0)),
                  pl.BlockSpec((d, tn), lambda i, j: (0, jnp.minimum(j, n_a - 1))),
                  pl.BlockSpec((d, tn), lambda i, j: (0, jnp.maximum(j - n_a, 0))),
                  pl.BlockSpec((d, LANES), lambda i, j: (0, 0)),
                  pl.BlockSpec((LANES, wq), lambda i, j: (0, 0)),
                  pl.BlockSpec((1, wq), lambda i, j: (0, 0))],
        out_specs=(pl.BlockSpec((tm, tn), lambda i, j: (i, j)),
                   pl.BlockSpec((tm, wq), lambda i, j: (i, 0))),
        scratch_shapes=[pltpu.VMEM((tm, d), BF16)],
        compiler_params=_cparams(("arbitrary", "arbitrary")),
    )(x2, sh, sc, w_att, w_gates, w_ab, w_alpha, b_alpha)


ONES_ROWS = 16


QUERY_GROUP = 512
POS_SPLIT = 32


def _attn_kernel(slopes_ref, q_ref, k_ref, v_ref, lq1_ref, lk1_ref, lq2_ref, lk2_ref, g_ref,
                 o_ref, kaug_scr, vt_scr, dmask_scr, sa_scr, sb_scr, acc_scr, m_scr, *, tk, lam_init):
    h = pl.program_id(1)
    qi = pl.program_id(2)
    slope = slopes_ref[h]
    dh = HEAD_DIM_DIFF
    dv = LANES
    n_chunks = vt_scr.shape[0]
    halves = q_ref.shape[0] // tk

    @pl.when(qi == 0)
    def _():
        for j in range(n_chunks):
            vt_scr[j, 0:dv, :] = v_ref[j * tk:(j + 1) * tk, :].astype(F32).T.astype(BF16)
            vt_scr[j, dv:dv + ONES_ROWS, :] = jnp.ones((ONES_ROWS, tk), BF16)
        kaug_scr[:, 0:LANES] = k_ref[...]
        koff = lax.broadcasted_iota(jnp.int32, (tk, LANES), 0)
        flane = lax.broadcasted_iota(jnp.int32, (tk, LANES), 1)
        feat = jnp.where(flane == 0, koff // POS_SPLIT, jnp.where(flane == 1, koff % POS_SPLIT, 0))
        feat = feat.astype(F32).astype(BF16)
        for j in range(n_chunks):
            kaug_scr[j * tk:(j + 1) * tk, LANES:2 * LANES] = feat
        kpos = lax.broadcasted_iota(jnp.int32, (tk, 2 * tk), 0)
        qpos = lax.broadcasted_iota(jnp.int32, (tk, 2 * tk), 1)
        qoff = jnp.where(qpos >= tk, qpos - tk, qpos)
        dmask_scr[...] = jnp.where(qoff >= kpos, 0.0, NEG_BIG)

    q = q_ref[...] * jnp.asarray(dh ** -0.5, BF16)
    lane = lax.broadcasted_iota(jnp.int32, q.shape, 1)
    zero = jnp.zeros_like(q)
    qf = jnp.where(lane == 0, slope * POS_SPLIT, jnp.where(lane == 1, slope, 0.0)).astype(BF16)
    q1 = jnp.concatenate([jnp.where(lane < dh, q, zero), qf], axis=1)
    q2 = jnp.concatenate([jnp.where(lane >= dh, q, zero), qf], axis=1)
    qq = jnp.concatenate([part[hq * tk:(hq + 1) * tk] for hq in range(halves) for part in (q1, q2)], axis=0)

    m_scr[...] = jnp.full(m_scr.shape, NEG_BIG, F32)
    acc_scr[...] = jnp.zeros(acc_scr.shape, F32)

    per_half = 2 * tk // QUERY_GROUP
    groups = [(hq, slice((hq * per_half + g) * QUERY_GROUP, (hq * per_half + g + 1) * QUERY_GROUP),
               slice(g * QUERY_GROUP, (g + 1) * QUERY_GROUP))
              for hq in range(halves) for g in range(per_half)]

    def score(kaug, ls):
        return lax.dot_general(kaug, qq[ls], (((1,), (1,)), ((), ())), preferred_element_type=F32)

    def step(j_cur, src, j_next, dst, first_half, masked_half, next_first_half=0):
        if dst is not None:
            kaug = kaug_scr[pl.ds(pl.multiple_of(j_next * tk, tk), tk), :]
        vt = vt_scr[j_cur]
        m_all = m_scr[...]
        acc_all = acc_scr[...]
        done = []
        for hq, ls, ms in groups:
            if dst is not None and hq >= next_first_half:
                dst[:, ls] = score(kaug, ls)
            if hq < first_half:
                continue
            cj = (-slope) * ((halves * qi + hq - j_cur) * tk).astype(F32)
            s = src[:, ls]
            if hq == masked_half:
                s = s + dmask_scr[:, ms]
            m_prev = m_all[:, ls]
            m_new = jnp.maximum(m_prev, jnp.max(s, axis=0, keepdims=True) + cj)
            p = jnp.exp(s - (m_new - cj))
            alpha = jnp.exp(m_prev - m_new)
            done.append((ls, m_new, alpha * acc_all[:, ls] + jnp.dot(vt, p.astype(BF16),
                                                                     preferred_element_type=F32)))
        for ls, m_new, acc_new in done:
            m_scr[:, ls] = m_new
            acc_scr[:, ls] = acc_new

    kaug0 = kaug_scr[0:tk, :]
    for _, ls, _ in groups:
        sa_scr[:, ls] = score(kaug0, ls)

    def pair(jj, carry):
        j = 2 * jj
        step(j, sa_scr, j + 1, sb_scr, 0, None)
        step(j + 1, sb_scr, j + 2, sa_scr, 0, None)
        return carry

    lax.fori_loop(0, qi, pair, 0)
    step(2 * qi, sa_scr, 2 * qi + 1, sb_scr, 0, 0, next_first_half=1)
    step(2 * qi + 1, sb_scr, None, None, 1, 1)

    lam = (jnp.exp(jnp.sum(lq1_ref[...] * lk1_ref[...], axis=-1, keepdims=True))
           - jnp.exp(jnp.sum(lq2_ref[...] * lk2_ref[...], axis=-1, keepdims=True)) + lam_init)
    acc = acc_scr[...]
    ot = acc[0:dv] / acc[dv:dv + 1]
    for hq in range(halves):
        base = hq * 2 * tk
        o = (ot[:, base:base + tk] - lam * ot[:, base + tk:base + 2 * tk]).T
        o = o * lax.rsqrt(jnp.mean(o * o, axis=-1, keepdims=True) + LN_EPS) * g_ref[...] * (1.0 - lam_init)
        o_ref[hq * tk:(hq + 1) * tk, :] = o.astype(o_ref.dtype)


def _attn(p3, lq1, lk1, lq2, lk2, g, lam_init, tk=512):
    b, s, _ = p3.shape
    h = N_HEADS_DIFF
    tq = 2 * tk
    slopes = jnp.asarray(2.0 ** (-8.0 * np.arange(1, h + 1) / h), dtype=F32)
    vec = pl.BlockSpec((1, HEAD_DIM_DIFF), lambda bi, hi, qi, sl: (0, 0))
    return pl.pallas_call(
        functools.partial(_attn_kernel, tk=tk, lam_init=lam_init),
        out_shape=jax.ShapeDtypeStruct((b, s, h * LANES), BF16),
        grid_spec=pltpu.PrefetchScalarGridSpec(
            num_scalar_prefetch=1,
            grid=(b, h, s // tq),
            in_specs=[pl.BlockSpec((None, tq, LANES), lambda bi, hi, qi, sl: (bi, qi, OFF_QA // LANES + hi)),
                      pl.BlockSpec((None, s, LANES), lambda bi, hi, qi, sl: (bi, 0, OFF_KA // LANES + hi)),
                      pl.BlockSpec((None, s, LANES), lambda bi, hi, qi, sl: (bi, 0, OFF_VA // LANES + hi)),
                      vec, vec, vec, vec,
                      pl.BlockSpec((1, LANES), lambda bi, hi, qi, sl: (0, 0))],
            out_specs=pl.BlockSpec((None, tq, LANES), lambda bi, hi, qi, sl: (bi, qi, hi)),
            scratch_shapes=[pltpu.VMEM((s, 2 * LANES), BF16),
                            pltpu.VMEM((s // tk, LANES + ONES_ROWS, tk), BF16),
                            pltpu.VMEM((tk, 2 * tk), F32),
                            pltpu.VMEM((tk, 2 * tq), F32),
                            pltpu.VMEM((tk, 2 * tq), F32),
                            pltpu.VMEM((LANES + ONES_ROWS, 2 * tq), F32),
                            pltpu.VMEM((1, 2 * tq), F32)]),
        compiler_params=_cparams(("arbitrary", "arbitrary", "arbitrary")),
    )(slopes, p3, p3, p3, lq1, lk1, lq2, lk2, g)


GLA_HEADS_PER_STEP = 4


def _gla_kernel(q_ref, k_ref, v_ref, gb_ref, la_ref, g_ref, o_ref, state_scr, *, tt):
    c = GLA_CHUNK
    dk, dv = KEY_DIM_GLA, VAL_DIM_GLA
    hp = state_scr.shape[0]
    n_c = tt // c
    work = [(hh, ci) for hh in range(hp) for ci in range(n_c)]

    @pl.when(pl.program_id(2) == 0)
    def _():
        state_scr[...] = jnp.zeros(state_scr.shape, F32)

    rr = lax.broadcasted_iota(jnp.int32, (c, c), 0)
    cr = lax.broadcasted_iota(jnp.int32, (c, c), 1)
    causal = cr <= rr
    tri = jnp.where(causal, 1.0, 0.0).astype(BF16)
    qscale = dk ** -0.5
    rows = lambda ci: slice(ci * c, (ci + 1) * c)
    kcols = lambda hh: slice(hh * dk, (hh + 1) * dk)
    vcols = lambda hh: slice(hh * dv, (hh + 1) * dv)

    cum = {}
    for hh, ci in work:
        la = la_ref[rows(ci), kcols(hh)]
        la_hi = la.astype(BF16)
        la_lo = (la - la_hi.astype(F32)).astype(BF16)
        cum2 = jnp.dot(tri, jnp.concatenate([la_hi, la_lo], axis=1), preferred_element_type=F32)
        cum[hh, ci] = cum2[:, :dk] + cum2[:, dk:]

    qe, q2, k2, kd, dec, v = {}, {}, {}, {}, {}, {}
    for hh, ci in work:
        cum_c = cum[hh, ci]
        last = cum_c[c - 1:c]
        mid = cum_c[c // 2:c // 2 + 1]
        q_c = q_ref[rows(ci), kcols(hh)].astype(F32) * qscale
        k_c = k_ref[rows(ci), kcols(hh)].astype(F32)
        qe[hh, ci] = (q_c * jnp.exp(cum_c)).astype(BF16)
        q2[hh, ci] = (q_c * jnp.exp(cum_c - mid)).astype(BF16)
        k2[hh, ci] = (k_c * jnp.exp(mid - cum_c)).astype(BF16)
        kd[hh, ci] = (k_c * jnp.exp(last - cum_c)).astype(BF16)
        dec[hh, ci] = jnp.exp(last)
        v[hh, ci] = v_ref[rows(ci), vcols(hh)]

    att = {w: lax.dot_general(q2[w], k2[w], (((1,), (1,)), ((), ())), preferred_element_type=F32)
           for w in work}
    o_intra = {w: jnp.dot(jnp.where(causal, att[w], 0.0).astype(BF16), v[w], preferred_element_type=F32)
               for w in work}
    kv = {w: jnp.dot(v[w].astype(F32).T.astype(BF16), kd[w], preferred_element_type=F32) for w in work}

    o_inter = {}
    for hh in range(hp):
        st = state_scr[hh]
        for ci in range(n_c):
            o_inter[hh, ci] = lax.dot_general(qe[hh, ci], st.astype(BF16), (((1,), (1,)), ((), ())),
                                              preferred_element_type=F32)
            st = st * dec[hh, ci] + kv[hh, ci]
        state_scr[hh] = st

    for hh, ci in work:
        o = o_inter[hh, ci] + o_intra[hh, ci]
        o = o * lax.rsqrt(jnp.mean(o * o, axis=-1, keepdims=True) + LN_EPS) * g_ref[...]
        gate = gb_ref[rows(ci), vcols(hh)].astype(F32)
        o_ref[rows(ci), vcols(hh)] = (o * (gate * _sigmoid(gate))).astype(o_ref.dtype)


def _gla(p3, la3, g, tt=512):
    b, s, _ = p3.shape
    hp = GLA_HEADS_PER_STEP
    dk, dv = KEY_DIM_GLA * hp, VAL_DIM_GLA * hp
    return pl.pallas_call(
        functools.partial(_gla_kernel, tt=tt),
        out_shape=jax.ShapeDtypeStruct((b, s, N_HEADS_GLA * VAL_DIM_GLA), BF16),
        grid=(b, N_HEADS_GLA // hp, s // tt),
        in_specs=[pl.BlockSpec((None, tt, dk), lambda bi, hi, ti: (bi, ti, OFF_QB // dk + hi)),
                  pl.BlockSpec((None, tt, dk), lambda bi, hi, ti: (bi, ti, OFF_KB // dk + hi)),
                  pl.BlockSpec((None, tt, dv), lambda bi, hi, ti: (bi, ti, OFF_VB // dv + hi)),
                  pl.BlockSpec((None, tt, dv), lambda bi, hi, ti: (bi, ti, OFF_GB // dv + hi)),
                  pl.BlockSpec((None, tt, dk), lambda bi, hi, ti: (bi, ti, hi)),
                  pl.BlockSpec((1, VAL_DIM_GLA), lambda bi, hi, ti: (0, 0))],
        out_specs=pl.BlockSpec((None, tt, dv), lambda bi, hi, ti: (bi, ti, hi)),
        scratch_shapes=[pltpu.VMEM((hp, VAL_DIM_GLA, KEY_DIM_GLA), F32)],
        compiler_params=_cparams(("arbitrary", "arbitrary", "arbitrary")),
    )(p3, p3, p3, p3, la3, g)


def _mix_kernel(oa_ref, ob_ref, gt_ref, x_ref, bg_ref, g1_ref, sh2_ref, sc2_ref, lng_ref, lnb_ref,
                wba_ref, wbb_ref, wo_ref, wr_ref, br_ref,
                x1_ref, u2_ref, lg_ref, *, alpha):
    d = x_ref.shape[-1]
    a = jnp.dot(oa_ref[...], wba_ref[...], preferred_element_type=F32)
    bm = jnp.dot(ob_ref[...], wbb_ref[...], preferred_element_type=F32)
    gates = _sigmoid(gt_ref[...].astype(F32) + bg_ref[...])
    mixed = gates[:, :d] * a + gates[:, d:] * bm
    y = jnp.dot(mixed.astype(BF16), wo_ref[...], preferred_element_type=F32)
    x1 = _standardize(alpha * x_ref[...] + g1_ref[0] * y) * lng_ref[...] + lnb_ref[...]
    x1_ref[...] = x1
    u2 = _standardize(x1) * (1.0 + sc2_ref[0]) + sh2_ref[0]
    _store_token_tiles(u2_ref, u2)
    uh = u2.astype(BF16)
    ul = (u2 - uh.astype(F32)).astype(BF16)
    hh = jnp.dot(uh, wr_ref[...], preferred_element_type=F32)
    lg_ref[...] = (hh[:, :LANES] + hh[:, LANES:]
                   + jnp.dot(ul, wr_ref[:, 0:LANES], preferred_element_type=F32) + br_ref[...])


def _mix(oa, ob, p, x2, bg, g1, sh2, sc2, lng, lnb, wba, wbb, wo, wr2, br, seq, alpha, tm=512):
    n, d = x2.shape
    tpb = seq // tm
    row = lambda i: (i, 0)
    const = lambda i: (0, 0)
    per_b = pl.BlockSpec((1, 1, d), lambda i: (i // tpb, 0, 0))
    return pl.pallas_call(
        functools.partial(_mix_kernel, alpha=alpha),
        out_shape=(jax.ShapeDtypeStruct((n, d), F32),
                   jax.ShapeDtypeStruct((n * (d // LANES), LANES), F32),
                   jax.ShapeDtypeStruct((n, LANES), F32)),
        grid=(n // tm,),
        in_specs=[pl.BlockSpec((tm, d), row), pl.BlockSpec((tm, d), row),
                  pl.BlockSpec((tm, 2 * d), lambda i: (i, OFF_GATES // (2 * d))),
                  pl.BlockSpec((tm, d), row),
                  pl.BlockSpec((1, 2 * d), const), per_b, per_b, per_b,
                  pl.BlockSpec((1, d), const), pl.BlockSpec((1, d), const),
                  pl.BlockSpec((d, d), const), pl.BlockSpec((d, d), const), pl.BlockSpec((d, d), const),
                  pl.BlockSpec((d, 2 * LANES), const),
                  pl.BlockSpec((1, LANES), const)],
        out_specs=(pl.BlockSpec((tm, d), row), pl.BlockSpec((tm * (d // LANES), LANES), row),
                   pl.BlockSpec((tm, LANES), row)),
        compiler_params=_cparams(("arbitrary",)),
    )(oa, ob, p, x2, bg, g1, sh2, sc2, lng, lnb, wba, wbb, wo, wr2, br)


def _route_kernel(lg_ref, ri_ref, rw_ref, cnt_ref, *, tm):
    @pl.when(pl.program_id(0) == 0)
    def _():
        cnt_ref[...] = jnp.zeros(cnt_ref.shape, F32)

    lg = lg_ref[...]
    lane = lax.broadcasted_iota(jnp.int32, lg.shape, 1)
    lanef = lane.astype(F32)
    far = float(LANES)
    is_g = lane < N_GROUPS
    gl = jnp.where(is_g, lg, NEG_BIG)
    gmax = jnp.max(gl, axis=-1, keepdims=True)
    gidx = jnp.min(jnp.where(gl == gmax, lanef, far), axis=-1, keepdims=True)
    gw = 1.0 / jnp.sum(jnp.where(is_g, jnp.exp(gl - gmax), 0.0), axis=-1, keepdims=True)
    lo = N_GROUPS + gidx * EXPERTS_PER_GROUP
    in_g = (lanef >= lo) & (lanef < lo + EXPERTS_PER_GROUP)
    el = jnp.where(in_g, lg, NEG_BIG)
    v1 = jnp.max(el, axis=-1, keepdims=True)
    i1 = jnp.min(jnp.where(in_g & (el == v1), lanef, far), axis=-1, keepdims=True)
    in_g2 = in_g & (lanef != i1)
    el2 = jnp.where(in_g2, lg, NEG_BIG)
    v2 = jnp.max(el2, axis=-1, keepdims=True)
    i2 = jnp.min(jnp.where(in_g2 & (el2 == v2), lanef, far), axis=-1, keepdims=True)
    t = jnp.exp(v2 - v1)
    w1 = gw / (1.0 + t)
    w2 = gw * t / (1.0 + t)

    oh1 = lanef == i1
    oh2 = lanef == i2
    oh = jnp.where(oh1 | oh2, 1.0, 0.0)
    r = lax.broadcasted_iota(jnp.int32, (tm, tm), 0)
    c = lax.broadcasted_iota(jnp.int32, (tm, tm), 1)
    lower = jnp.where(c < r, 1.0, 0.0).astype(BF16)
    base = jnp.dot(lower, oh.astype(BF16), preferred_element_type=F32) + cnt_ref[0:1, :]
    r1 = jnp.sum(jnp.where(oh1, base, 0.0), axis=-1, keepdims=True)
    r2 = jnp.sum(jnp.where(oh2, base, 0.0), axis=-1, keepdims=True)
    cnt_ref[...] = cnt_ref[...] + jnp.sum(oh, axis=0, keepdims=True)

    e1 = i1 - float(N_GROUPS)
    e2 = i2 - float(N_GROUPS)
    ri = jnp.where(lane == 0, e1, jnp.where(lane == 1, e2, jnp.where(lane == 2, r1, jnp.where(lane == 3, r2, 0.0))))
    ri_ref[...] = ri.astype(jnp.int32)
    rw_ref[...] = jnp.where(lane == 0, w1, jnp.where(lane == 1, w2, 0.0))


def _route(lg, tm=512):
    n = lg.shape[0]
    row = lambda i: (i, 0)
    return pl.pallas_call(
        functools.partial(_route_kernel, tm=tm),
        out_shape=(jax.ShapeDtypeStruct((n, LANES), jnp.int32),
                   jax.ShapeDtypeStruct((n, LANES), F32),
                   jax.ShapeDtypeStruct((8, LANES), F32)),
        grid=(n // tm,),
        in_specs=[pl.BlockSpec((tm, LANES), row)],
        out_specs=(pl.BlockSpec((tm, LANES), row), pl.BlockSpec((tm, LANES), row),
                   pl.BlockSpec((8, LANES), lambda i: (0, 0))),
        compiler_params=_cparams(("arbitrary",)),
    )(lg)


MOE_ROWS = 256


def _expert_kernel(be_ref, nu_ref, tok_a_ref, tok_b_ref, dst_ref, u_ref, wg_ref, wu_ref, wd_ref, y_ref,
                   xb0, xb1, yb0, yb1, wg_scr, wu_scr, wd_scr, gsem, ssem):
    i = pl.program_id(0)
    n_used = nu_ref[0]
    blk = tok_a_ref.shape[-1]
    xbs, ybs = (xb0, xb1), (yb0, yb1)

    def tile(ref, t):
        return ref.at[pl.ds(pl.multiple_of(t * ROW_TILES, ROW_TILES), ROW_TILES)]

    def gather_issue(tok_ref, xb, sem):
        for r in range(blk):
            pltpu.make_async_copy(tile(u_ref, tok_ref[0, 0, r]), tile(xb, r), sem).start()

    def gather_wait(xb, sem):
        pltpu.make_async_copy(u_ref.at[pl.ds(0, blk * ROW_TILES)], xb, sem).wait()

    def scatter_issue(yb, sem):
        for r in range(blk):
            pltpu.make_async_copy(tile(yb, r), tile(y_ref, dst_ref[0, 0, r]), sem).start()

    def scatter_wait(yb, sem):
        pltpu.make_async_copy(yb, y_ref.at[pl.ds(0, blk * ROW_TILES)], sem).wait()

    def load_weights():
        @pl.when((i == 0) | (be_ref[i] != be_ref[jnp.maximum(i - 1, 0)]))
        def _():
            wg_scr[...] = wg_ref[0].astype(BF16)
            wu_scr[...] = wu_ref[0].astype(BF16)
            wd_scr[...] = wd_ref[0].astype(BF16)

    def compute(xb, yb):
        x = _load_token_tiles(xb).astype(BF16)
        g = jnp.dot(x, wg_scr[...], preferred_element_type=F32)
        u = jnp.dot(x, wu_scr[...], preferred_element_type=F32)
        hid = (g * _sigmoid(g) * u).astype(BF16)
        _store_token_tiles(yb, jnp.dot(hid, wd_scr[...], preferred_element_type=F32))

    @pl.when(i == 0)
    def _():
        gather_issue(tok_a_ref, xb0, gsem.at[0])
        gather_wait(xb0, gsem.at[0])
        load_weights()
        gather_issue(tok_b_ref, xb1, gsem.at[1])
        compute(xb0, yb0)

    for p in range(2):
        cur, oth = p, 1 - p

        @pl.when((i >= 1) & (i < n_used) & (i % 2 == p))
        def _():
            gather_wait(xbs[cur], gsem.at[cur])

            @pl.when(i >= 2)
            def _():
                scatter_wait(ybs[cur], ssem.at[cur])

            load_weights()
            gather_issue(tok_b_ref, xbs[oth], gsem.at[oth])
            scatter_issue(ybs[oth], ssem.at[oth])
            compute(xbs[cur], ybs[cur])

        @pl.when((i == n_used) & (i % 2 == p))
        def _():
            gather_wait(xbs[cur], gsem.at[cur])

            @pl.when(i >= 2)
            def _():
                scatter_wait(ybs[cur], ssem.at[cur])

            scatter_issue(ybs[oth], ssem.at[oth])

        @pl.when((i == n_used + 1) & (i % 2 == p))
        def _():
            scatter_wait(ybs[cur], ssem.at[cur])


def _experts(blk_e, n_used, src_tok3, dst_row3, u2t, wg, wu, wd, layer, n_rows):
    d, de = wg.shape[-2:]
    blk = MOE_ROWS
    rt = d // LANES
    n_steps = blk_e.shape[0]
    wspec = lambda r, c: pl.BlockSpec((None, 1, r, c), lambda i, be, nu: (layer, be[i], 0, 0))
    ispec = lambda f: pl.BlockSpec((1, 1, blk), lambda i, be, nu: (f(i, nu), 0, 0), memory_space=pltpu.SMEM)
    return pl.pallas_call(
        _expert_kernel,
        out_shape=jax.ShapeDtypeStruct((n_rows * rt, LANES), F32),
        grid_spec=pltpu.PrefetchScalarGridSpec(
            num_scalar_prefetch=2,
            grid=(n_steps,),
            in_specs=[ispec(lambda i, nu: jnp.minimum(i, nu[0] - 1)),
                      ispec(lambda i, nu: jnp.minimum(i + 1, nu[0] - 1)),
                      ispec(lambda i, nu: jnp.clip(i - 1, 0, nu[0] - 1)),
                      pl.BlockSpec(memory_space=pl.ANY),
                      wspec(d, de), wspec(d, de), wspec(de, d)],
            out_specs=pl.BlockSpec(memory_space=pl.ANY),
            scratch_shapes=[pltpu.VMEM((blk * rt, LANES), F32), pltpu.VMEM((blk * rt, LANES), F32),
                            pltpu.VMEM((blk * rt, LANES), F32), pltpu.VMEM((blk * rt, LANES), F32),
                            pltpu.VMEM((d, de), BF16), pltpu.VMEM((d, de), BF16),
                            pltpu.VMEM((de, d), BF16),
                            pltpu.SemaphoreType.DMA((2,)), pltpu.SemaphoreType.DMA((2,))]),
        compiler_params=_cparams(("arbitrary",), has_side_effects=True),
    )(blk_e, n_used, src_tok3, src_tok3, dst_row3, u2t, wg, wu, wd)


def _combine_kernel(x1_ref, y1_ref, y2_ref, rw_ref, g2_ref, lng_ref, lnb_ref, o_ref, *, alpha):
    rw = rw_ref[...]
    y = rw[:, 0:1] * _load_token_tiles(y1_ref) + rw[:, 1:2] * _load_token_tiles(y2_ref)
    o_ref[...] = _standardize(alpha * x1_ref[...] + g2_ref[0] * y) * lng_ref[...] + lnb_ref[...]


def _combine(x1, y_tok, rw, g2, lng, lnb, seq, alpha, tm=512):
    n, d = x1.shape
    tpb = seq // tm
    const = lambda i: (0, 0)
    return pl.pallas_call(
        functools.partial(_combine_kernel, alpha=alpha),
        out_shape=jax.ShapeDtypeStruct((n, d), F32),
        grid=(n // tm,),
        in_specs=[pl.BlockSpec((tm, d), lambda i: (i, 0)),
                  pl.BlockSpec((tm * (d // LANES), LANES), lambda i: (i, 0)),
                  pl.BlockSpec((tm * (d // LANES), LANES), lambda i: (n // tm + i, 0)),
                  pl.BlockSpec((tm, LANES), lambda i: (i, 0)),
                  pl.BlockSpec((1, 1, d), lambda i: (i // tpb, 0, 0)),
                  pl.BlockSpec((1, d), const), pl.BlockSpec((1, d), const)],
        out_specs=pl.BlockSpec((tm, d), lambda i: (i, 0)),
        compiler_params=_cparams(("arbitrary",)),
    )(x1, y_tok, y_tok, rw, g2, lng, lnb)


ID_SPLIT = 256


def _invert_kernel(ri_ref, ps_ref, inv_ref, *, n):
    i = pl.program_id(0)
    tm = ri_ref.shape[0]
    n_hi = inv_ref.shape[0]

    @pl.when(i == 0)
    def _():
        inv_ref[...] = jnp.zeros(inv_ref.shape, F32)

    ri = ri_ref[...].astype(F32)
    lane = lax.broadcasted_iota(jnp.int32, (tm, LANES), 1)
    lanef = lane.astype(F32)
    tok = (lax.broadcasted_iota(jnp.int32, (tm, 1), 0) + i * tm).astype(F32)
    cols = []
    onehot_lo = []
    for j in range(2):
        e = ri[:, j:j + 1]
        row = jnp.sum(jnp.where(lanef == e, ps_ref[...], 0.0), axis=-1, keepdims=True) + ri[:, 2 + j:3 + j]
        hi = jnp.floor(row * (1.0 / LANES))
        lo = row - hi * LANES
        ident = tok + float(j * n + 1)
        id_hi = jnp.floor(ident * (1.0 / ID_SPLIT))
        cols += [hi, id_hi, ident - id_hi * ID_SPLIT]
        onehot_lo.append(jnp.where(lanef == lo, 1.0, 0.0).astype(BF16))
    packed = jnp.zeros((tm, LANES), F32)
    for k, col in enumerate(cols):
        packed = jnp.where(lane == k, col, packed)
    rows = packed.T
    hsel = lax.broadcasted_iota(jnp.int32, (n_hi, tm), 0).astype(F32)
    lhs_hi, lhs_lo = [], []
    for j in range(2):
        hit = hsel == rows[3 * j:3 * j + 1]
        lhs_hi.append(jnp.where(hit, rows[3 * j + 1:3 * j + 2], 0.0).astype(BF16))
        lhs_lo.append(jnp.where(hit, rows[3 * j + 2:3 * j + 3], 0.0).astype(BF16))
    rhs = jnp.concatenate(onehot_lo, axis=0)
    inv_ref[...] += (float(ID_SPLIT) * jnp.dot(jnp.concatenate(lhs_hi, axis=1), rhs, preferred_element_type=F32)
                     + jnp.dot(jnp.concatenate(lhs_lo, axis=1), rhs, preferred_element_type=F32))


def _invert(ri, pstarts, n_rows, tm=1024):
    n = ri.shape[0]
    assert n % tm == 0 and n_rows % LANES == 0
    ps = jnp.pad(pstarts.astype(F32), (0, LANES - pstarts.shape[0])).reshape(1, LANES)
    return pl.pallas_call(
        functools.partial(_invert_kernel, n=n),
        out_shape=jax.ShapeDtypeStruct((n_rows // LANES, LANES), F32),
        grid=(n // tm,),
        in_specs=[pl.BlockSpec((tm, LANES), lambda i: (i, 0)),
                  pl.BlockSpec((1, LANES), lambda i: (0, 0))],
        out_specs=pl.BlockSpec((n_rows // LANES, LANES), lambda i: (0, 0)),
        compiler_params=_cparams(("arbitrary",)),
    )(ri, ps)


def _moe_plan(ri, cnt, n):
    blk = MOE_ROWS
    n_rows = 2 * n + N_EXPERTS * blk
    n_blocks = n_rows // blk
    n_steps = n_blocks + 2
    counts = cnt[0, N_GROUPS:N_GROUPS + N_EXPERTS].astype(jnp.int32)
    padded = ((counts + blk - 1) // blk) * blk
    pends = jnp.cumsum(padded)
    pstarts = pends - padded
    blk_start = jnp.arange(n_steps, dtype=jnp.int32) * blk
    blk_e = jnp.minimum(jnp.sum(blk_start[:, None] >= pends[None, :], axis=1), N_EXPERTS - 1).astype(jnp.int32)
    n_used = (pends[-1:] // blk).astype(jnp.int32)
    assigned_upto = jnp.cumsum(counts)[blk_e[:n_blocks]]
    spare = (2 * n + jnp.arange(n_rows, dtype=jnp.int32).reshape(n_blocks, blk) - assigned_upto[:, None]).reshape(-1)
    inv = _invert(ri, pstarts, n_rows).reshape(-1).astype(jnp.int32)
    dst_row = jnp.where(inv > 0, inv - 1, spare)
    src_tok = dst_row % n
    return blk_e, n_used, src_tok.reshape(n_blocks, 1, blk), dst_row.reshape(n_blocks, 1, blk)


def kernel(x, c, w_ada, b_ada, w_in, b_gates, w_alpha, b_alpha, lambda_q1, lambda_k1, lambda_q2, lambda_k2, diff_norm_g, gla_norm_g, w_branch_a, w_branch_b, w_out, ln1_g, ln1_b, w_router_g, b_router_g, w_router_e, b_router_e, w_gate_e, w_up_e, w_down_e, ln2_g, ln2_b):
    b, s, d = x.shape
    depth = w_ada.shape[0]
    n = b * s
    alpha = (2.0 * depth) ** 0.25

    ada = _ada(c, w_ada, b_ada)
    x2 = x.reshape(n, d)
    for l in range(depth):
        sh1, sc1, g1, sh2, sc2, g2 = [ada[l, :, i * d:(i + 1) * d].reshape(b, 1, d) for i in range(6)]
        wl = w_in[l]
        w_att = wl[:, :OFF_GATES].astype(BF16)
        w_gates = wl[:, OFF_GATES + GLA_RANK:].astype(BF16)
        w_ab = jnp.pad(wl[:, OFF_GATES:OFF_GATES + GLA_RANK], ((0, 0), (0, LANES - GLA_RANK))).astype(BF16)
        wal = jnp.pad(w_alpha[l], ((0, LANES - GLA_RANK), (0, 0))).astype(BF16)
        p, la = _inproj(x2, sh1, sc1, w_att, w_gates, w_ab, wal, b_alpha[l].reshape(1, -1), s)
        p3 = p.reshape(b, s, W_MAIN)

        lam_init = 0.8 - 0.6 * math.exp(-0.3 * l)
        oa = _attn(p3, lambda_q1[l].reshape(1, -1), lambda_k1[l].reshape(1, -1),
                   lambda_q2[l].reshape(1, -1), lambda_k2[l].reshape(1, -1),
                   diff_norm_g[l].reshape(1, -1), lam_init)
        ob = _gla(p3, la.reshape(b, s, -1), gla_norm_g[l].reshape(1, -1))

        wr = jnp.pad(jnp.concatenate([w_router_g[l], w_router_e[l]], axis=1),
                     ((0, 0), (0, LANES - N_GROUPS - N_EXPERTS)))
        wrh = wr.astype(BF16)
        wr2 = jnp.concatenate([wrh, (wr - wrh.astype(F32)).astype(BF16)], axis=1)
        br = jnp.pad(jnp.concatenate([b_router_g[l], b_router_e[l]]),
                     (0, LANES - N_GROUPS - N_EXPERTS)).reshape(1, LANES)
        x1, u2, lg = _mix(oa.reshape(n, d), ob.reshape(n, d), p, x2, b_gates[l].reshape(1, -1),
                          g1, sh2, sc2, ln1_g[l].reshape(1, d), ln1_b[l].reshape(1, d),
                          w_branch_a[l].astype(BF16), w_branch_b[l].astype(BF16), w_out[l].astype(BF16),
                          wr2, br, s, alpha)

        ri, rw, cnt = _route(lg)
        blk_e, n_used, src_tok3, dst_row3 = _moe_plan(ri, cnt, n)
        y_tok = _experts(blk_e, n_used, src_tok3, dst_row3, u2, w_gate_e, w_up_e, w_down_e, l,
                         2 * n + N_EXPERTS * MOE_ROWS)
        x2 = _combine(x1, y_tok, rw, g2, ln2_g[l].reshape(1, d), ln2_b[l].reshape(1, d), s, alpha)
    return x2.reshape(b, s, d)
```

```python
import functools
import math

import jax
import jax.numpy as jnp
import numpy as np
from jax import lax
from jax.experimental import pallas as pl
from jax.experimental.pallas import tpu as pltpu

F32 = jnp.float32
BF16 = jnp.bfloat16

N_HEADS_DIFF = 8
HEAD_DIM_DIFF = 64
N_HEADS_GLA = 4
KEY_DIM_GLA = 128
VAL_DIM_GLA = 256
GLA_RANK = 16
GLA_TAU = 16.0
GLA_CHUNK = 64
N_GROUPS = 4
EXPERTS_PER_GROUP = 8
N_EXPERTS = N_GROUPS * EXPERTS_PER_GROUP
LN_EPS = 1e-5
LANES = 128
NEG_BIG = -1e30

OFF_QA, OFF_KA, OFF_VA = 0, 1024, 2048
OFF_QB, OFF_KB, OFF_VB, OFF_GB, OFF_GATES = 3072, 3584, 4096, 5120, 6144
W_MAIN = 8192

VMEM_LIMIT = 56 * 1024 * 1024


def _cparams(sem, **kw):
    return pltpu.CompilerParams(dimension_semantics=sem, vmem_limit_bytes=VMEM_LIMIT, **kw)


def _standardize(x):
    mu = jnp.mean(x, axis=-1, keepdims=True)
    xc = x - mu
    var = jnp.mean(xc * xc, axis=-1, keepdims=True)
    return xc * lax.rsqrt(var + LN_EPS)


def _sigmoid(x):
    return 1.0 / (1.0 + jnp.exp(-x))


ROW_TILES = 8


def _store_token_tiles(ref, x):
    t = x.shape[0]
    for s in range(ROW_TILES):
        ref[pl.ds(s, t, stride=ROW_TILES), :] = x[:, s * LANES:(s + 1) * LANES]


def _load_token_tiles(ref):
    t = ref.shape[0] // ROW_TILES
    return jnp.concatenate([ref[pl.ds(s, t, stride=ROW_TILES), :] for s in range(ROW_TILES)], axis=1)


def _ada_kernel(c_ref, w_ref, b_ref, o_ref):
    c = c_ref[...]
    cond = c * _sigmoid(c)
    o_ref[0] = jnp.dot(cond, w_ref[0], preferred_element_type=F32,
                       precision=lax.Precision.HIGHEST) + b_ref[0]


def _ada(c, w_ada, b_ada):
    depth, d, d6 = w_ada.shape
    b = c.shape[0]
    return pl.pallas_call(
        _ada_kernel,
        out_shape=jax.ShapeDtypeStruct((depth, b, d6), F32),
        grid=(depth, d6 // d),
        in_specs=[pl.BlockSpec((b, d), lambda l, j: (0, 0)),
                  pl.BlockSpec((1, d, d), lambda l, j: (l, 0, j)),
                  pl.BlockSpec((1, 1, d), lambda l, j: (l, 0, j))],
        out_specs=pl.BlockSpec((1, b, d), lambda l, j: (l, 0, j)),
        compiler_params=_cparams(("arbitrary", "arbitrary")),
    )(c, w_ada, b_ada.reshape(depth, 1, d6))


def _prep_tile(x, sh_ref, sc_ref, wab_ref, wal_ref, bal_ref, la_ref, u_scr):
    u = _standardize(x) * (1.0 + sc_ref[0]) + sh_ref[0]
    ub = u.astype(BF16)
    u_scr[...] = ub
    ab = jnp.dot(ub, wab_ref[...], preferred_element_type=F32)
    pre = jnp.dot(ab.astype(BF16), wal_ref[...], preferred_element_type=F32) + bal_ref[...]
    la_ref[...] = (jnp.minimum(pre, 0.0) - jnp.log(1.0 + jnp.exp(-jnp.abs(pre)))) * (1.0 / GLA_TAU)


def _inproj_kernel(x_ref, sh_ref, sc_ref, wa_ref, wg_ref, wab_ref, wal_ref, bal_ref,
                   p_ref, la_ref, u_scr, *, n_a):
    j = pl.program_id(1)

    @pl.when(j == 0)
    def _():
        _prep_tile(x_ref[...], sh_ref, sc_ref, wab_ref, wal_ref, bal_ref, la_ref, u_scr)

    @pl.when(j < n_a)
    def _():
        p_ref[...] = jnp.dot(u_scr[...], wa_ref[...], preferred_element_type=F32).astype(BF16)

    @pl.when(j >= n_a)
    def _():
        p_ref[...] = jnp.dot(u_scr[...], wg_ref[...], preferred_element_type=F32).astype(BF16)


def _inproj(x2, sh, sc, w_att, w_gates, w_ab, w_alpha, b_alpha, seq, tm=1024, tn=2048):
    n, d = x2.shape
    tpb = seq // tm
    wq = w_alpha.shape[1]
    n_a = w_att.shape[1] // tn
    return pl.pallas_call(
        functools.partial(_inproj_kernel, n_a=n_a),
        out_shape=(jax.ShapeDtypeStruct((n, W_MAIN), BF16),
                   jax.ShapeDtypeStruct((n, wq), F32)),
        grid=(n // tm, W_MAIN // tn),
        in_specs=[pl.BlockSpec((tm, d), lambda i, j: (i, 0)),
                  pl.BlockSpec((1, 1, d), lambda i, j: (i // tpb, 0, 0)),
                  pl.BlockSpec((1, 1, d), lambda i, j: (i // tpb, 0, 0)),
                  pl.BlockSpec((d, tn), lambda i, j: (0, jnp.minimum(j, n_a - 1))),
                  pl.BlockSpec((d, tn), lambda i, j: (0, jnp.maximum(j - n_a, 0))),
                  pl.BlockSpec((d, LANES), lambda i, j: (0, 0)),
                  pl.BlockSpec((LANES, wq), lambda i, j: (0, 0)),
                  pl.BlockSpec((1, wq), lambda i, j: (0, 0))],
        out_specs=(pl.BlockSpec((tm, tn), lambda i, j: (i, j)),
                   pl.BlockSpec((tm, wq), lambda i, j: (i, 0))),
        scratch_shapes=[pltpu.VMEM((tm, d), BF16)],
        compiler_params=_cparams(("arbitrary", "arbitrary")),
    )(x2, sh, sc, w_att, w_gates, w_ab, w_alpha, b_alpha)


ONES_ROWS = 16


QUERY_GROUP = 256
POS_SPLIT = 32


def _attn_kernel(slopes_ref, q_ref, k_ref, v_ref, lq1_ref, lk1_ref, lq2_ref, lk2_ref, g_ref,
                 o_ref, kaug_scr, vt_scr, dmask_scr, sa_scr, sb_scr, acc_scr, m_scr, *, tk, lam_init):
    h = pl.program_id(1)
    qi = pl.program_id(2)
    slope = slopes_ref[h]
    dh = HEAD_DIM_DIFF
    dv = LANES
    n_chunks = vt_scr.shape[0]
    halves = q_ref.shape[0] // tk

    @pl.when(qi == 0)
    def _():
        for j in range(n_chunks):
            vt_scr[j, 0:dv, :] = v_ref[j * tk:(j + 1) * tk, :].astype(F32).T.astype(BF16)
            vt_scr[j, dv:dv + ONES_ROWS, :] = jnp.ones((ONES_ROWS, tk), BF16)
        kaug_scr[:, 0:LANES] = k_ref[...]
        koff = lax.broadcasted_iota(jnp.int32, (tk, LANES), 0)
        flane = lax.broadcasted_iota(jnp.int32, (tk, LANES), 1)
        feat = jnp.where(flane == 0, koff // POS_SPLIT, jnp.where(flane == 1, koff % POS_SPLIT, 0))
        feat = feat.astype(F32).astype(BF16)
        for j in range(n_chunks):
            kaug_scr[j * tk:(j + 1) * tk, LANES:2 * LANES] = feat
        kpos = lax.broadcasted_iota(jnp.int32, (tk, 2 * tk), 0)
        qpos = lax.broadcasted_iota(jnp.int32, (tk, 2 * tk), 1)
        qoff = jnp.where(qpos >= tk, qpos - tk, qpos)
        dmask_scr[...] = jnp.where(qoff >= kpos, 0.0, NEG_BIG)

    q = q_ref[...] * jnp.asarray(dh ** -0.5, BF16)
    lane = lax.broadcasted_iota(jnp.int32, q.shape, 1)
    zero = jnp.zeros_like(q)
    qf = jnp.where(lane == 0, slope * POS_SPLIT, jnp.where(lane == 1, slope, 0.0)).astype(BF16)
    q1 = jnp.concatenate([jnp.where(lane < dh, q, zero), qf], axis=1)
    q2 = jnp.concatenate([jnp.where(lane >= dh, q, zero), qf], axis=1)
    qq = jnp.concatenate([part[hq * tk:(hq + 1) * tk] for hq in range(halves) for part in (q1, q2)], axis=0)

    m_scr[...] = jnp.full(m_scr.shape, NEG_BIG, F32)
    acc_scr[...] = jnp.zeros(acc_scr.shape, F32)

    per_half = 2 * tk // QUERY_GROUP
    groups = [(hq, slice((hq * per_half + g) * QUERY_GROUP, (hq * per_half + g + 1) * QUERY_GROUP),
               slice(g * QUERY_GROUP, (g + 1) * QUERY_GROUP))
              for hq in range(halves) for g in range(per_half)]

    def score(kaug, ls):
        return lax.dot_general(kaug, qq[ls], (((1,), (1,)), ((), ())), preferred_element_type=F32)

    def step(j_cur, src, j_next, dst, first_half, masked_half, next_first_half=0):
        if dst is not None:
            kaug = kaug_scr[pl.ds(pl.multiple_of(j_next * tk, tk), tk), :]
        vt = vt_scr[j_cur]
        m_all = m_scr[...]
        acc_all = acc_scr[...]
        done = []
        for hq, ls, ms in groups:
            if dst is not None and hq >= next_first_half:
                dst[:, ls] = score(kaug, ls)
            if hq < first_half:
                continue
            cj = (-slope) * ((halves * qi + hq - j_cur) * tk).astype(F32)
            s = src[:, ls]
            if hq == masked_half:
                s = s + dmask_scr[:, ms]
            m_prev = m_all[:, ls]
            m_new = jnp.maximum(m_prev, jnp.max(s, axis=0, keepdims=True) + cj)
            p = jnp.exp(s - (m_new - cj))
            alpha = jnp.exp(m_prev - m_new)
            done.append((ls, m_new, alpha * acc_all[:, ls] + jnp.dot(vt, p.astype(BF16),
                                                                     preferred_element_type=F32)))
        for ls, m_new, acc_new in done:
            m_scr[:, ls] = m_new
            acc_scr[:, ls] = acc_new

    kaug0 = kaug_scr[0:tk, :]
    for _, ls, _ in groups:
        sa_scr[:, ls] = score(kaug0, ls)

    def pair(jj, carry):
        j = 2 * jj
        step(j, sa_scr, j + 1, sb_scr, 0, None)
        step(j + 1, sb_scr, j + 2, sa_scr, 0, None)
        return carry

    lax.fori_loop(0, qi, pair, 0)
    step(2 * qi, sa_scr, 2 * qi + 1, sb_scr, 0, 0, next_first_half=1)
    step(2 * qi + 1, sb_scr, None, None, 1, 1)

    lam = (jnp.exp(jnp.sum(lq1_ref[...] * lk1_ref[...], axis=-1, keepdims=True))
           - jnp.exp(jnp.sum(lq2_ref[...] * lk2_ref[...], axis=-1, keepdims=True)) + lam_init)
    acc = acc_scr[...]
    ot = acc[0:dv] / acc[dv:dv + 1]
    for hq in range(halves):
        base = hq * 2 * tk
        o = (ot[:, base:base + tk] - lam * ot[:, base + tk:base + 2 * tk]).T
        o = o * lax.rsqrt(jnp.mean(o * o, axis=-1, keepdims=True) + LN_EPS) * g_ref[...] * (1.0 - lam_init)
        o_ref[hq * tk:(hq + 1) * tk, :] = o.astype(o_ref.dtype)


def _attn(p3, lq1, lk1, lq2, lk2, g, lam_init, tk=512):
    b, s, _ = p3.shape
    h = N_HEADS_DIFF
    tq = 2 * tk
    slopes = jnp.asarray(2.0 ** (-8.0 * np.arange(1, h + 1) / h), dtype=F32)
    vec = pl.BlockSpec((1, HEAD_DIM_DIFF), lambda bi, hi, qi, sl: (0, 0))
    return pl.pallas_call(
        functools.partial(_attn_kernel, tk=tk, lam_init=lam_init),
        out_shape=jax.ShapeDtypeStruct((b, s, h * LANES), BF16),
        grid_spec=pltpu.PrefetchScalarGridSpec(
            num_scalar_prefetch=1,
            grid=(b, h, s // tq),
            in_specs=[pl.BlockSpec((None, tq, LANES), lambda bi, hi, qi, sl: (bi, qi, OFF_QA // LANES + hi)),
                      pl.BlockSpec((None, s, LANES), lambda bi, hi, qi, sl: (bi, 0, OFF_KA // LANES + hi)),
                      pl.BlockSpec((None, s, LANES), lambda bi, hi, qi, sl: (bi, 0, OFF_VA // LANES + hi)),
                      vec, vec, vec, vec,
                      pl.BlockSpec((1, LANES), lambda bi, hi, qi, sl: (0, 0))],
            out_specs=pl.BlockSpec((None, tq, LANES), lambda bi, hi, qi, sl: (bi, qi, hi)),
            scratch_shapes=[pltpu.VMEM((s, 2 * LANES), BF16),
                            pltpu.VMEM((s // tk, LANES + ONES_ROWS, tk), BF16),
                            pltpu.VMEM((tk, 2 * tk), F32),
                            pltpu.VMEM((tk, 2 * tq), F32),
                            pltpu.VMEM((tk, 2 * tq), F32),
                            pltpu.VMEM((LANES + ONES_ROWS, 2 * tq), F32),
                            pltpu.VMEM((1, 2 * tq), F32)]),
        compiler_params=_cparams(("arbitrary", "arbitrary", "arbitrary")),
    )(slopes, p3, p3, p3, lq1, lk1, lq2, lk2, g)


GLA_HEADS_PER_STEP = 4


def _gla_kernel(q_ref, k_ref, v_ref, gb_ref, la_ref, g_ref, o_ref, state_scr, *, tt):
    c = GLA_CHUNK
    dk, dv = KEY_DIM_GLA, VAL_DIM_GLA
    hp = state_scr.shape[0]
    n_c = tt // c
    work = [(hh, ci) for hh in range(hp) for ci in range(n_c)]

    @pl.when(pl.program_id(2) == 0)
    def _():
        state_scr[...] = jnp.zeros(state_scr.shape, F32)

    rr = lax.broadcasted_iota(jnp.int32, (c, c), 0)
    cr = lax.broadcasted_iota(jnp.int32, (c, c), 1)
    causal = cr <= rr
    tri = jnp.where(causal, 1.0, 0.0).astype(BF16)
    qscale = dk ** -0.5
    rows = lambda ci: slice(ci * c, (ci + 1) * c)
    kcols = lambda hh: slice(hh * dk, (hh + 1) * dk)
    vcols = lambda hh: slice(hh * dv, (hh + 1) * dv)

    cum = {}
    for hh, ci in work:
        la = la_ref[rows(ci), kcols(hh)]
        la_hi = la.astype(BF16)
        la_lo = (la - la_hi.astype(F32)).astype(BF16)
        cum2 = jnp.dot(tri, jnp.concatenate([la_hi, la_lo], axis=1), preferred_element_type=F32)
        cum[hh, ci] = cum2[:, :dk] + cum2[:, dk:]

    qe, q2, k2, kd, dec, v = {}, {}, {}, {}, {}, {}
    for hh, ci in work:
        cum_c = cum[hh, ci]
        last = cum_c[c - 1:c]
        mid = cum_c[c // 2:c // 2 + 1]
        q_c = q_ref[rows(ci), kcols(hh)].astype(F32) * qscale
        k_c = k_ref[rows(ci), kcols(hh)].astype(F32)
        qe[hh, ci] = (q_c * jnp.exp(cum_c)).astype(BF16)
        q2[hh, ci] = (q_c * jnp.exp(cum_c - mid)).astype(BF16)
        k2[hh, ci] = (k_c * jnp.exp(mid - cum_c)).astype(BF16)
        kd[hh, ci] = (k_c * jnp.exp(last - cum_c)).astype(BF16)
        dec[hh, ci] = jnp.exp(last)
        v[hh, ci] = v_ref[rows(ci), vcols(hh)]

    att = {w: lax.dot_general(q2[w], k2[w], (((1,), (1,)), ((), ())), preferred_element_type=F32)
           for w in work}
    o_intra = {w: jnp.dot(jnp.where(causal, att[w], 0.0).astype(BF16), v[w], preferred_element_type=F32)
               for w in work}
    kv = {w: jnp.dot(v[w].astype(F32).T.astype(BF16), kd[w], preferred_element_type=F32) for w in work}

    o_inter = {}
    for hh in range(hp):
        st = state_scr[hh]
        for ci in range(n_c):
            o_inter[hh, ci] = lax.dot_general(qe[hh, ci], st.astype(BF16), (((1,), (1,)), ((), ())),
                                              preferred_element_type=F32)
            st = st * dec[hh, ci] + kv[hh, ci]
        state_scr[hh] = st

    for hh, ci in work:
        o = o_inter[hh, ci] + o_intra[hh, ci]
        o = o * lax.rsqrt(jnp.mean(o * o, axis=-1, keepdims=True) + LN_EPS) * g_ref[...]
        gate = gb_ref[rows(ci), vcols(hh)].astype(F32)
        o_ref[rows(ci), vcols(hh)] = (o * (gate * _sigmoid(gate))).astype(o_ref.dtype)


def _gla(p3, la3, g, tt=512):
    b, s, _ = p3.shape
    hp = GLA_HEADS_PER_STEP
    dk, dv = KEY_DIM_GLA * hp, VAL_DIM_GLA * hp
    return pl.pallas_call(
        functools.partial(_gla_kernel, tt=tt),
        out_shape=jax.ShapeDtypeStruct((b, s, N_HEADS_GLA * VAL_DIM_GLA), BF16),
        grid=(b, N_HEADS_GLA // hp, s // tt),
        in_specs=[pl.BlockSpec((None, tt, dk), lambda bi, hi, ti: (bi, ti, OFF_QB // dk + hi)),
                  pl.BlockSpec((None, tt, dk), lambda bi, hi, ti: (bi, ti, OFF_KB // dk + hi)),
                  pl.BlockSpec((None, tt, dv), lambda bi, hi, ti: (bi, ti, OFF_VB // dv + hi)),
                  pl.BlockSpec((None, tt, dv), lambda bi, hi, ti: (bi, ti, OFF_GB // dv + hi)),
                  pl.BlockSpec((None, tt, dk), lambda bi, hi, ti: (bi, ti, hi)),
                  pl.BlockSpec((1, VAL_DIM_GLA), lambda bi, hi, ti: (0, 0))],
        out_specs=pl.BlockSpec((None, tt, dv), lambda bi, hi, ti: (bi, ti, hi)),
        scratch_shapes=[pltpu.VMEM((hp, VAL_DIM_GLA, KEY_DIM_GLA), F32)],
        compiler_params=_cparams(("arbitrary", "arbitrary", "arbitrary")),
    )(p3, p3, p3, p3, la3, g)


def _mix_kernel(oa_ref, ob_ref, gt_ref, x_ref, bg_ref, g1_ref, sh2_ref, sc2_ref, lng_ref, lnb_ref,
                wba_ref, wbb_ref, wo_ref, wr_ref, br_ref,
                x1_ref, u2_ref, lg_ref, *, alpha):
    d = x_ref.shape[-1]
    a = jnp.dot(oa_ref[...], wba_ref[...], preferred_element_type=F32)
    bm = jnp.dot(ob_ref[...], wbb_ref[...], preferred_element_type=F32)
    gates = _sigmoid(gt_ref[...].astype(F32) + bg_ref[...])
    mixed = gates[:, :d] * a + gates[:, d:] * bm
    y = jnp.dot(mixed.astype(BF16), wo_ref[...], preferred_element_type=F32)
    x1 = _standardize(alpha * x_ref[...] + g1_ref[0] * y) * lng_ref[...] + lnb_ref[...]
    x1_ref[...] = x1
    u2 = _standardize(x1) * (1.0 + sc2_ref[0]) + sh2_ref[0]
    _store_token_tiles(u2_ref, u2)
    uh = u2.astype(BF16)
    ul = (u2 - uh.astype(F32)).astype(BF16)
    hh = jnp.dot(uh, wr_ref[...], preferred_element_type=F32)
    lg_ref[...] = (hh[:, :LANES] + hh[:, LANES:]
                   + jnp.dot(ul, wr_ref[:, 0:LANES], preferred_element_type=F32) + br_ref[...])


def _mix(oa, ob, p, x2, bg, g1, sh2, sc2, lng, lnb, wba, wbb, wo, wr2, br, seq, alpha, tm=512):
    n, d = x2.shape
    tpb = seq // tm
    row = lambda i: (i, 0)
    const = lambda i: (0, 0)
    per_b = pl.BlockSpec((1, 1, d), lambda i: (i // tpb, 0, 0))
    return pl.pallas_call(
        functools.partial(_mix_kernel, alpha=alpha),
        out_shape=(jax.ShapeDtypeStruct((n, d), F32),
                   jax.ShapeDtypeStruct((n * (d // LANES), LANES), F32),
                   jax.ShapeDtypeStruct((n, LANES), F32)),
        grid=(n // tm,),
        in_specs=[pl.BlockSpec((tm, d), row), pl.BlockSpec((tm, d), row),
                  pl.BlockSpec((tm, 2 * d), lambda i: (i, OFF_GATES // (2 * d))),
                  pl.BlockSpec((tm, d), row),
                  pl.BlockSpec((1, 2 * d), const), per_b, per_b, per_b,
                  pl.BlockSpec((1, d), const), pl.BlockSpec((1, d), const),
                  pl.BlockSpec((d, d), const), pl.BlockSpec((d, d), const), pl.BlockSpec((d, d), const),
                  pl.BlockSpec((d, 2 * LANES), const),
                  pl.BlockSpec((1, LANES), const)],
        out_specs=(pl.BlockSpec((tm, d), row), pl.BlockSpec((tm * (d // LANES), LANES), row),
                   pl.BlockSpec((tm, LANES), row)),
        compiler_params=_cparams(("arbitrary",)),
    )(oa, ob, p, x2, bg, g1, sh2, sc2, lng, lnb, wba, wbb, wo, wr2, br)


def _route_kernel(lg_ref, ri_ref, rw_ref, cnt_ref, *, tm):
    @pl.when(pl.program_id(0) == 0)
    def _():
        cnt_ref[...] = jnp.zeros(cnt_ref.shape, F32)

    lg = lg_ref[...]
    lane = lax.broadcasted_iota(jnp.int32, lg.shape, 1)
    lanef = lane.astype(F32)
    far = float(LANES)
    is_g = lane < N_GROUPS
    gl = jnp.where(is_g, lg, NEG_BIG)
    gmax = jnp.max(gl, axis=-1, keepdims=True)
    gidx = jnp.min(jnp.where(gl == gmax, lanef, far), axis=-1, keepdims=True)
    gw = 1.0 / jnp.sum(jnp.where(is_g, jnp.exp(gl - gmax), 0.0), axis=-1, keepdims=True)
    lo = N_GROUPS + gidx * EXPERTS_PER_GROUP
    in_g = (lanef >= lo) & (lanef < lo + EXPERTS_PER_GROUP)
    el = jnp.where(in_g, lg, NEG_BIG)
    v1 = jnp.max(el, axis=-1, keepdims=True)
    i1 = jnp.min(jnp.where(in_g & (el == v1), lanef, far), axis=-1, keepdims=True)
    in_g2 = in_g & (lanef != i1)
    el2 = jnp.where(in_g2, lg, NEG_BIG)
    v2 = jnp.max(el2, axis=-1, keepdims=True)
    i2 = jnp.min(jnp.where(in_g2 & (el2 == v2), lanef, far), axis=-1, keepdims=True)
    t = jnp.exp(v2 - v1)
    w1 = gw / (1.0 + t)
    w2 = gw * t / (1.0 + t)

    oh1 = lanef == i1
    oh2 = lanef == i2
    oh = jnp.where(oh1 | oh2, 1.0, 0.0)
    r = lax.broadcasted_iota(jnp.int32, (tm, tm), 0)
    c = lax.broadcasted_iota(jnp.int32, (tm, tm), 1)
    lower = jnp.where(c < r, 1.0, 0.0).astype(BF16)
    base = jnp.dot(lower, oh.astype(BF16), preferred_element_type=F32) + cnt_ref[0:1, :]
    r1 = jnp.sum(jnp.where(oh1, base, 0.0), axis=-1, keepdims=True)
    r2 = jnp.sum(jnp.where(oh2, base, 0.0), axis=-1, keepdims=True)
    cnt_ref[...] = cnt_ref[...] + jnp.sum(oh, axis=0, keepdims=True)

    e1 = i1 - float(N_GROUPS)
    e2 = i2 - float(N_GROUPS)
    ri = jnp.where(lane == 0, e1, jnp.where(lane == 1, e2, jnp.where(lane == 2, r1, jnp.where(lane == 3, r2, 0.0))))
    ri_ref[...] = ri.astype(jnp.int32)
    rw_ref[...] = jnp.where(lane == 0, w1, jnp.where(lane == 1, w2, 0.0))


def _route(lg, tm=512):
    n = lg.shape[0]
    row = lambda i: (i, 0)
    return pl.pallas_call(
        functools.partial(_route_kernel, tm=tm),
        out_shape=(jax.ShapeDtypeStruct((n, LANES), jnp.int32),
                   jax.ShapeDtypeStruct((n, LANES), F32),
                   jax.ShapeDtypeStruct((8, LANES), F32)),
        grid=(n // tm,),
        in_specs=[pl.BlockSpec((tm, LANES), row)],
        out_specs=(pl.BlockSpec((tm, LANES), row), pl.BlockSpec((tm, LANES), row),
                   pl.BlockSpec((8, LANES), lambda i: (0, 0))),
        compiler_params=_cparams(("arbitrary",)),
    )(lg)


MOE_ROWS = 256


def _expert_kernel(be_ref, nu_ref, tok_a_ref, tok_b_ref, dst_ref, u_ref, wg_ref, wu_ref, wd_ref, y_ref,
                   xb0, xb1, yb0, yb1, wg_scr, wu_scr, wd_scr, gsem, ssem):
    i = pl.program_id(0)
    n_used = nu_ref[0]
    blk = tok_a_ref.shape[-1]
    xbs, ybs = (xb0, xb1), (yb0, yb1)

    def tile(ref, t):
        return ref.at[pl.ds(pl.multiple_of(t * ROW_TILES, ROW_TILES), ROW_TILES)]

    def gather_issue(tok_ref, xb, sem):
        for r in range(blk):
            pltpu.make_async_copy(tile(u_ref, tok_ref[0, 0, r]), tile(xb, r), sem).start()

    def gather_wait(xb, sem):
        pltpu.make_async_copy(u_ref.at[pl.ds(0, blk * ROW_TILES)], xb, sem).wait()

    def scatter_issue(yb, sem):
        for r in range(blk):
            pltpu.make_async_copy(tile(yb, r), tile(y_ref, dst_ref[0, 0, r]), sem).start()

    def scatter_wait(yb, sem):
        pltpu.make_async_copy(yb, y_ref.at[pl.ds(0, blk * ROW_TILES)], sem).wait()

    def load_weights():
        @pl.when((i == 0) | (be_ref[i] != be_ref[jnp.maximum(i - 1, 0)]))
        def _():
            wg_scr[...] = wg_ref[0].astype(BF16)
            wu_scr[...] = wu_ref[0].astype(BF16)
            wd_scr[...] = wd_ref[0].astype(BF16)

    def compute(xb, yb):
        x = _load_token_tiles(xb).astype(BF16)
        g = jnp.dot(x, wg_scr[...], preferred_element_type=F32)
        u = jnp.dot(x, wu_scr[...], preferred_element_type=F32)
        hid = (g * _sigmoid(g) * u).astype(BF16)
        _store_token_tiles(yb, jnp.dot(hid, wd_scr[...], preferred_element_type=F32))

    @pl.when(i == 0)
    def _():
        gather_issue(tok_a_ref, xb0, gsem.at[0])
        gather_wait(xb0, gsem.at[0])
        load_weights()
        gather_issue(tok_b_ref, xb1, gsem.at[1])
        compute(xb0, yb0)

    for p in range(2):
        cur, oth = p, 1 - p

        @pl.when((i >= 1) & (i < n_used) & (i % 2 == p))
        def _():
            gather_wait(xbs[cur], gsem.at[cur])

            @pl.when(i >= 2)
            def _():
                scatter_wait(ybs[cur], ssem.at[cur])

            load_weights()
            gather_issue(tok_b_ref, xbs[oth], gsem.at[oth])
            scatter_issue(ybs[oth], ssem.at[oth])
            compute(xbs[cur], ybs[cur])

        @pl.when((i == n_used) & (i % 2 == p))
        def _():
            gather_wait(xbs[cur], gsem.at[cur])

            @pl.when(i >= 2)
            def _():
                scatter_wait(ybs[cur], ssem.at[cur])

            scatter_issue(ybs[oth], ssem.at[oth])

        @pl.when((i == n_used + 1) & (i % 2 == p))
        def _():
            scatter_wait(ybs[cur], ssem.at[cur])


def _experts(blk_e, n_used, src_tok3, dst_row3, u2t, wg, wu, wd, layer, n_rows):
    d, de = wg.shape[-2:]
    blk = MOE_ROWS
    rt = d // LANES
    n_steps = blk_e.shape[0]
    wspec = lambda r, c: pl.BlockSpec((None, 1, r, c), lambda i, be, nu: (layer, be[i], 0, 0))
    ispec = lambda f: pl.BlockSpec((1, 1, blk), lambda i, be, nu: (f(i, nu), 0, 0), memory_space=pltpu.SMEM)
    return pl.pallas_call(
        _expert_kernel,
        out_shape=jax.ShapeDtypeStruct((n_rows * rt, LANES), F32),
        grid_spec=pltpu.PrefetchScalarGridSpec(
            num_scalar_prefetch=2,
            grid=(n_steps,),
            in_specs=[ispec(lambda i, nu: jnp.minimum(i, nu[0] - 1)),
                      ispec(lambda i, nu: jnp.minimum(i + 1, nu[0] - 1)),
                      ispec(lambda i, nu: jnp.clip(i - 1, 0, nu[0] - 1)),
                      pl.BlockSpec(memory_space=pl.ANY),
                      wspec(d, de), wspec(d, de), wspec(de, d)],
            out_specs=pl.BlockSpec(memory_space=pl.ANY),
            scratch_shapes=[pltpu.VMEM((blk * rt, LANES), F32), pltpu.VMEM((blk * rt, LANES), F32),
                            pltpu.VMEM((blk * rt, LANES), F32), pltpu.VMEM((blk * rt, LANES), F32),
                            pltpu.VMEM((d, de), BF16), pltpu.VMEM((d, de), BF16),
                            pltpu.VMEM((de, d), BF16),
                            pltpu.SemaphoreType.DMA((2,)), pltpu.SemaphoreType.DMA((2,))]),
        compiler_params=_cparams(("arbitrary",), has_side_effects=True),
    )(blk_e, n_used, src_tok3, src_tok3, dst_row3, u2t, wg, wu, wd)


def _combine_kernel(x1_ref, y1_ref, y2_ref, rw_ref, g2_ref, lng_ref, lnb_ref, o_ref, *, alpha):
    rw = rw_ref[...]
    y = rw[:, 0:1] * _load_token_tiles(y1_ref) + rw[:, 1:2] * _load_token_tiles(y2_ref)
    o_ref[...] = _standardize(alpha * x1_ref[...] + g2_ref[0] * y) * lng_ref[...] + lnb_ref[...]


def _combine(x1, y_tok, rw, g2, lng, lnb, seq, alpha, tm=512):
    n, d = x1.shape
    tpb = seq // tm
    const = lambda i: (0, 0)
    return pl.pallas_call(
        functools.partial(_combine_kernel, alpha=alpha),
        out_shape=jax.ShapeDtypeStruct((n, d), F32),
        grid=(n // tm,),
        in_specs=[pl.BlockSpec((tm, d), lambda i: (i, 0)),
                  pl.BlockSpec((tm * (d // LANES), LANES), lambda i: (i, 0)),
                  pl.BlockSpec((tm * (d // LANES), LANES), lambda i: (n // tm + i, 0)),
                  pl.BlockSpec((tm, LANES), lambda i: (i, 0)),
                  pl.BlockSpec((1, 1, d), lambda i: (i // tpb, 0, 0)),
                  pl.BlockSpec((1, d), const), pl.BlockSpec((1, d), const)],
        out_specs=pl.BlockSpec((tm, d), lambda i: (i, 0)),
        compiler_params=_cparams(("arbitrary",)),
    )(x1, y_tok, y_tok, rw, g2, lng, lnb)


ID_SPLIT = 256


def _invert_kernel(ri_ref, ps_ref, inv_ref, *, n):
    i = pl.program_id(0)
    tm = ri_ref.shape[0]
    n_hi = inv_ref.shape[0]

    @pl.when(i == 0)
    def _():
        inv_ref[...] = jnp.zeros(inv_ref.shape, F32)

    ri = ri_ref[...].astype(F32)
    lane = lax.broadcasted_iota(jnp.int32, (tm, LANES), 1)
    lanef = lane.astype(F32)
    tok = (lax.broadcasted_iota(jnp.int32, (tm, 1), 0) + i * tm).astype(F32)
    cols = []
    onehot_lo = []
    for j in range(2):
        e = ri[:, j:j + 1]
        row = jnp.sum(jnp.where(lanef == e, ps_ref[...], 0.0), axis=-1, keepdims=True) + ri[:, 2 + j:3 + j]
        hi = jnp.floor(row * (1.0 / LANES))
        lo = row - hi * LANES
        ident = tok + float(j * n + 1)
        id_hi = jnp.floor(ident * (1.0 / ID_SPLIT))
        cols += [hi, id_hi, ident - id_hi * ID_SPLIT]
        onehot_lo.append(jnp.where(lanef == lo, 1.0, 0.0).astype(BF16))
    packed = jnp.zeros((tm, LANES), F32)
    for k, col in enumerate(cols):
        packed = jnp.where(lane == k, col, packed)
    rows = packed.T
    hsel = lax.broadcasted_iota(jnp.int32, (n_hi, tm), 0).astype(F32)
    lhs_hi, lhs_lo = [], []
    for j in range(2):
        hit = hsel == rows[3 * j:3 * j + 1]
        lhs_hi.append(jnp.where(hit, rows[3 * j + 1:3 * j + 2], 0.0).astype(BF16))
        lhs_lo.append(jnp.where(hit, rows[3 * j + 2:3 * j + 3], 0.0).astype(BF16))
    rhs = jnp.concatenate(onehot_lo, axis=0)
    inv_ref[...] += (float(ID_SPLIT) * jnp.dot(jnp.concatenate(lhs_hi, axis=1), rhs, preferred_element_type=F32)
                     + jnp.dot(jnp.concatenate(lhs_lo, axis=1), rhs, preferred_element_type=F32))


def _invert(ri, pstarts, n_rows, tm=1024):
    n = ri.shape[0]
    assert n % tm == 0 and n_rows % LANES == 0
    ps = jnp.pad(pstarts.astype(F32), (0, LANES - pstarts.shape[0])).reshape(1, LANES)
    return pl.pallas_call(
        functools.partial(_invert_kernel, n=n),
        out_shape=jax.ShapeDtypeStruct((n_rows // LANES, LANES), F32),
        grid=(n // tm,),
        in_specs=[pl.BlockSpec((tm, LANES), lambda i: (i, 0)),
                  pl.BlockSpec((1, LANES), lambda i: (0, 0))],
        out_specs=pl.BlockSpec((n_rows // LANES, LANES), lambda i: (0, 0)),
        compiler_params=_cparams(("arbitrary",)),
    )(ri, ps)


def _moe_plan(ri, cnt, n):
    blk = MOE_ROWS
    n_rows = 2 * n + N_EXPERTS * blk
    n_blocks = n_rows // blk
    n_steps = n_blocks + 2
    counts = cnt[0, N_GROUPS:N_GROUPS + N_EXPERTS].astype(jnp.int32)
    padded = ((counts + blk - 1) // blk) * blk
    pends = jnp.cumsum(padded)
    pstarts = pends - padded
    blk_start = jnp.arange(n_steps, dtype=jnp.int32) * blk
    blk_e = jnp.minimum(jnp.sum(blk_start[:, None] >= pends[None, :], axis=1), N_EXPERTS - 1).astype(jnp.int32)
    n_used = (pends[-1:] // blk).astype(jnp.int32)
    assigned_upto = jnp.cumsum(counts)[blk_e[:n_blocks]]
    spare = (2 * n + jnp.arange(n_rows, dtype=jnp.int32).reshape(n_blocks, blk) - assigned_upto[:, None]).reshape(-1)
    inv = _invert(ri, pstarts, n_rows).reshape(-1).astype(jnp.int32)
    dst_row = jnp.where(inv > 0, inv - 1, spare)
    src_tok = dst_row % n
    return blk_e, n_used, src_tok.reshape(n_blocks, 1, blk), dst_row.reshape(n_blocks, 1, blk)


def kernel(x, c, w_ada, b_ada, w_in, b_gates, w_alpha, b_alpha, lambda_q1, lambda_k1, lambda_q2, lambda_k2, diff_norm_g, gla_norm_g, w_branch_a, w_branch_b, w_out, ln1_g, ln1_b, w_router_g, b_router_g, w_router_e, b_router_e, w_gate_e, w_up_e, w_down_e, ln2_g, ln2_b):
    b, s, d = x.shape
    depth = w_ada.shape[0]
    n = b * s
    alpha = (2.0 * depth) ** 0.25

    ada = _ada(c, w_ada, b_ada)
    x2 = x.reshape(n, d)
    for l in range(depth):
        sh1, sc1, g1, sh2, sc2, g2 = [ada[l, :, i * d:(i + 1) * d].reshape(b, 1, d) for i in range(6)]
        wl = w_in[l]
        w_att = wl[:, :OFF_GATES].astype(BF16)
        w_gates = wl[:, OFF_GATES + GLA_RANK:].astype(BF16)
        w_ab = jnp.pad(wl[:, OFF_GATES:OFF_GATES + GLA_RANK], ((0, 0), (0, LANES - GLA_RANK))).astype(BF16)
        wal = jnp.pad(w_alpha[l], ((0, LANES - GLA_RANK), (0, 0))).astype(BF16)
        p, la = _inproj(x2, sh1, sc1, w_att, w_gates, w_ab, wal, b_alpha[l].reshape(1, -1), s)
        p3 = p.reshape(b, s, W_MAIN)

        lam_init = 0.8 - 0.6 * math.exp(-0.3 * l)
        oa = _attn(p3, lambda_q1[l].reshape(1, -1), lambda_k1[l].reshape(1, -1),
                   lambda_q2[l].reshape(1, -1), lambda_k2[l].reshape(1, -1),
                   diff_norm_g[l].reshape(1, -1), lam_init)
        ob = _gla(p3, la.reshape(b, s, -1), gla_norm_g[l].reshape(1, -1))

        wr = jnp.pad(jnp.concatenate([w_router_g[l], w_router_e[l]], axis=1),
                     ((0, 0), (0, LANES - N_GROUPS - N_EXPERTS)))
        wrh = wr.astype(BF16)
        wr2 = jnp.concatenate([wrh, (wr - wrh.astype(F32)).astype(BF16)], axis=1)
        br = jnp.pad(jnp.concatenate([b_router_g[l], b_router_e[l]]),
                     (0, LANES - N_GROUPS - N_EXPERTS)).reshape(1, LANES)
        x1, u2, lg = _mix(oa.reshape(n, d), ob.reshape(n, d), p, x2, b_gates[l].reshape(1, -1),
                          g1, sh2, sc2, ln1_g[l].reshape(1, d), ln1_b[l].reshape(1, d),
                          w_branch_a[l].astype(BF16), w_branch_b[l].astype(BF16), w_out[l].astype(BF16),
                          wr2, br, s, alpha)

        ri, rw, cnt = _route(lg)
        blk_e, n_used, src_tok3, dst_row3 = _moe_plan(ri, cnt, n)
        y_tok = _experts(blk_e, n_used, src_tok3, dst_row3, u2, w_gate_e, w_up_e, w_down_e, l,
                         2 * n + N_EXPERTS * MOE_ROWS)
        x2 = _combine(x1, y_tok, rw, g2, ln2_g[l].reshape(1, d), ln2_b[l].reshape(1, d), s, alpha)
    return x2.reshape(b, s, d)
```
